```python
import math
import jax, jax.numpy as jnp
from jax import lax
import numpy as np

D_MODEL = 1024
BATCH = 8
SEQ = 2048
DEPTH = 1
DEC_BATCH = 16
DEC_SEQ = 16
PAST_LEN = 2048

CHUNK = 64
HEAD_DIM = 64
H_A = D_MODEL // (2 * HEAD_DIM)
HKV_A = 2
H_B = D_MODEL // (2 * HEAD_DIM)
H_IDX = 8
D_IDX = 64
K_TOP_MAX = 256
NUM_BUCKETS = 32
MAX_DISTANCE = 128
Q_BLOCK = 128
D_FF = 4 * D_MODEL
EPS = 1e-6
MIX_WIDTH = (H_A + H_B) * HEAD_DIM
FORGET_BIAS_INIT = 2.0
IN_SPLITS = (H_A * HEAD_DIM, HKV_A * HEAD_DIM, HKV_A * HEAD_DIM, H_IDX * D_IDX, D_IDX, H_IDX,
             H_B * HEAD_DIM, H_B * HEAD_DIM, H_B * HEAD_DIM, H_B)
N_IN = sum(IN_SPLITS)

kernel_name = "hybrid_dsa_fox_streaming_step"


def rmsnorm(x, g):
    xf = x.astype(jnp.float32)
    y = xf * lax.rsqrt(jnp.mean(xf * xf, axis=-1, keepdims=True) + EPS)
    return (y * g.astype(jnp.float32)).astype(x.dtype)


def t5_bucket(rel):
    half = NUM_BUCKETS // 2
    max_exact = half // 2
    ret = jnp.where(rel > 0, half, 0)
    n = jnp.abs(rel)
    n_f = jnp.maximum(n, max_exact).astype(jnp.float32)
    large = max_exact + (jnp.log(n_f / max_exact) / math.log(MAX_DISTANCE / max_exact)
                         * (half - max_exact)).astype(jnp.int32)
    large = jnp.minimum(large, half - 1)
    return ret + jnp.where(n < max_exact, n, large)


def sweep_queries(fn, q_arrays):
    b, tq = q_arrays[0].shape[:2]
    blk = min(Q_BLOCK, tq)
    nb = tq // blk
    xs = tuple(jnp.moveaxis(a.reshape((b, nb, blk) + a.shape[2:]), 1, 0) for a in q_arrays)
    out = lax.map(lambda qs: fn(*qs), xs)
    return jnp.moveaxis(out, 0, 1).reshape((b, tq) + out.shape[3:])


def dsa_block(q, iq, iw, qpos, k_all, v_all, ik_all, kpos, rel_bias, k_top):
    b, t = q.shape[:2]
    g = H_A // HKV_A
    dots = jnp.einsum('bthd,bsd->bths', iq.astype(jnp.float32), ik_all.astype(jnp.float32)) * (D_IDX ** -0.5)
    score = jnp.einsum('bth,bths->bts', iw.astype(jnp.float32) * (H_IDX ** -0.5), jax.nn.relu(dots))
    adm = (kpos[None, None, :] // CHUNK) <= (qpos[:, :, None] // CHUNK)
    score = jnp.where(adm, score, -jnp.inf)
    sel_score, sel = lax.top_k(score, k_top)
    valid = sel_score > -jnp.inf
    k_sel = jax.vmap(lambda kb, ib: kb[ib])(k_all, sel)
    v_sel = jax.vmap(lambda vb, ib: vb[ib])(v_all, sel)
    sel_pos = kpos[sel]
    bias = rel_bias.astype(jnp.float32)[t5_bucket(sel_pos - qpos[..., None])]
    bias = bias.reshape(b, t, k_top, HKV_A, g).transpose(0, 1, 3, 4, 2)
    qg = q.reshape(b, t, HKV_A, g, HEAD_DIM)
    logits = jnp.einsum('btkgd,btnkd->btkgn', qg, k_sel).astype(jnp.float32) * (HEAD_DIM ** -0.5) + bias
    logits = jnp.where(valid[:, :, None, None, :], logits, -jnp.inf)
    p = jax.nn.softmax(logits, axis=-1).astype(v_sel.dtype)
    out = jnp.einsum('btkgn,btnkd->btkgd', p, v_sel)
    return out.reshape(b, t, H_A * HEAD_DIM)


def fox_block(q, fq, qpos, k_all, v_all, fk, kpos):
    b, t = q.shape[:2]
    logits = jnp.einsum('bthd,bshd->bhts', q, k_all).astype(jnp.float32) * (HEAD_DIM ** -0.5)
    logits = logits + (fq.transpose(0, 2, 1)[:, :, :, None] - fk.transpose(0, 2, 1)[:, :, None, :])
    mask = kpos[None, None, None, :] <= qpos[:, None, :, None]
    logits = jnp.where(mask, logits, -jnp.inf)
    p = jax.nn.softmax(logits, axis=-1).astype(v_all.dtype)
    out = jnp.einsum('bhts,bshd->bthd', p, v_all)
    return out.reshape(b, t, H_B * HEAD_DIM)


def mixer_sublayer(h, past, w_in, w_o, b_f, rel_bias):
    pa_k, pa_v, pi_k, pb_k, pb_v, pb_logf = past
    b, t, _ = h.shape
    p_len = pa_k.shape[1]
    proj = h @ w_in
    offs = [int(v) for v in np.cumsum(IN_SPLITS)[:-1]]
    a_q, a_k, a_v, i_q, i_k, i_w, b_q, b_k, b_v, f_z = jnp.split(proj, offs, axis=-1)
    a_q = a_q.reshape(b, t, H_A, HEAD_DIM)
    a_k = a_k.reshape(b, t, HKV_A, HEAD_DIM)
    a_v = a_v.reshape(b, t, HKV_A, HEAD_DIM)
    i_q = i_q.reshape(b, t, H_IDX, D_IDX)
    b_q = b_q.reshape(b, t, H_B, HEAD_DIM)
    b_k = b_k.reshape(b, t, H_B, HEAD_DIM)
    b_v = b_v.reshape(b, t, H_B, HEAD_DIM)
    logf = jax.nn.log_sigmoid(f_z.astype(jnp.float32) + b_f.astype(jnp.float32))

    k_all_a = jnp.concatenate([pa_k, a_k], axis=1)
    v_all_a = jnp.concatenate([pa_v, a_v], axis=1)
    ik_all = jnp.concatenate([pi_k, i_k], axis=1)
    k_all_b = jnp.concatenate([pb_k, b_k], axis=1)
    v_all_b = jnp.concatenate([pb_v, b_v], axis=1)
    logf_all = jnp.concatenate([pb_logf.astype(jnp.float32), logf], axis=1)
    l_len = p_len + t
    kpos = jnp.arange(l_len, dtype=jnp.int32)
    qpos = jnp.broadcast_to(p_len + jnp.arange(t, dtype=jnp.int32), (b, t))
    k_top = min(K_TOP_MAX, l_len // 4)
    f_cum = jnp.cumsum(logf_all, axis=1)
    f_q = f_cum[:, p_len:]

    out_a = sweep_queries(
        lambda q, iq, iw, qp: dsa_block(q, iq, iw, qp, k_all_a, v_all_a, ik_all, kpos, rel_bias, k_top),
        (a_q, i_q, i_w, qpos))
    out_b = sweep_queries(
        lambda q, fq, qp: fox_block(q, fq, qp, k_all_b, v_all_b, f_cum, kpos),
        (b_q, f_q, qpos))
    mixed = jnp.concatenate([out_a, out_b], axis=-1) @ w_o
    new_rows = (a_k, a_v, i_k, b_k, b_v, logf.astype(h.dtype))
    return mixed, new_rows


def run_trunk(x, past_layers, w_in, w_o, b_f, rel_bias, g_attn, w_up, w_down, g_mlp, g_final):
    rows = []
    for l in range(DEPTH):
        h = rmsnorm(x, g_attn[l])
        mixed, new_rows = mixer_sublayer(h, past_layers[l], w_in[l], w_o[l], b_f[l], rel_bias)
        x = x + mixed
        h2 = rmsnorm(x, g_mlp[l])
        x = x + jnp.square(jax.nn.relu(h2 @ w_up[l])) @ w_down[l]
        rows.append(new_rows)
    y = rmsnorm(x, g_final)
    stacked = [jnp.stack([r[i] for r in rows], axis=0) for i in range(6)]
    return y, stacked


def setup_inputs(seed: int = 0) -> dict:
    key = jax.random.key(seed)
    ks = jax.random.split(key, 18)
    f32 = jnp.float32

    def nrm(k, shape, s):
        return jax.random.normal(k, shape, f32) * s

    return {
        "x_prompt": nrm(ks[0], (BATCH, SEQ, D_MODEL), 1.0),
        "x_sample": nrm(ks[1], (DEC_BATCH, DEC_SEQ, D_MODEL), 1.0),
        "cache_a_k": nrm(ks[2], (DEPTH, DEC_BATCH, PAST_LEN, HKV_A, HEAD_DIM), 1.0),
        "cache_a_v": nrm(ks[3], (DEPTH, DEC_BATCH, PAST_LEN, HKV_A, HEAD_DIM), 1.0),
        "cache_idx_k": nrm(ks[4], (DEPTH, DEC_BATCH, PAST_LEN, D_IDX), 1.0),
        "cache_b_k": nrm(ks[5], (DEPTH, DEC_BATCH, PAST_LEN, H_B, HEAD_DIM), 1.0),
        "cache_b_v": nrm(ks[6], (DEPTH, DEC_BATCH, PAST_LEN, H_B, HEAD_DIM), 1.0),
        "cache_b_logf": jax.nn.log_sigmoid(nrm(ks[7], (DEPTH, DEC_BATCH, PAST_LEN, H_B), 1.0) + FORGET_BIAS_INIT),
        "w_in": nrm(ks[8], (DEPTH, D_MODEL, N_IN), D_MODEL ** -0.5),
        "w_o": nrm(ks[9], (DEPTH, MIX_WIDTH, D_MODEL), MIX_WIDTH ** -0.5),
        "b_f": FORGET_BIAS_INIT + nrm(ks[10], (DEPTH, H_B), 0.1),
        "rel_bias": nrm(ks[11], (NUM_BUCKETS, H_A), 0.5),
        "g_attn": 1.0 + nrm(ks[12], (DEPTH, D_MODEL), 0.05),
        "w_up": nrm(ks[13], (DEPTH, D_MODEL, D_FF), D_MODEL ** -0.5),
        "w_down": nrm(ks[14], (DEPTH, D_FF, D_MODEL), D_FF ** -0.5),
        "g_mlp": 1.0 + nrm(ks[15], (DEPTH, D_MODEL), 0.05),
        "g_final": 1.0 + nrm(ks[16], (D_MODEL,), 0.05),
    }


def reference(x_prompt, x_sample, cache_a_k, cache_a_v, cache_idx_k, cache_b_k, cache_b_v, cache_b_logf,
              w_in, w_o, b_f, rel_bias, g_attn, w_up, w_down, g_mlp, g_final):
    dt = x_prompt.dtype
    bp = x_prompt.shape[0]
    empty = (jnp.zeros((bp, 0, HKV_A, HEAD_DIM), dt), jnp.zeros((bp, 0, HKV_A, HEAD_DIM), dt),
             jnp.zeros((bp, 0, D_IDX), dt), jnp.zeros((bp, 0, H_B, HEAD_DIM), dt),
             jnp.zeros((bp, 0, H_B, HEAD_DIM), dt), jnp.zeros((bp, 0, H_B), dt))
    past_prompt = [empty for _ in range(DEPTH)]
    past_sample = [(cache_a_k[l], cache_a_v[l], cache_idx_k[l], cache_b_k[l], cache_b_v[l], cache_b_logf[l])
                   for l in range(DEPTH)]
    y_prompt, rows_p = run_trunk(x_prompt, past_prompt, w_in, w_o, b_f, rel_bias, g_attn, w_up, w_down, g_mlp, g_final)
    y_sample, rows_s = run_trunk(x_sample, past_sample, w_in, w_o, b_f, rel_bias, g_attn, w_up, w_down, g_mlp, g_final)
    p_a_k, p_a_v, p_idx_k, p_b_k, p_b_v, p_b_logf = rows_p
    s_a_k, s_a_v, s_idx_k, s_b_k, s_b_v, s_b_logf = rows_s
    return (y_prompt, y_sample, p_a_k, p_a_v, p_idx_k, p_b_k, p_b_v, p_b_logf,
            s_a_k, s_a_v, s_idx_k, s_b_k, s_b_v, s_b_logf)
```

```python
import functools
import math

import numpy as np
import jax
import jax.numpy as jnp
from jax import lax
from jax.experimental import pallas as pl
from jax.experimental.pallas import tpu as pltpu

F32 = jnp.float32
BF16 = jnp.bfloat16
I32 = jnp.int32

D_MODEL = 1024
CHUNK = 64
HEAD_DIM = 64
H_A = 8
HKV_A = 2
H_B = 8
H_IDX = 8
D_IDX = 64
K_TOP_MAX = 256
NUM_BUCKETS = 32
MAX_DISTANCE = 128
D_FF = 4 * D_MODEL
EPS = 1e-6
IN_SPLITS = (H_A * HEAD_DIM, HKV_A * HEAD_DIM, HKV_A * HEAD_DIM, H_IDX * D_IDX, D_IDX, H_IDX,
             H_B * HEAD_DIM, H_B * HEAD_DIM, H_B * HEAD_DIM, H_B)

LANES = 128
VMEM_LIMIT = 56 * 1024 * 1024

TQ = 256
TK = 256
NEG = -1e30
KEY_NEG_INF = -2139095041
INT_MIN = -2147483648

C_AQ, C_IQ, C_BQ, C_AKV, C_BK, C_BV, C_IK2, C_SM, N_CAT = 0, 512, 1024, 1536, 1792, 2304, 2816, 2944, 3072
AQ_HEAD_ORDER = (0, 4, 1, 5, 2, 6, 3, 7)

NT_DIMS = (((1,), (1,)), ((), ()))


def _cparams(*sem):
    return pltpu.CompilerParams(dimension_semantics=sem, vmem_limit_bytes=VMEM_LIMIT)


def _resident(shape):
    nd = len(shape)
    return pl.BlockSpec(shape, lambda *_: (0,) * nd, pipeline_mode=pl.Buffered(1))


def _rms(x, g):
    ms = jnp.mean(x * x, axis=-1, keepdims=True)
    return (x * lax.rsqrt(ms + EPS)) * g


def _sortable(x):
    b = lax.bitcast_convert_type(x, I32)
    return b ^ (lax.shift_right_arithmetic(b, 31) & 0x7FFFFFFF)


def _t5_bucket_np(rel):
    half = NUM_BUCKETS // 2
    max_exact = half // 2
    ret = np.where(rel > 0, half, 0)
    n = np.abs(rel)
    n_f = np.maximum(n, max_exact).astype(np.float64)
    large = max_exact + (np.log(n_f / max_exact) / math.log(MAX_DISTANCE / max_exact)
                         * (half - max_exact)).astype(np.int32)
    large = np.minimum(large, half - 1)
    return (ret + np.where(n < max_exact, n, large)).astype(np.int32)


def _inproj_kernel(x_ref, g_ref, w_ref, bf_ref,
                   aq_ref, iq_ref, bq_ref, akv_ref, bk_ref, bv_ref, ik2_ref,
                   ak32_ref, av32_ref, ik32_ref, bk32_ref, bv32_ref, logf_ref, iw_ref):
    h = _rms(x_ref[0], g_ref[...])
    p = jnp.dot(h.astype(BF16), w_ref[...], preferred_element_type=F32)
    qscale = HEAD_DIM ** -0.5
    aq_ref[0] = (p[:, C_AQ:C_AQ + 512] * qscale).astype(BF16)
    iq_ref[0] = (p[:, C_IQ:C_IQ + 512] * (D_IDX ** -0.5)).astype(BF16)
    bq_ref[0] = (p[:, C_BQ:C_BQ + 512] * qscale).astype(BF16)
    akv_ref[0] = p[:, C_AKV:C_AKV + 256].astype(BF16)
    bk_ref[0] = p[:, C_BK:C_BK + 512].astype(BF16)
    bv_ref[0] = p[:, C_BV:C_BV + 512].astype(BF16)
    ik2_ref[0] = p[:, C_IK2:C_IK2 + 128].astype(BF16)
    ak32_ref[0] = p[:, C_AKV:C_AKV + 128]
    av32_ref[0] = p[:, C_AKV + 128:C_AKV + 256]
    ik32_ref[0] = p[:, C_IK2:C_IK2 + 64]
    bk32_ref[0] = p[:, C_BK:C_BK + 512]
    bv32_ref[0] = p[:, C_BV:C_BV + 512]
    iw_ref[0] = p[:, C_SM:C_SM + 8] * (H_IDX ** -0.5)
    z = p[:, C_SM + 8:C_SM + 16] + bf_ref[...]
    logf_ref[0] = jnp.minimum(z, 0.0) - jnp.log1p(jnp.exp(-jnp.abs(z)))


def _inproj(x, g, w_cat, b_f, tm):
    b, t, d = x.shape
    grid = (b, t // tm)
    row = lambda c: pl.BlockSpec((1, tm, c), lambda i, j: (i, j, 0))
    outs = [(512, BF16), (512, BF16), (512, BF16), (256, BF16), (512, BF16), (512, BF16), (128, BF16),
            (128, F32), (128, F32), (64, F32), (512, F32), (512, F32), (8, F32), (8, F32)]
    return pl.pallas_call(
        _inproj_kernel,
        grid=grid,
        in_specs=[row(d), _resident((1, d)), _resident((d, N_CAT)), _resident((1, H_B))],
        out_specs=[row(c) for c, _ in outs],
        out_shape=[jax.ShapeDtypeStruct((b, t, c), dt) for c, dt in outs],
        compiler_params=_cparams("parallel", "parallel"),
        name="inproj",
    )(x, g, w_cat, b_f)


CS_BLK = 512


def _cumsum_kernel(x_ref, out_ref, carry_ref):
    @pl.when(pl.program_id(1) == 0)
    def _():
        carry_ref[...] = jnp.zeros_like(carry_ref)

    x = x_ref[0]
    hi = x.astype(BF16)
    r1 = x - hi.astype(F32)
    mid = r1.astype(BF16)
    lo = (r1 - mid.astype(F32)).astype(BF16)
    src = lax.broadcasted_iota(I32, (CS_BLK, CS_BLK), 0)
    dst = lax.broadcasted_iota(I32, (CS_BLK, CS_BLK), 1)
    tri = jnp.where(src <= dst, 1.0, 0.0).astype(BF16)
    cs = (jnp.dot(hi, tri, preferred_element_type=F32)
          + jnp.dot(mid, tri, preferred_element_type=F32)
          + jnp.dot(lo, tri, preferred_element_type=F32))
    out = cs + carry_ref[...]
    out_ref[0] = out
    carry_ref[...] = jnp.broadcast_to(out[:, CS_BLK - 1:CS_BLK], carry_ref.shape)


def _cumsum_lanes(x_t):
    b, h, l = x_t.shape
    spec = pl.BlockSpec((1, h, CS_BLK), lambda i, j: (i, 0, j))
    return pl.pallas_call(
        _cumsum_kernel,
        grid=(b, l // CS_BLK),
        in_specs=[spec],
        out_specs=spec,
        out_shape=jax.ShapeDtypeStruct((b, h, l), F32),
        scratch_shapes=[pltpu.VMEM((h, CS_BLK), F32)],
        compiler_params=_cparams("parallel", "arbitrary"),
        name="cumsum",
    )(x_t)


def _bias_kernel(rb_ref, bucket_ref, out_ref):
    b = bucket_ref[0]
    for h in range(H_A):
        acc = jnp.zeros(b.shape, F32)
        for k in range(NUM_BUCKETS):
            acc = jnp.where(b == k, rb_ref[k, h], acc)
        out_ref[0, h] = acc


def _bias_tiles(rel_bias, bucket):
    n, r, c = bucket.shape
    return pl.pallas_call(
        _bias_kernel,
        grid=(n,),
        in_specs=[pl.BlockSpec(memory_space=pltpu.SMEM),
                  pl.BlockSpec((1, r, c), lambda i: (i, 0, 0))],
        out_specs=pl.BlockSpec((1, H_A, r, c), lambda i: (i, 0, 0, 0)),
        out_shape=jax.ShapeDtypeStruct((n, H_A, r, c), F32),
        compiler_params=_cparams("parallel"),
        name="t5_bias",
    )(rel_bias, bucket)


def _dsa_prompt_kernel(rb_ref, aq_ref, iq_ref, iw_ref, ik2_ref, akv_ref, bias_ref, out_ref,
                       qi_ref, qa_ref, iwrep_ref, key_ref, selb_ref, lg_ref, mx_ref, ps_ref,
                       p_ref, acc_ref, thr_ref, *, k_top, far_bucket, seq_len):
    i = pl.program_id(1)
    nkb = i + 1
    lane = lax.broadcasted_iota(I32, (TQ, LANES), 1)
    low = lane < HEAD_DIM

    for p in range(4):
        s_i = iq_ref[0, :, LANES * p:LANES * (p + 1)].astype(F32)
        qi_ref[(2 * p) * TQ:(2 * p + 1) * TQ, :] = jnp.where(low, s_i, 0.0).astype(BF16)
        qi_ref[(2 * p + 1) * TQ:(2 * p + 2) * TQ, :] = jnp.where(low, 0.0, s_i).astype(BF16)
        s_a = aq_ref[0, :, LANES * p:LANES * (p + 1)].astype(F32)
        qa_ref[p * TQ:(p + 1) * TQ, :] = jnp.where(low, s_a, 0.0).astype(BF16)
        qa_ref[(p + 4) * TQ:(p + 5) * TQ, :] = jnp.where(low, 0.0, s_a).astype(BF16)
    iw = iw_ref[0]
    for h in range(H_IDX):
        iwrep_ref[h] = jnp.broadcast_to(iw[:, h:h + 1], (TQ, TK))

    def blk(j):
        return pl.ds(pl.multiple_of(j * TK, TK), TK)

    def score_blk(j, c):
        d = lax.dot_general(qi_ref[...], ik2_ref[0, blk(j), :], NT_DIMS, preferred_element_type=F32)
        acc = jnp.zeros((TQ, TK), F32)
        for h in range(H_IDX):
            acc = acc + iwrep_ref[h] * jnp.maximum(d[h * TQ:(h + 1) * TQ, :], 0.0)
        key_ref[:, blk(j)] = _sortable(acc)
        return c

    lax.fori_loop(0, nkb, score_blk, 0)

    r2 = lax.broadcasted_iota(I32, (TQ, TK), 0)
    c2 = lax.broadcasted_iota(I32, (TQ, TK), 1)
    adm = (c2 // CHUNK) <= (r2 // CHUNK)
    key_ref[:, blk(i)] = jnp.where(adm, key_ref[:, blk(i)], KEY_NEG_INF)

    def count(pred):
        def body(j, acc):
            kb = key_ref[:, blk(j)]
            col = j * TK + lane
            return (acc + jnp.where(pred(kb[:, :LANES], col), 1.0, 0.0)
                    + jnp.where(pred(kb[:, LANES:], col + LANES), 1.0, 0.0))
        acc = lax.fori_loop(0, nkb, body, jnp.zeros((TQ, LANES), F32))
        return jnp.broadcast_to(jnp.sum(acc, axis=1, keepdims=True), (TQ, LANES))

    kf = float(k_top)
    c0 = count(lambda kb, col: kb >= 0)
    ans = jnp.where(c0 >= kf, 0, INT_MIN).astype(I32)

    def bit_body(t, ans):
        cand = ans | lax.shift_left(jnp.int32(1), 30 - t)
        c = count(lambda kb, col: kb >= cand)
        return jnp.where(c >= kf, cand, ans)

    ans = lax.fori_loop(0, 31, bit_body, ans)

    cgt = count(lambda kb, col: kb > ans)
    ceq = count(lambda kb, col: kb == ans)
    need = kf - cgt
    tie = (ceq > need) & (ans > KEY_NEG_INF)
    flag = jnp.max(jnp.max(jnp.where(tie, 1.0, 0.0), axis=1, keepdims=True), axis=0, keepdims=True)
    thr_ref[...] = jnp.full((TQ, LANES), seq_len, I32)

    @pl.when(flag[0, 0] > 0.0)
    def _():
        nbits = int(seq_len - 1).bit_length()

        def tie_body(t, m):
            cand = m | lax.shift_left(jnp.int32(1), nbits - 1 - t)
            c = count(lambda kb, col: (kb == ans) & (col < cand))
            return jnp.where(c < need, cand, m)

        m = lax.fori_loop(0, nbits, tie_body, jnp.zeros((TQ, LANES), I32))
        thr_ref[...] = jnp.where(tie, m, seq_len)

    thr = thr_ref[...]

    def selb_blk(j, c):
        kb = key_ref[:, blk(j)]
        col = j * TK + lane
        for half in range(2):
            kh = kb[:, half * LANES:(half + 1) * LANES]
            ch = col + half * LANES
            sel = ((kh > ans) | ((kh == ans) & (ch <= thr))) & (kh > KEY_NEG_INF)
            selb_ref[:, pl.ds(pl.multiple_of(j * TK + half * LANES, LANES), LANES)] = jnp.where(sel, 0.0, NEG)
        return c

    lax.fori_loop(0, nkb, selb_blk, 0)

    for h in range(H_A):
        mx_ref[h] = jnp.full((TQ, LANES), NEG, F32)
        ps_ref[h] = jnp.zeros((TQ, LANES), F32)
    acc_ref[...] = jnp.zeros_like(acc_ref)

    def pass_a(j, bias_of_head):
        s = lax.dot_general(qa_ref[...], akv_ref[0, blk(j), 0:LANES], NT_DIMS, preferred_element_type=F32)
        sb = selb_ref[:, blk(j)]
        for h in range(H_A):
            lg = s[h * TQ:(h + 1) * TQ, :] + sb + bias_of_head(h)
            lg_ref[h, :, blk(j)] = lg
            mx_ref[h] = jnp.maximum(mx_ref[h], jnp.maximum(lg[:, :LANES], lg[:, LANES:]))

    def far_blk(j, c):
        pass_a(j, lambda h: rb_ref[far_bucket, h])
        return c

    lax.fori_loop(0, jnp.maximum(i - 1, 0), far_blk, 0)

    @pl.when(i >= 1)
    def _():
        pass_a(i - 1, lambda h: bias_ref[0, h])

    pass_a(i, lambda h: bias_ref[1, h])

    for h in range(H_A):
        mx_ref[h] = jnp.broadcast_to(jnp.max(mx_ref[h], axis=1, keepdims=True), (TQ, LANES))

    def pass_b(j, c):
        for h in range(H_A):
            lg = lg_ref[h, :, blk(j)]
            m = mx_ref[h]
            p0 = jnp.exp(lg[:, :LANES] - m)
            p1 = jnp.exp(lg[:, LANES:] - m)
            ps_ref[h] = ps_ref[h] + (p0 + p1)
            p_ref[h * TQ:(h + 1) * TQ, 0:LANES] = p0.astype(BF16)
            p_ref[h * TQ:(h + 1) * TQ, LANES:TK] = p1.astype(BF16)
        acc_ref[...] += jnp.dot(p_ref[...], akv_ref[0, blk(j), LANES:2 * LANES], preferred_element_type=F32)
        return c

    lax.fori_loop(0, nkb, pass_b, 0)

    for p in range(4):
        l_lo = jnp.sum(ps_ref[p], axis=1, keepdims=True)
        l_hi = jnp.sum(ps_ref[p + 4], axis=1, keepdims=True)
        o_lo = acc_ref[p * TQ:(p + 1) * TQ, :] / l_lo
        o_hi = acc_ref[(p + 4) * TQ:(p + 5) * TQ, :] / l_hi
        out_ref[0, :, LANES * p:LANES * (p + 1)] = jnp.where(low, o_lo, o_hi).astype(BF16)


def _dsa_prompt(rel_bias, aq, iq, iw, ik2, akv, bias_tiles, far_bucket):
    b, t, _ = aq.shape
    k_top = min(K_TOP_MAX, t // 4)
    qrow = lambda c: pl.BlockSpec((1, TQ, c), lambda i, j: (i, j, 0))
    full = lambda c: pl.BlockSpec((1, t, c), lambda i, j: (i, 0, 0))
    kern = functools.partial(_dsa_prompt_kernel, k_top=k_top, far_bucket=far_bucket, seq_len=t)
    return pl.pallas_call(
        kern,
        grid=(b, t // TQ),
        in_specs=[pl.BlockSpec(memory_space=pltpu.SMEM),
                  qrow(512), qrow(512), qrow(8), full(128), full(256),
                  _resident((2, H_A, TQ, TK))],
        out_specs=qrow(512),
        out_shape=jax.ShapeDtypeStruct((b, t, 512), BF16),
        scratch_shapes=[
            pltpu.VMEM((H_IDX * TQ, LANES), BF16),
            pltpu.VMEM((H_A * TQ, LANES), BF16),
            pltpu.VMEM((H_IDX, TQ, TK), F32),
            pltpu.VMEM((TQ, t), I32),
            pltpu.VMEM((TQ, t), F32),
            pltpu.VMEM((H_A, TQ, t), F32),
            pltpu.VMEM((H_A, TQ, LANES), F32),
            pltpu.VMEM((H_A, TQ, LANES), F32),
            pltpu.VMEM((H_A * TQ, TK), BF16),
            pltpu.VMEM((H_A * TQ, LANES), F32),
            pltpu.VMEM((TQ, LANES), I32),
        ],
        compiler_params=_cparams("parallel", "parallel"),
        name="dsa_prompt",
    )(rel_bias, aq, iq, iw, ik2, akv, bias_tiles)


def _fox_prompt_kernel(bq_ref, bk_ref, bv_ref, fq_ref, fkt_ref, out_ref,
                       qb_ref, fqrep_ref, lg_ref, mx_ref, ps_ref, p_ref, acc_ref):
    i = pl.program_id(1)
    lane = lax.broadcasted_iota(I32, (TQ, LANES), 1)
    low = lane < HEAD_DIM

    for p in range(4):
        s_b = bq_ref[0, :, LANES * p:LANES * (p + 1)].astype(F32)
        qb_ref[(2 * p) * TQ:(2 * p + 1) * TQ, :] = jnp.where(low, s_b, 0.0).astype(BF16)
        qb_ref[(2 * p + 1) * TQ:(2 * p + 2) * TQ, :] = jnp.where(low, 0.0, s_b).astype(BF16)
    fq = fq_ref[0]
    for h in range(H_B):
        fqrep_ref[h] = jnp.broadcast_to(fq[:, h:h + 1], (TQ, LANES))
        mx_ref[h] = jnp.full((TQ, LANES), NEG, F32)
        ps_ref[h] = jnp.zeros((TQ, LANES), F32)
    acc_ref[...] = jnp.zeros_like(acc_ref)

    def blk(j):
        return pl.ds(pl.multiple_of(j * TK, TK), TK)

    r2 = lax.broadcasted_iota(I32, (TQ, LANES), 0)

    def pass_a(j, diagonal):
        for p in range(4):
            s = lax.dot_general(qb_ref[(2 * p) * TQ:(2 * p + 2) * TQ, :],
                                bk_ref[0, blk(j), LANES * p:LANES * (p + 1)],
                                NT_DIMS, preferred_element_type=F32)
            for e in range(2):
                h = 2 * p + e
                fk = fkt_ref[0, h:h + 1, blk(j)]
                halves = []
                for half in range(2):
                    lg = (s[e * TQ:(e + 1) * TQ, half * LANES:(half + 1) * LANES]
                          + (fqrep_ref[h] - fk[:, half * LANES:(half + 1) * LANES]))
                    if diagonal:
                        lg = jnp.where(lane + half * LANES <= r2, lg, NEG)
                    lg_ref[h, :, pl.ds(pl.multiple_of(j * TK + half * LANES, LANES), LANES)] = lg
                    halves.append(lg)
                mx_ref[h] = jnp.maximum(mx_ref[h], jnp.maximum(halves[0], halves[1]))

    def off_diag(j, c):
        pass_a(j, False)
        return c

    lax.fori_loop(0, i, off_diag, 0)
    pass_a(i, True)

    for h in range(H_B):
        mx_ref[h] = jnp.broadcast_to(jnp.max(mx_ref[h], axis=1, keepdims=True), (TQ, LANES))

    def pass_b(j, c):
        for p in range(4):
            for e in range(2):
                h = 2 * p + e
                lg = lg_ref[h, :, blk(j)]
                m = mx_ref[h]
                p0 = jnp.exp(lg[:, :LANES] - m)
                p1 = jnp.exp(lg[:, LANES:] - m)
                ps_ref[h] = ps_ref[h] + (p0 + p1)
                p_ref[h * TQ:(h + 1) * TQ, 0:LANES] = p0.astype(BF16)
                p_ref[h * TQ:(h + 1) * TQ, LANES:TK] = p1.astype(BF16)
            acc_ref[(2 * p) * TQ:(2 * p + 2) * TQ, :] += jnp.dot(
                p_ref[(2 * p) * TQ:(2 * p + 2) * TQ, :], bv_ref[0, blk(j), LANES * p:LANES * (p + 1)],
                preferred_element_type=F32)
        return c

    lax.fori_loop(0, i + 1, pass_b, 0)

    for p in range(4):
        l_lo = jnp.sum(ps_ref[2 * p], axis=1, keepdims=True)
        l_hi = jnp.sum(ps_ref[2 * p + 1], axis=1, keepdims=True)
        o_lo = acc_ref[(2 * p) * TQ:(2 * p + 1) * TQ, :] / l_lo
        o_hi = acc_ref[(2 * p + 1) * TQ:(2 * p + 2) * TQ, :] / l_hi
        out_ref[0, :, LANES * p:LANES * (p + 1)] = jnp.where(low, o_lo, o_hi).astype(BF16)


def _fox_prompt(bq, bk, bv, fq, fkt):
    b, t, _ = bq.shape
    qrow = lambda c: pl.BlockSpec((1, TQ, c), lambda i, j: (i, j, 0))
    full = lambda c: pl.BlockSpec((1, t, c), lambda i, j: (i, 0, 0))
    return pl.pallas_call(
        _fox_prompt_kernel,
        grid=(b, t // TQ),
        in_specs=[qrow(512), full(512), full(512), qrow(8),
                  pl.BlockSpec((1, H_B, t), lambda i, j: (i, 0, 0))],
        out_specs=qrow(512),
        out_shape=jax.ShapeDtypeStruct((b, t, 512), BF16),
        scratch_shapes=[
            pltpu.VMEM((H_B * TQ, LANES), BF16),
            pltpu.VMEM((H_B, TQ, LANES), F32),
            pltpu.VMEM((H_B, TQ, t), F32),
            pltpu.VMEM((H_B, TQ, LANES), F32),
            pltpu.VMEM((H_B, TQ, LANES), F32),
            pltpu.VMEM((H_B * TQ, TK), BF16),
            pltpu.VMEM((H_B * TQ, LANES), F32),
        ],
        compiler_params=_cparams("parallel", "parallel"),
        name="fox_prompt",
    )(bq, bk, bv, fq, fkt)


FF_BLK = 1024


def _tail_kernel(x_ref, oa_ref, ob_ref, woa_ref, wob_ref, gm_ref, wup_ref, wdn_ref, gf_ref, y_ref):
    mixed = (jnp.dot(oa_ref[...], woa_ref[...], preferred_element_type=F32)
             + jnp.dot(ob_ref[...], wob_ref[...], preferred_element_type=F32))
    x1 = x_ref[...] + mixed
    h2 = _rms(x1, gm_ref[...]).astype(BF16)
    acc = x1
    for c in range(D_FF // FF_BLK):
        u = jnp.dot(h2, wup_ref[:, c * FF_BLK:(c + 1) * FF_BLK], preferred_element_type=F32)
        u = jnp.square(jnp.maximum(u, 0.0)).astype(BF16)
        acc = acc + jnp.dot(u, wdn_ref[c * FF_BLK:(c + 1) * FF_BLK, :], preferred_element_type=F32)
    y_ref[...] = _rms(acc, gf_ref[...])


def _tail(x, oa, ob, woa, wob, g_mlp, w_up, w_down, g_final, tm):
    n, d = x.shape
    row = lambda c: pl.BlockSpec((tm, c), lambda i: (i, 0))
    return pl.pallas_call(
        _tail_kernel,
        grid=(n // tm,),
        in_specs=[row(d), row(512), row(512), _resident((512, d)), _resident((512, d)), _resident((1, d)),
                  _resident((d, D_FF)), _resident((D_FF, d)), _resident((1, d))],
        out_specs=row(d),
        out_shape=jax.ShapeDtypeStruct((n, d), F32),
        compiler_params=_cparams("parallel"),
        name="tail",
    )(x, oa, ob, woa, wob, g_mlp, w_up, w_down, g_final)


def _dsa_sample_kernel(aq_ref, iq_ref, iw_ref, ik2_ref, akv_ref, cik_ref, cak_ref, cav_ref, bias_ref, out_ref,
                       ik_all, ak_all, av_all, *, past, n_new, l_pad, k_top):
    l_all = past + n_new
    ik_all[0:past, :] = cik_ref[0].astype(BF16)
    ik_all[past:l_all, :] = ik2_ref[0][:, 0:D_IDX]
    ik_all[l_all:l_pad, :] = jnp.zeros((l_pad - l_all, D_IDX), BF16)
    ak_all[0:past, :] = cak_ref[0].astype(BF16)
    ak_all[past:l_all, :] = akv_ref[0][:, 0:LANES]
    ak_all[l_all:l_pad, :] = jnp.zeros((l_pad - l_all, LANES), BF16)
    av_all[0:past, :] = cav_ref[0].astype(BF16)
    av_all[past:l_all, :] = akv_ref[0][:, LANES:2 * LANES]
    av_all[l_all:l_pad, :] = jnp.zeros((l_pad - l_all, LANES), BF16)

    iq32 = iq_ref[0].astype(F32)
    qi = jnp.concatenate([iq32[:, D_IDX * h:D_IDX * (h + 1)] for h in range(H_IDX)], axis=0).astype(BF16)
    d = lax.dot_general(qi, ik_all[...], NT_DIMS, preferred_element_type=F32)
    iw = iw_ref[0]
    score = jnp.zeros((n_new, l_pad), F32)
    for h in range(H_IDX):
        score = score + iw[:, h:h + 1] * jnp.maximum(d[h * n_new:(h + 1) * n_new, :], 0.0)
    row = lax.broadcasted_iota(I32, (n_new, l_pad), 0)
    col = lax.broadcasted_iota(I32, (n_new, l_pad), 1)
    adm = (col < l_all) & ((col // CHUNK) <= ((past + row) // CHUNK))
    key = jnp.where(adm, _sortable(score), KEY_NEG_INF)

    def count(mask):
        return jnp.sum(jnp.where(mask, 1.0, 0.0), axis=1, keepdims=True)

    kf = float(k_top)
    ans = jnp.where(count(key >= 0) >= kf, 0, INT_MIN).astype(I32)

    def bit_body(t, ans):
        cand = ans | lax.shift_left(jnp.int32(1), 30 - t)
        return jnp.where(count(key >= cand) >= kf, cand, ans)

    ans = lax.fori_loop(0, 31, bit_body, ans)
    eq = key == ans
    need = kf - count(key > ans)
    tie = (count(eq) > need) & (ans > KEY_NEG_INF)
    flag = jnp.max(jnp.where(tie, 1.0, 0.0), axis=0, keepdims=True)
    nbits = int(l_pad - 1).bit_length()

    def tie_search(_):
        def tie_body(t, m):
            cand = m | lax.shift_left(jnp.int32(1), nbits - 1 - t)
            return jnp.where(count(eq & (col < cand)) < need, cand, m)
        m = lax.fori_loop(0, nbits, tie_body, jnp.zeros((n_new, 1), I32))
        return jnp.where(tie, m, l_pad)

    thr = lax.cond(flag[0, 0] > 0.0, tie_search, lambda _: jnp.full((n_new, 1), l_pad, I32), 0)
    sel = ((key > ans) | (eq & (col <= thr))) & (key > KEY_NEG_INF)
    selb = jnp.where(sel, 0.0, NEG)

    lane = lax.broadcasted_iota(I32, (n_new, LANES), 1)
    low = lane < HEAD_DIM
    aq32 = aq_ref[0].astype(F32)
    slabs = [aq32[:, LANES * p:LANES * (p + 1)] for p in range(4)]
    qa = jnp.concatenate([jnp.where(low, s, 0.0) for s in slabs] + [jnp.where(low, 0.0, s) for s in slabs],
                         axis=0).astype(BF16)
    s = lax.dot_general(qa, ak_all[...], NT_DIMS, preferred_element_type=F32)
    lg = s + bias_ref[...] + jnp.concatenate([selb] * H_A, axis=0)
    m = jnp.max(lg, axis=1, keepdims=True)
    p = jnp.exp(lg - m)
    l = jnp.sum(p, axis=1, keepdims=True)
    o = jnp.dot(p.astype(BF16), av_all[...], preferred_element_type=F32) / l
    for q in range(4):
        out_ref[0, :, LANES * q:LANES * (q + 1)] = jnp.where(
            low, o[q * n_new:(q + 1) * n_new, :], o[(q + 4) * n_new:(q + 5) * n_new, :]).astype(BF16)


def _dsa_sample(aq, iq, iw, ik2, akv, cik, cak, cav, bias_rows, past):
    b, n_new, _ = aq.shape
    l_pad = bias_rows.shape[1]
    k_top = min(K_TOP_MAX, (past + n_new) // 4)
    new = lambda c: pl.BlockSpec((1, n_new, c), lambda i: (i, 0, 0))
    old = lambda c: pl.BlockSpec((1, past, c), lambda i: (i, 0, 0))
    kern = functools.partial(_dsa_sample_kernel, past=past, n_new=n_new, l_pad=l_pad, k_top=k_top)
    return pl.pallas_call(
        kern,
        grid=(b,),
        in_specs=[new(512), new(512), new(8), new(128), new(256), old(D_IDX), old(LANES), old(LANES),
                  _resident((H_A * n_new, l_pad))],
        out_specs=new(512),
        out_shape=jax.ShapeDtypeStruct((b, n_new, 512), BF16),
        scratch_shapes=[pltpu.VMEM((l_pad, D_IDX), BF16), pltpu.VMEM((l_pad, LANES), BF16),
                        pltpu.VMEM((l_pad, LANES), BF16)],
        compiler_params=_cparams("parallel"),
        name="dsa_sample",
    )(aq, iq, iw, ik2, akv, cik, cak, cav, bias_rows)


def _fox_sample_kernel(bq_ref, bkn_ref, bvn_ref, ck_ref, cv_ref, fkt_ref, tot_ref, lf_ref, lft_ref, out_ref,
                       *, past, n_new):
    lf = lf_ref[0]
    lft = lft_ref[0]
    rown = lax.broadcasted_iota(I32, (n_new, H_B), 0)
    lanen = lax.broadcasted_iota(I32, (H_B, n_new), 1)
    fq = jnp.broadcast_to(tot_ref[0], (n_new, H_B))
    fqt = jnp.broadcast_to(fkt_ref[0][:, past - 1:past], (H_B, n_new))
    for s in range(n_new):
        fq = fq + jnp.where(rown >= s, lf[s:s + 1, :], 0.0)
        fqt = fqt + jnp.where(lanen >= s, lft[:, s:s + 1], 0.0)

    kc = ck_ref[0].astype(BF16)
    vc = cv_ref[0].astype(BF16)
    bq32 = bq_ref[0].astype(F32)
    head_of_lane = lax.broadcasted_iota(I32, (n_new, H_B * HEAD_DIM), 1) // HEAD_DIM
    qb = jnp.concatenate([jnp.where(head_of_lane == h, bq32, 0.0) for h in range(H_B)], axis=0).astype(BF16)
    s_past = lax.dot_general(qb, kc, NT_DIMS, preferred_element_type=F32)
    s_new = lax.dot_general(qb, bkn_ref[0], NT_DIMS, preferred_element_type=F32)
    fkt = fkt_ref[0]
    fq_col = jnp.concatenate([fq[:, h:h + 1] for h in range(H_B)], axis=0)
    fk_past = jnp.concatenate([jnp.broadcast_to(fkt[h:h + 1, :], (n_new, past)) for h in range(H_B)], axis=0)
    fk_new = jnp.concatenate([jnp.broadcast_to(fqt[h:h + 1, :], (n_new, n_new)) for h in range(H_B)], axis=0)
    lg_past = s_past + (fq_col - fk_past)
    trow = lax.broadcasted_iota(I32, (H_B * n_new, n_new), 0) % n_new
    tcol = lax.broadcasted_iota(I32, (H_B * n_new, n_new), 1)
    lg_new = jnp.where(tcol <= trow, s_new + (fq_col - fk_new), NEG)
    m = jnp.maximum(jnp.max(lg_past, axis=1, keepdims=True), jnp.max(lg_new, axis=1, keepdims=True))
    p_past = jnp.exp(lg_past - m)
    p_new = jnp.exp(lg_new - m)
    l = jnp.sum(p_past, axis=1, keepdims=True) + jnp.sum(p_new, axis=1, keepdims=True)
    o = (jnp.dot(p_past.astype(BF16), vc, preferred_element_type=F32)
         + jnp.dot(p_new.astype(BF16), bvn_ref[0], preferred_element_type=F32)) / l
    out = jnp.zeros((n_new, H_B * HEAD_DIM), F32)
    for h in range(H_B):
        out = out + jnp.where(head_of_lane == h, o[h * n_new:(h + 1) * n_new, :], 0.0)
    out_ref[0] = out.astype(BF16)


def _fox_sample(bq, bkn, bvn, ck, cv, fkt, tot, lf, lft, past):
    b, n_new, _ = bq.shape
    new = lambda c: pl.BlockSpec((1, n_new, c), lambda i: (i, 0, 0))
    old = lambda c: pl.BlockSpec((1, past, c), lambda i: (i, 0, 0))
    kern = functools.partial(_fox_sample_kernel, past=past, n_new=n_new)
    return pl.pallas_call(
        kern,
        grid=(b,),
        in_specs=[new(512), new(512), new(512), old(512), old(512),
                  pl.BlockSpec((1, H_B, past), lambda i: (i, 0, 0)),
                  pl.BlockSpec((1, 1, H_B), lambda i: (i, 0, 0)),
                  new(H_B),
                  pl.BlockSpec((1, H_B, n_new), lambda i: (i, 0, 0))],
        out_specs=new(512),
        out_shape=jax.ShapeDtypeStruct((b, n_new, 512), BF16),
        compiler_params=_cparams("parallel"),
        name="fox_sample",
    )(bq, bkn, bvn, ck, cv, fkt, tot, lf, lft)


def _prep_weights(w_in, w_o, w_up, w_down):
    offs = np.concatenate([[0], np.cumsum(IN_SPLITS)])
    seg = lambda k: w_in[:, int(offs[k]):int(offs[k + 1])]
    a_q, a_k, a_v, i_q, i_k, i_w, b_q, b_k, b_v, f_z = (seg(k) for k in range(10))
    a_q_pairs = jnp.concatenate([a_q[:, HEAD_DIM * h:HEAD_DIM * (h + 1)] for h in AQ_HEAD_ORDER], axis=1)
    pad = jnp.zeros((w_in.shape[0], N_CAT - C_SM - 16), w_in.dtype)
    w_cat = jnp.concatenate([a_q_pairs, i_q, b_q, a_k, a_v, b_k, b_v, i_k, i_k, i_w, f_z, pad], axis=1)
    wo_a = jnp.concatenate([w_o[HEAD_DIM * h:HEAD_DIM * (h + 1)] for h in AQ_HEAD_ORDER], axis=0)
    wo_b = w_o[H_A * HEAD_DIM:]
    return (w_cat.astype(BF16), wo_a.astype(BF16), wo_b.astype(BF16), w_up.astype(BF16), w_down.astype(BF16))


def kernel(x_prompt, x_sample, cache_a_k, cache_a_v, cache_idx_k, cache_b_k, cache_b_v, cache_b_logf,
           w_in, w_o, b_f, rel_bias, g_attn, w_up, w_down, g_mlp, g_final):
    assert w_in.shape[0] == 1, "single-layer trunk"
    bp, tp, d = x_prompt.shape
    bs, ts, _ = x_sample.shape
    past = cache_a_k.shape[2]
    w_cat, wo_a, wo_b, wup, wdn = _prep_weights(w_in[0], w_o[0], w_up[0], w_down[0])
    g_a = g_attn[0].reshape(1, d)
    g_m = g_mlp[0].reshape(1, d)
    g_f = g_final.reshape(1, d)
    bf = b_f[0].reshape(1, H_B)
    rel_bias = rel_bias.astype(F32)

    (aq, iq, bq, akv, bk, bv, ik2, ak32, av32, ik32, bk32, bv32, logf, iw) = _inproj(x_prompt, g_a, w_cat, bf, 512)
    fkt = _cumsum_lanes(jnp.swapaxes(logf, 1, 2))
    fq = jnp.swapaxes(fkt, 1, 2)
    r = np.arange(TQ)[:, None]
    c = np.arange(TK)[None, :]
    bucket = _t5_bucket_np(np.stack([c - TK - r, c - r]))
    far_bucket = int(_t5_bucket_np(np.array(-TK - 1)))
    assert far_bucket == int(_t5_bucket_np(np.array(-tp)))
    bias_tiles = _bias_tiles(rel_bias, jnp.asarray(bucket))
    out_a = _dsa_prompt(rel_bias, aq, iq, iw, ik2, akv, bias_tiles, far_bucket)
    out_b = _fox_prompt(bq, bk, bv, fq, fkt)
    y_p = _tail(x_prompt.reshape(bp * tp, d), out_a.reshape(bp * tp, 512), out_b.reshape(bp * tp, 512),
                wo_a, wo_b, g_m, wup, wdn, g_f, 512).reshape(bp, tp, d)

    n_s = bs * ts
    outs = _inproj(x_sample.reshape(1, n_s, d), g_a, w_cat, bf, n_s)
    (aq_s, iq_s, bq_s, akv_s, bk_s, bv_s, ik2_s, ak32_s, av32_s, ik32_s, bk32_s, bv32_s, logf_s, iw_s) = (
        o.reshape(bs, ts, o.shape[-1]) for o in outs)
    l_all = past + ts
    l_pad = -(-l_all // LANES) * LANES
    rel_s = np.arange(l_pad)[None, :] - (past + np.arange(ts))[:, None]
    bias_s = _bias_tiles(rel_bias, jnp.asarray(_t5_bucket_np(rel_s))[None])[0].reshape(H_A * ts, l_pad)
    out_a_s = _dsa_sample(aq_s, iq_s, iw_s, ik2_s, akv_s, cache_idx_k[0],
                          cache_a_k[0].reshape(bs, past, HKV_A * HEAD_DIM),
                          cache_a_v[0].reshape(bs, past, HKV_A * HEAD_DIM), bias_s, past)
    fkt_c = _cumsum_lanes(jnp.swapaxes(cache_b_logf[0].astype(F32), 1, 2))
    tot = fkt_c[:, :, past - 1].reshape(bs, 1, H_B)
    out_b_s = _fox_sample(bq_s, bk_s, bv_s, cache_b_k[0].reshape(bs, past, H_B * HEAD_DIM),
                          cache_b_v[0].reshape(bs, past, H_B * HEAD_DIM), fkt_c, tot,
                          logf_s, jnp.swapaxes(logf_s, 1, 2), past)
    y_s = _tail(x_sample.reshape(n_s, d), out_a_s.reshape(n_s, 512), out_b_s.reshape(n_s, 512),
                wo_a, wo_b, g_m, wup, wdn, g_f, n_s).reshape(bs, ts, d)

    def rows(a, heads, b, t):
        return a.reshape(1, b, t, heads, HEAD_DIM)

    return (y_p, y_s,
            rows(ak32, HKV_A, bp, tp), rows(av32, HKV_A, bp, tp), ik32.reshape(1, bp, tp, D_IDX),
            rows(bk32, H_B, bp, tp), rows(bv32, H_B, bp, tp), logf.reshape(1, bp, tp, H_B),
            rows(ak32_s, HKV_A, bs, ts), rows(av32_s, HKV_A, bs, ts), ik32_s.reshape(1, bs, ts, D_IDX),
            rows(bk32_s, H_B, bs, ts), rows(bv32_s, H_B, bs, ts), logf_s.reshape(1, bs, ts, H_B))
```

```python
import functools
import math

import numpy as np
import jax
import jax.numpy as jnp
from jax import lax
from jax.experimental import pallas as pl
from jax.experimental.pallas import tpu as pltpu

F32 = jnp.float32
BF16 = jnp.bfloat16
I32 = jnp.int32

D_MODEL = 1024
CHUNK = 64
HEAD_DIM = 64
H_A = 8
HKV_A = 2
H_B = 8
H_IDX = 8
D_IDX = 64
K_TOP_MAX = 256
NUM_BUCKETS = 32
MAX_DISTANCE = 128
D_FF = 4 * D_MODEL
EPS = 1e-6
IN_SPLITS = (H_A * HEAD_DIM, HKV_A * HEAD_DIM, HKV_A * HEAD_DIM, H_IDX * D_IDX, D_IDX, H_IDX,
             H_B * HEAD_DIM, H_B * HEAD_DIM, H_B * HEAD_DIM, H_B)

LANES = 128
SUBLANES = 8
VMEM_LIMIT = 56 * 1024 * 1024

TQ = 256
TK = 256
NEG = -1e30
KEY_NEG_INF = -2139095041
INT_MIN = -2147483648

C_AQ, C_IQ, C_BQ, C_AKV, C_BK, C_BV, C_IK2, C_SM, N_CAT = 0, 512, 1024, 1536, 1792, 2304, 2816, 2944, 3072
AQ_HEAD_ORDER = (0, 4, 1, 5, 2, 6, 3, 7)

NT_DIMS = (((1,), (1,)), ((), ()))


def _cparams(*sem):
    return pltpu.CompilerParams(dimension_semantics=sem, vmem_limit_bytes=VMEM_LIMIT)


def _resident(shape):
    nd = len(shape)
    return pl.BlockSpec(shape, lambda *_: (0,) * nd, pipeline_mode=pl.Buffered(1))


def _rms(x, g):
    ms = jnp.mean(x * x, axis=-1, keepdims=True)
    return (x * lax.rsqrt(ms + EPS)) * g


def _sortable(x):
    b = lax.bitcast_convert_type(x, I32)
    return b ^ (lax.shift_right_arithmetic(b, 31) & 0x7FFFFFFF)


def _t5_bucket_np(rel):
    half = NUM_BUCKETS // 2
    max_exact = half // 2
    ret = np.where(rel > 0, half, 0)
    n = np.abs(rel)
    n_f = np.maximum(n, max_exact).astype(np.float64)
    large = max_exact + (np.log(n_f / max_exact) / math.log(MAX_DISTANCE / max_exact)
                         * (half - max_exact)).astype(np.int32)
    large = np.minimum(large, half - 1)
    return (ret + np.where(n < max_exact, n, large)).astype(np.int32)


def _inproj_kernel(x_ref, g_ref, w_ref, bf_ref,
                   aq_ref, iq_ref, bq_ref, akv_ref, bk_ref, bv_ref, ik2_ref,
                   ak32_ref, av32_ref, ik32_ref, bk32_ref, bv32_ref, logf_ref, iw_ref):
    h = _rms(x_ref[0], g_ref[...])
    p = jnp.dot(h.astype(BF16), w_ref[...], preferred_element_type=F32)
    qscale = HEAD_DIM ** -0.5
    aq_ref[0] = (p[:, C_AQ:C_AQ + 512] * qscale).astype(BF16)
    iq_ref[0] = (p[:, C_IQ:C_IQ + 512] * (D_IDX ** -0.5)).astype(BF16)
    bq_ref[0] = (p[:, C_BQ:C_BQ + 512] * qscale).astype(BF16)
    akv_ref[0] = p[:, C_AKV:C_AKV + 256].astype(BF16)
    bk_ref[0] = p[:, C_BK:C_BK + 512].astype(BF16)
    bv_ref[0] = p[:, C_BV:C_BV + 512].astype(BF16)
    ik2_ref[0] = p[:, C_IK2:C_IK2 + 128].astype(BF16)
    ak32_ref[0] = p[:, C_AKV:C_AKV + 128]
    av32_ref[0] = p[:, C_AKV + 128:C_AKV + 256]
    ik32_ref[0] = p[:, C_IK2:C_IK2 + 64]
    bk32_ref[0] = p[:, C_BK:C_BK + 512]
    bv32_ref[0] = p[:, C_BV:C_BV + 512]
    iw_ref[0] = p[:, C_SM:C_SM + 8] * (H_IDX ** -0.5)
    z = p[:, C_SM + 8:C_SM + 16] + bf_ref[...]
    logf_ref[0] = jnp.minimum(z, 0.0) - jnp.log1p(jnp.exp(-jnp.abs(z)))


def _inproj(x, g, w_cat, b_f, tm):
    b, t, d = x.shape
    grid = (b, t // tm)
    row = lambda c: pl.BlockSpec((1, tm, c), lambda i, j: (i, j, 0))
    outs = [(512, BF16), (512, BF16), (512, BF16), (256, BF16), (512, BF16), (512, BF16), (128, BF16),
            (128, F32), (128, F32), (64, F32), (512, F32), (512, F32), (8, F32), (8, F32)]
    return pl.pallas_call(
        _inproj_kernel,
        grid=grid,
        in_specs=[row(d), _resident((1, d)), _resident((d, N_CAT)), _resident((1, H_B))],
        out_specs=[row(c) for c, _ in outs],
        out_shape=[jax.ShapeDtypeStruct((b, t, c), dt) for c, dt in outs],
        compiler_params=_cparams("parallel", "parallel"),
        name="inproj",
    )(x, g, w_cat, b_f)


CS_BLK = 512


def _cumsum_kernel(x_ref, out_ref, carry_ref):
    @pl.when(pl.program_id(1) == 0)
    def _():
        carry_ref[...] = jnp.zeros_like(carry_ref)

    x = x_ref[0]
    hi = x.astype(BF16)
    r1 = x - hi.astype(F32)
    mid = r1.astype(BF16)
    lo = (r1 - mid.astype(F32)).astype(BF16)
    src = lax.broadcasted_iota(I32, (CS_BLK, CS_BLK), 0)
    dst = lax.broadcasted_iota(I32, (CS_BLK, CS_BLK), 1)
    tri = jnp.where(src <= dst, 1.0, 0.0).astype(BF16)
    cs = (jnp.dot(hi, tri, preferred_element_type=F32)
          + jnp.dot(mid, tri, preferred_element_type=F32)
          + jnp.dot(lo, tri, preferred_element_type=F32))
    out = cs + carry_ref[...]
    out_ref[0] = out
    carry_ref[...] = jnp.broadcast_to(out[:, CS_BLK - 1:CS_BLK], carry_ref.shape)


def _cumsum_lanes(x_t):
    b, h, l = x_t.shape
    spec = pl.BlockSpec((1, h, CS_BLK), lambda i, j: (i, 0, j))
    return pl.pallas_call(
        _cumsum_kernel,
        grid=(b, l // CS_BLK),
        in_specs=[spec],
        out_specs=spec,
        out_shape=jax.ShapeDtypeStruct((b, h, l), F32),
        scratch_shapes=[pltpu.VMEM((h, CS_BLK), F32)],
        compiler_params=_cparams("parallel", "arbitrary"),
        name="cumsum",
    )(x_t)


def _bias_kernel(rb_ref, bucket_ref, out_ref):
    b = bucket_ref[0]
    for h in range(H_A):
        acc = jnp.zeros(b.shape, F32)
        for k in range(NUM_BUCKETS):
            acc = jnp.where(b == k, rb_ref[k, h], acc)
        out_ref[0, h] = acc


def _bias_tiles(rel_bias, bucket):
    n, r, c = bucket.shape
    return pl.pallas_call(
        _bias_kernel,
        grid=(n,),
        in_specs=[pl.BlockSpec(memory_space=pltpu.SMEM),
                  pl.BlockSpec((1, r, c), lambda i: (i, 0, 0))],
        out_specs=pl.BlockSpec((1, H_A, r, c), lambda i: (i, 0, 0, 0)),
        out_shape=jax.ShapeDtypeStruct((n, H_A, r, c), F32),
        compiler_params=_cparams("parallel"),
        name="t5_bias",
    )(rel_bias, bucket)


def _dsa_prompt_kernel(rb_ref, aq_ref, iq_ref, iw_ref, ik2_ref, akv_ref, bias_ref, out_ref,
                       qi_ref, qa_ref, iwrep_ref, key_ref, keyt_ref, selb_ref, lg_ref, mx_ref, ps_ref,
                       p_ref, acc_ref, thr_ref, *, k_top, far_bucket, seq_len):
    i = pl.program_id(1)
    nkb = i + 1
    lane = lax.broadcasted_iota(I32, (TQ, LANES), 1)
    low = lane < HEAD_DIM

    for p in range(4):
        s_i = iq_ref[0, :, LANES * p:LANES * (p + 1)].astype(F32)
        qi_ref[(2 * p) * TQ:(2 * p + 1) * TQ, :] = jnp.where(low, s_i, 0.0).astype(BF16)
        qi_ref[(2 * p + 1) * TQ:(2 * p + 2) * TQ, :] = jnp.where(low, 0.0, s_i).astype(BF16)
        s_a = aq_ref[0, :, LANES * p:LANES * (p + 1)].astype(F32)
        qa_ref[p * TQ:(p + 1) * TQ, :] = jnp.where(low, s_a, 0.0).astype(BF16)
        qa_ref[(p + 4) * TQ:(p + 5) * TQ, :] = jnp.where(low, 0.0, s_a).astype(BF16)
    iw = iw_ref[0]
    for h in range(H_IDX):
        iwrep_ref[h] = jnp.broadcast_to(iw[:, h:h + 1], (TQ, TK))

    def blk(j):
        return pl.ds(pl.multiple_of(j * TK, TK), TK)

    def score_blk(j, c):
        d = lax.dot_general(qi_ref[...], ik2_ref[0, blk(j), :], NT_DIMS, preferred_element_type=F32)
        acc = jnp.zeros((TQ, TK), F32)
        for h in range(H_IDX):
            acc = acc + iwrep_ref[h] * jnp.maximum(d[h * TQ:(h + 1) * TQ, :], 0.0)
        key_ref[:, blk(j)] = _sortable(acc)
        keyt_ref[blk(j), :] = _sortable(acc.T)
        return c

    lax.fori_loop(0, nkb, score_blk, 0)

    r2 = lax.broadcasted_iota(I32, (TQ, TK), 0)
    c2 = lax.broadcasted_iota(I32, (TQ, TK), 1)
    key_ref[:, blk(i)] = jnp.where((c2 // CHUNK) <= (r2 // CHUNK), key_ref[:, blk(i)], KEY_NEG_INF)
    keyt_ref[blk(i), :] = jnp.where((r2 // CHUNK) <= (c2 // CHUNK), keyt_ref[blk(i), :], KEY_NEG_INF)

    sub = lax.broadcasted_iota(I32, (SUBLANES, TQ), 0)

    def count(pred):
        def body(j, accs):
            base = pl.multiple_of(j * TK, TK)
            accs = list(accs)
            kblk = keyt_ref[pl.ds(base, TK), :]
            for g in range(TK // SUBLANES):
                kb = kblk[SUBLANES * g:SUBLANES * (g + 1), :]
                hit = jnp.where(pred(kb, base + SUBLANES * g + sub), 1.0, 0.0)
                accs[g % len(accs)] = accs[g % len(accs)] + hit
            return tuple(accs)
        zero = jnp.zeros((SUBLANES, TQ), F32)
        a = lax.fori_loop(0, nkb, body, (zero, zero, zero, zero))
        tot = jnp.sum((a[0] + a[1]) + (a[2] + a[3]), axis=0, keepdims=True)
        return jnp.broadcast_to(tot, (SUBLANES, TQ))

    kf = float(k_top)
    c0 = count(lambda kb, idx: kb >= 0)
    ans = jnp.where(c0 >= kf, 0, INT_MIN).astype(I32)

    def bit_body(t, ans):
        cand = ans | lax.shift_left(jnp.int32(1), 30 - t)
        c = count(lambda kb, idx: kb >= cand)
        return jnp.where(c >= kf, cand, ans)

    ans = lax.fori_loop(0, 31, bit_body, ans)

    cgt = count(lambda kb, idx: kb > ans)
    ceq = count(lambda kb, idx: kb == ans)
    need = kf - cgt
    tie = (ceq > need) & (ans > KEY_NEG_INF)
    flag = jnp.max(jnp.where(tie[0:1, :], 1.0, 0.0), axis=1, keepdims=True)
    thr_ref[...] = jnp.full((SUBLANES, TQ), seq_len, I32)

    @pl.when(flag[0, 0] > 0.0)
    def _():
        nbits = int(seq_len - 1).bit_length()

        def tie_body(t, m):
            cand = m | lax.shift_left(jnp.int32(1), nbits - 1 - t)
            c = count(lambda kb, idx: (kb == ans) & (idx < cand))
            return jnp.where(c < need, cand, m)

        m = lax.fori_loop(0, nbits, tie_body, jnp.zeros((SUBLANES, TQ), I32))
        thr_ref[...] = jnp.where(tie, m, seq_len)

    def to_rows(x):
        return jnp.broadcast_to(x[0:1, :], (LANES, TQ)).T

    ans_r = to_rows(ans)
    thr_r = to_rows(thr_ref[...])

    def selb_blk(j, c):
        kb = key_ref[:, blk(j)]
        col = j * TK + lane
        for half in range(2):
            kh = kb[:, half * LANES:(half + 1) * LANES]
            ch = col + half * LANES
            sel = ((kh > ans_r) | ((kh == ans_r) & (ch <= thr_r))) & (kh > KEY_NEG_INF)
            selb_ref[:, pl.ds(pl.multiple_of(j * TK + half * LANES, LANES), LANES)] = jnp.where(sel, 0.0, NEG)
        return c

    lax.fori_loop(0, nkb, selb_blk, 0)

    for h in range(H_A):
        mx_ref[h] = jnp.full((TQ, LANES), NEG, F32)
        ps_ref[h] = jnp.zeros((TQ, LANES), F32)
    acc_ref[...] = jnp.zeros_like(acc_ref)

    def pass_a(j, bias_of_head):
        s = lax.dot_general(qa_ref[...], akv_ref[0, blk(j), 0:LANES], NT_DIMS, preferred_element_type=F32)
        sb = selb_ref[:, blk(j)]
        for h in range(H_A):
            lg = s[h * TQ:(h + 1) * TQ, :] + sb + bias_of_head(h)
            lg_ref[h, :, blk(j)] = lg
            mx_ref[h] = jnp.maximum(mx_ref[h], jnp.maximum(lg[:, :LANES], lg[:, LANES:]))

    def far_blk(j, c):
        pass_a(j, lambda h: rb_ref[far_bucket, h])
        return c

    lax.fori_loop(0, jnp.maximum(i - 1, 0), far_blk, 0)

    @pl.when(i >= 1)
    def _():
        pass_a(i - 1, lambda h: bias_ref[0, h])

    pass_a(i, lambda h: bias_ref[1, h])

    for h in range(H_A):
        mx_ref[h] = jnp.broadcast_to(jnp.max(mx_ref[h], axis=1, keepdims=True), (TQ, LANES))

    def pass_b(j, c):
        for h in range(H_A):
            lg = lg_ref[h, :, blk(j)]
            m = mx_ref[h]
            p0 = jnp.exp(lg[:, :LANES] - m)
            p1 = jnp.exp(lg[:, LANES:] - m)
            ps_ref[h] = ps_ref[h] + (p0 + p1)
            p_ref[h * TQ:(h + 1) * TQ, 0:LANES] = p0.astype(BF16)
            p_ref[h * TQ:(h + 1) * TQ, LANES:TK] = p1.astype(BF16)
        acc_ref[...] += jnp.dot(p_ref[...], akv_ref[0, blk(j), LANES:2 * LANES], preferred_element_type=F32)
        return c

    lax.fori_loop(0, nkb, pass_b, 0)

    for p in range(4):
        l_lo = jnp.sum(ps_ref[p], axis=1, keepdims=True)
        l_hi = jnp.sum(ps_ref[p + 4], axis=1, keepdims=True)
        o_lo = acc_ref[p * TQ:(p + 1) * TQ, :] / l_lo
        o_hi = acc_ref[(p + 4) * TQ:(p + 5) * TQ, :] / l_hi
        out_ref[0, :, LANES * p:LANES * (p + 1)] = jnp.where(low, o_lo, o_hi).astype(BF16)


def _dsa_prompt(rel_bias, aq, iq, iw, ik2, akv, bias_tiles, far_bucket):
    b, t, _ = aq.shape
    k_top = min(K_TOP_MAX, t // 4)
    qrow = lambda c: pl.BlockSpec((1, TQ, c), lambda i, j: (i, j, 0))
    full = lambda c: pl.BlockSpec((1, t, c), lambda i, j: (i, 0, 0))
    kern = functools.partial(_dsa_prompt_kernel, k_top=k_top, far_bucket=far_bucket, seq_len=t)
    return pl.pallas_call(
        kern,
        grid=(b, t // TQ),
        in_specs=[pl.BlockSpec(memory_space=pltpu.SMEM),
                  qrow(512), qrow(512), qrow(8), full(128), full(256),
                  _resident((2, H_A, TQ, TK))],
        out_specs=qrow(512),
        out_shape=jax.ShapeDtypeStruct((b, t, 512), BF16),
        scratch_shapes=[
            pltpu.VMEM((H_IDX * TQ, LANES), BF16),
            pltpu.VMEM((H_A * TQ, LANES), BF16),
            pltpu.VMEM((H_IDX, TQ, TK), F32),
            pltpu.VMEM((TQ, t), I32),
            pltpu.VMEM((t, TQ), I32),
            pltpu.VMEM((TQ, t), F32),
            pltpu.VMEM((H_A, TQ, t), F32),
            pltpu.VMEM((H_A, TQ, LANES), F32),
            pltpu.VMEM((H_A, TQ, LANES), F32),
            pltpu.VMEM((H_A * TQ, TK), BF16),
            pltpu.VMEM((H_A * TQ, LANES), F32),
            pltpu.VMEM((SUBLANES, TQ), I32),
        ],
        compiler_params=_cparams("parallel", "parallel"),
        name="dsa_prompt",
    )(rel_bias, aq, iq, iw, ik2, akv, bias_tiles)


def _fox_prompt_kernel(bq_ref, bk_ref, bv_ref, fq_ref, fkt_ref, out_ref,
                       qb_ref, fqrep_ref, lg_ref, mx_ref, ps_ref, p_ref, acc_ref):
    i = pl.program_id(1)
    lane = lax.broadcasted_iota(I32, (TQ, LANES), 1)
    low = lane < HEAD_DIM

    for p in range(4):
        s_b = bq_ref[0, :, LANES * p:LANES * (p + 1)].astype(F32)
        qb_ref[(2 * p) * TQ:(2 * p + 1) * TQ, :] = jnp.where(low, s_b, 0.0).astype(BF16)
        qb_ref[(2 * p + 1) * TQ:(2 * p + 2) * TQ, :] = jnp.where(low, 0.0, s_b).astype(BF16)
    fq = fq_ref[0]
    for h in range(H_B):
        fqrep_ref[h] = jnp.broadcast_to(fq[:, h:h + 1], (TQ, LANES))
        mx_ref[h] = jnp.full((TQ, LANES), NEG, F32)
        ps_ref[h] = jnp.zeros((TQ, LANES), F32)
    acc_ref[...] = jnp.zeros_like(acc_ref)

    def blk(j):
        return pl.ds(pl.multiple_of(j * TK, TK), TK)

    r2 = lax.broadcasted_iota(I32, (TQ, LANES), 0)

    def pass_a(j, diagonal):
        for p in range(4):
            s = lax.dot_general(qb_ref[(2 * p) * TQ:(2 * p + 2) * TQ, :],
                                bk_ref[0, blk(j), LANES * p:LANES * (p + 1)],
                                NT_DIMS, preferred_element_type=F32)
            for e in range(2):
                h = 2 * p + e
                fk = fkt_ref[0, h:h + 1, blk(j)]
                halves = []
                for half in range(2):
                    lg = (s[e * TQ:(e + 1) * TQ, half * LANES:(half + 1) * LANES]
                          + (fqrep_ref[h] - fk[:, half * LANES:(half + 1) * LANES]))
                    if diagonal:
                        lg = jnp.where(lane + half * LANES <= r2, lg, NEG)
                    lg_ref[h, :, pl.ds(pl.multiple_of(j * TK + half * LANES, LANES), LANES)] = lg
                    halves.append(lg)
                mx_ref[h] = jnp.maximum(mx_ref[h], jnp.maximum(halves[0], halves[1]))

    def off_diag(j, c):
        pass_a(j, False)
        return c

    lax.fori_loop(0, i, off_diag, 0)
    pass_a(i, True)

    for h in range(H_B):
        mx_ref[h] = jnp.broadcast_to(jnp.max(mx_ref[h], axis=1, keepdims=True), (TQ, LANES))

    def pass_b(j, c):
        for p in range(4):
            for e in range(2):
                h = 2 * p + e
                lg = lg_ref[h, :, blk(j)]
                m = mx_ref[h]
                p0 = jnp.exp(lg[:, :LANES] - m)
                p1 = jnp.exp(lg[:, LANES:] - m)
                ps_ref[h] = ps_ref[h] + (p0 + p1)
                p_ref[h * TQ:(h + 1) * TQ, 0:LANES] = p0.astype(BF16)
                p_ref[h * TQ:(h + 1) * TQ, LANES:TK] = p1.astype(BF16)
            acc_ref[(2 * p) * TQ:(2 * p + 2) * TQ, :] += jnp.dot(
                p_ref[(2 * p) * TQ:(2 * p + 2) * TQ, :], bv_ref[0, blk(j), LANES * p:LANES * (p + 1)],
                preferred_element_type=F32)
        return c

    lax.fori_loop(0, i + 1, pass_b, 0)

    for p in range(4):
        l_lo = jnp.sum(ps_ref[2 * p], axis=1, keepdims=True)
        l_hi = jnp.sum(ps_ref[2 * p + 1], axis=1, keepdims=True)
        o_lo = acc_ref[(2 * p) * TQ:(2 * p + 1) * TQ, :] / l_lo
        o_hi = acc_ref[(2 * p + 1) * TQ:(2 * p + 2) * TQ, :] / l_hi
        out_ref[0, :, LANES * p:LANES * (p + 1)] = jnp.where(low, o_lo, o_hi).astype(BF16)


def _fox_prompt(bq, bk, bv, fq, fkt):
    b, t, _ = bq.shape
    qrow = lambda c: pl.BlockSpec((1, TQ, c), lambda i, j: (i, j, 0))
    full = lambda c: pl.BlockSpec((1, t, c), lambda i, j: (i, 0, 0))
    return pl.pallas_call(
        _fox_prompt_kernel,
        grid=(b, t // TQ),
        in_specs=[qrow(512), full(512), full(512), qrow(8),
                  pl.BlockSpec((1, H_B, t), lambda i, j: (i, 0, 0))],
        out_specs=qrow(512),
        out_shape=jax.ShapeDtypeStruct((b, t, 512), BF16),
        scratch_shapes=[
            pltpu.VMEM((H_B * TQ, LANES), BF16),
            pltpu.VMEM((H_B, TQ, LANES), F32),
            pltpu.VMEM((H_B, TQ, t), F32),
            pltpu.VMEM((H_B, TQ, LANES), F32),
            pltpu.VMEM((H_B, TQ, LANES), F32),
            pltpu.VMEM((H_B * TQ, TK), BF16),
            pltpu.VMEM((H_B * TQ, LANES), F32),
        ],
        compiler_params=_cparams("parallel", "parallel"),
        name="fox_prompt",
    )(bq, bk, bv, fq, fkt)


FF_BLK = 1024


def _tail_kernel(x_ref, oa_ref, ob_ref, woa_ref, wob_ref, gm_ref, wup_ref, wdn_ref, gf_ref, y_ref):
    mixed = (jnp.dot(oa_ref[...], woa_ref[...], preferred_element_type=F32)
             + jnp.dot(ob_ref[...], wob_ref[...], preferred_element_type=F32))
    x1 = x_ref[...] + mixed
    h2 = _rms(x1, gm_ref[...]).astype(BF16)
    acc = x1
    for c in range(D_FF // FF_BLK):
        u = jnp.dot(h2, wup_ref[:, c * FF_BLK:(c + 1) * FF_BLK], preferred_element_type=F32)
        u = jnp.square(jnp.maximum(u, 0.0)).astype(BF16)
        acc = acc + jnp.dot(u, wdn_ref[c * FF_BLK:(c + 1) * FF_BLK, :], preferred_element_type=F32)
    y_ref[...] = _rms(acc, gf_ref[...])


def _tail(x, oa, ob, woa, wob, g_mlp, w_up, w_down, g_final, tm):
    n, d = x.shape
    row = lambda c: pl.BlockSpec((tm, c), lambda i: (i, 0))
    return pl.pallas_call(
        _tail_kernel,
        grid=(n // tm,),
        in_specs=[row(d), row(512), row(512), _resident((512, d)), _resident((512, d)), _resident((1, d)),
                  _resident((d, D_FF)), _resident((D_FF, d)), _resident((1, d))],
        out_specs=row(d),
        out_shape=jax.ShapeDtypeStruct((n, d), F32),
        compiler_params=_cparams("parallel"),
        name="tail",
    )(x, oa, ob, woa, wob, g_mlp, w_up, w_down, g_final)


def _dsa_sample_kernel(aq_ref, iq_ref, iw_ref, ik2_ref, akv_ref, cik_ref, cak_ref, cav_ref, bias_ref, out_ref,
                       ik_all, ak_all, av_all, *, past, n_new, l_pad, k_top):
    l_all = past + n_new
    ik_all[0:past, :] = cik_ref[0].astype(BF16)
    ik_all[past:l_all, :] = ik2_ref[0][:, 0:D_IDX]
    ik_all[l_all:l_pad, :] = jnp.zeros((l_pad - l_all, D_IDX), BF16)
    ak_all[0:past, :] = cak_ref[0].astype(BF16)
    ak_all[past:l_all, :] = akv_ref[0][:, 0:LANES]
    ak_all[l_all:l_pad, :] = jnp.zeros((l_pad - l_all, LANES), BF16)
    av_all[0:past, :] = cav_ref[0].astype(BF16)
    av_all[past:l_all, :] = akv_ref[0][:, LANES:2 * LANES]
    av_all[l_all:l_pad, :] = jnp.zeros((l_pad - l_all, LANES), BF16)

    iq32 = iq_ref[0].astype(F32)
    qi = jnp.concatenate([iq32[:, D_IDX * h:D_IDX * (h + 1)] for h in range(H_IDX)], axis=0).astype(BF16)
    d = lax.dot_general(qi, ik_all[...], NT_DIMS, preferred_element_type=F32)
    iw = iw_ref[0]
    score = jnp.zeros((n_new, l_pad), F32)
    for h in range(H_IDX):
        score = score + iw[:, h:h + 1] * jnp.maximum(d[h * n_new:(h + 1) * n_new, :], 0.0)
    row = lax.broadcasted_iota(I32, (n_new, l_pad), 0)
    col = lax.broadcasted_iota(I32, (n_new, l_pad), 1)
    adm = (col < l_all) & ((col // CHUNK) <= ((past + row) // CHUNK))
    key = jnp.where(adm, _sortable(score), KEY_NEG_INF)

    def count(mask):
        return jnp.sum(jnp.where(mask, 1.0, 0.0), axis=1, keepdims=True)

    kf = float(k_top)
    ans = jnp.where(count(key >= 0) >= kf, 0, INT_MIN).astype(I32)

    def bit_body(t, ans):
        cand = ans | lax.shift_left(jnp.int32(1), 30 - t)
        return jnp.where(count(key >= cand) >= kf, cand, ans)

    ans = lax.fori_loop(0, 31, bit_body, ans)
    eq = key == ans
    need = kf - count(key > ans)
    tie = (count(eq) > need) & (ans > KEY_NEG_INF)
    flag = jnp.max(jnp.where(tie, 1.0, 0.0), axis=0, keepdims=True)
    nbits = int(l_pad - 1).bit_length()

    def tie_search(_):
        def tie_body(t, m):
            cand = m | lax.shift_left(jnp.int32(1), nbits - 1 - t)
            return jnp.where(count(eq & (col < cand)) < need, cand, m)
        m = lax.fori_loop(0, nbits, tie_body, jnp.zeros((n_new, 1), I32))
        return jnp.where(tie, m, l_pad)

    thr = lax.cond(flag[0, 0] > 0.0, tie_search, lambda _: jnp.full((n_new, 1), l_pad, I32), 0)
    sel = ((key > ans) | (eq & (col <= thr))) & (key > KEY_NEG_INF)
    selb = jnp.where(sel, 0.0, NEG)

    lane = lax.broadcasted_iota(I32, (n_new, LANES), 1)
    low = lane < HEAD_DIM
    aq32 = aq_ref[0].astype(F32)
    slabs = [aq32[:, LANES * p:LANES * (p + 1)] for p in range(4)]
    qa = jnp.concatenate([jnp.where(low, s, 0.0) for s in slabs] + [jnp.where(low, 0.0, s) for s in slabs],
                         axis=0).astype(BF16)
    s = lax.dot_general(qa, ak_all[...], NT_DIMS, preferred_element_type=F32)
    lg = s + bias_ref[...] + jnp.concatenate([selb] * H_A, axis=0)
    m = jnp.max(lg, axis=1, keepdims=True)
    p = jnp.exp(lg - m)
    l = jnp.sum(p, axis=1, keepdims=True)
    o = jnp.dot(p.astype(BF16), av_all[...], preferred_element_type=F32) / l
    for q in range(4):
        out_ref[0, :, LANES * q:LANES * (q + 1)] = jnp.where(
            low, o[q * n_new:(q + 1) * n_new, :], o[(q + 4) * n_new:(q + 5) * n_new, :]).astype(BF16)


def _dsa_sample(aq, iq, iw, ik2, akv, cik, cak, cav, bias_rows, past):
    b, n_new, _ = aq.shape
    l_pad = bias_rows.shape[1]
    k_top = min(K_TOP_MAX, (past + n_new) // 4)
    new = lambda c: pl.BlockSpec((1, n_new, c), lambda i: (i, 0, 0))
    old = lambda c: pl.BlockSpec((1, past, c), lambda i: (i, 0, 0))
    kern = functools.partial(_dsa_sample_kernel, past=past, n_new=n_new, l_pad=l_pad, k_top=k_top)
    return pl.pallas_call(
        kern,
        grid=(b,),
        in_specs=[new(512), new(512), new(8), new(128), new(256), old(D_IDX), old(LANES), old(LANES),
                  _resident((H_A * n_new, l_pad))],
        out_specs=new(512),
        out_shape=jax.ShapeDtypeStruct((b, n_new, 512), BF16),
        scratch_shapes=[pltpu.VMEM((l_pad, D_IDX), BF16), pltpu.VMEM((l_pad, LANES), BF16),
                        pltpu.VMEM((l_pad, LANES), BF16)],
        compiler_params=_cparams("parallel"),
        name="dsa_sample",
    )(aq, iq, iw, ik2, akv, cik, cak, cav, bias_rows)


def _fox_sample_kernel(bq_ref, bkn_ref, bvn_ref, ck_ref, cv_ref, fkt_ref, tot_ref, lf_ref, lft_ref, out_ref,
                       *, past, n_new):
    lf = lf_ref[0]
    lft = lft_ref[0]
    rown = lax.broadcasted_iota(I32, (n_new, H_B), 0)
    lanen = lax.broadcasted_iota(I32, (H_B, n_new), 1)
    fq = jnp.broadcast_to(tot_ref[0], (n_new, H_B))
    fqt = jnp.broadcast_to(fkt_ref[0][:, past - 1:past], (H_B, n_new))
    for s in range(n_new):
        fq = fq + jnp.where(rown >= s, lf[s:s + 1, :], 0.0)
        fqt = fqt + jnp.where(lanen >= s, lft[:, s:s + 1], 0.0)

    kc = ck_ref[0].astype(BF16)
    vc = cv_ref[0].astype(BF16)
    bq32 = bq_ref[0].astype(F32)
    head_of_lane = lax.broadcasted_iota(I32, (n_new, H_B * HEAD_DIM), 1) // HEAD_DIM
    qb = jnp.concatenate([jnp.where(head_of_lane == h, bq32, 0.0) for h in range(H_B)], axis=0).astype(BF16)
    s_past = lax.dot_general(qb, kc, NT_DIMS, preferred_element_type=F32)
    s_new = lax.dot_general(qb, bkn_ref[0], NT_DIMS, preferred_element_type=F32)
    fkt = fkt_ref[0]
    fq_col = jnp.concatenate([fq[:, h:h + 1] for h in range(H_B)], axis=0)
    fk_past = jnp.concatenate([jnp.broadcast_to(fkt[h:h + 1, :], (n_new, past)) for h in range(H_B)], axis=0)
    fk_new = jnp.concatenate([jnp.broadcast_to(fqt[h:h + 1, :], (n_new, n_new)) for h in range(H_B)], axis=0)
    lg_past = s_past + (fq_col - fk_past)
    trow = lax.broadcasted_iota(I32, (H_B * n_new, n_new), 0) % n_new
    tcol = lax.broadcasted_iota(I32, (H_B * n_new, n_new), 1)
    lg_new = jnp.where(tcol <= trow, s_new + (fq_col - fk_new), NEG)
    m = jnp.maximum(jnp.max(lg_past, axis=1, keepdims=True), jnp.max(lg_new, axis=1, keepdims=True))
    p_past = jnp.exp(lg_past - m)
    p_new = jnp.exp(lg_new - m)
    l = jnp.sum(p_past, axis=1, keepdims=True) + jnp.sum(p_new, axis=1, keepdims=True)
    o = (jnp.dot(p_past.astype(BF16), vc, preferred_element_type=F32)
         + jnp.dot(p_new.astype(BF16), bvn_ref[0], preferred_element_type=F32)) / l
    out = jnp.zeros((n_new, H_B * HEAD_DIM), F32)
    for h in range(H_B):
        out = out + jnp.where(head_of_lane == h, o[h * n_new:(h + 1) * n_new, :], 0.0)
    out_ref[0] = out.astype(BF16)


def _fox_sample(bq, bkn, bvn, ck, cv, fkt, tot, lf, lft, past):
    b, n_new, _ = bq.shape
    new = lambda c: pl.BlockSpec((1, n_new, c), lambda i: (i, 0, 0))
    old = lambda c: pl.BlockSpec((1, past, c), lambda i: (i, 0, 0))
    kern = functools.partial(_fox_sample_kernel, past=past, n_new=n_new)
    return pl.pallas_call(
        kern,
        grid=(b,),
        in_specs=[new(512), new(512), new(512), old(512), old(512),
                  pl.BlockSpec((1, H_B, past), lambda i: (i, 0, 0)),
                  pl.BlockSpec((1, 1, H_B), lambda i: (i, 0, 0)),
                  new(H_B),
                  pl.BlockSpec((1, H_B, n_new), lambda i: (i, 0, 0))],
        out_specs=new(512),
        out_shape=jax.ShapeDtypeStruct((b, n_new, 512), BF16),
        compiler_params=_cparams("parallel"),
        name="fox_sample",
    )(bq, bkn, bvn, ck, cv, fkt, tot, lf, lft)


def _prep_weights(w_in, w_o, w_up, w_down):
    offs = np.concatenate([[0], np.cumsum(IN_SPLITS)])
    seg = lambda k: w_in[:, int(offs[k]):int(offs[k + 1])]
    a_q, a_k, a_v, i_q, i_k, i_w, b_q, b_k, b_v, f_z = (seg(k) for k in range(10))
    a_q_pairs = jnp.concatenate([a_q[:, HEAD_DIM * h:HEAD_DIM * (h + 1)] for h in AQ_HEAD_ORDER], axis=1)
    pad = jnp.zeros((w_in.shape[0], N_CAT - C_SM - 16), w_in.dtype)
    w_cat = jnp.concatenate([a_q_pairs, i_q, b_q, a_k, a_v, b_k, b_v, i_k, i_k, i_w, f_z, pad], axis=1)
    wo_a = jnp.concatenate([w_o[HEAD_DIM * h:HEAD_DIM * (h + 1)] for h in AQ_HEAD_ORDER], axis=0)
    wo_b = w_o[H_A * HEAD_DIM:]
    return (w_cat.astype(BF16), wo_a.astype(BF16), wo_b.astype(BF16), w_up.astype(BF16), w_down.astype(BF16))


def kernel(x_prompt, x_sample, cache_a_k, cache_a_v, cache_idx_k, cache_b_k, cache_b_v, cache_b_logf,
           w_in, w_o, b_f, rel_bias, g_attn, w_up, w_down, g_mlp, g_final):
    assert w_in.shape[0] == 1, "single-layer trunk"
    bp, tp, d = x_prompt.shape
    bs, ts, _ = x_sample.shape
    past = cache_a_k.shape[2]
    w_cat, wo_a, wo_b, wup, wdn = _prep_weights(w_in[0], w_o[0], w_up[0], w_down[0])
    g_a = g_attn[0].reshape(1, d)
    g_m = g_mlp[0].reshape(1, d)
    g_f = g_final.reshape(1, d)
    bf = b_f[0].reshape(1, H_B)
    rel_bias = rel_bias.astype(F32)

    (aq, iq, bq, akv, bk, bv, ik2, ak32, av32, ik32, bk32, bv32, logf, iw) = _inproj(x_prompt, g_a, w_cat, bf, 512)
    fkt = _cumsum_lanes(jnp.swapaxes(logf, 1, 2))
    fq = jnp.swapaxes(fkt, 1, 2)
    r = np.arange(TQ)[:, None]
    c = np.arange(TK)[None, :]
    bucket = _t5_bucket_np(np.stack([c - TK - r, c - r]))
    far_bucket = int(_t5_bucket_np(np.array(-TK - 1)))
    assert far_bucket == int(_t5_bucket_np(np.array(-tp)))
    bias_tiles = _bias_tiles(rel_bias, jnp.asarray(bucket))
    out_a = _dsa_prompt(rel_bias, aq, iq, iw, ik2, akv, bias_tiles, far_bucket)
    out_b = _fox_prompt(bq, bk, bv, fq, fkt)
    y_p = _tail(x_prompt.reshape(bp * tp, d), out_a.reshape(bp * tp, 512), out_b.reshape(bp * tp, 512),
                wo_a, wo_b, g_m, wup, wdn, g_f, 512).reshape(bp, tp, d)

    n_s = bs * ts
    outs = _inproj(x_sample.reshape(1, n_s, d), g_a, w_cat, bf, n_s)
    (aq_s, iq_s, bq_s, akv_s, bk_s, bv_s, ik2_s, ak32_s, av32_s, ik32_s, bk32_s, bv32_s, logf_s, iw_s) = (
        o.reshape(bs, ts, o.shape[-1]) for o in outs)
    l_all = past + ts
    l_pad = -(-l_all // LANES) * LANES
    rel_s = np.arange(l_pad)[None, :] - (past + np.arange(ts))[:, None]
    bias_s = _bias_tiles(rel_bias, jnp.asarray(_t5_bucket_np(rel_s))[None])[0].reshape(H_A * ts, l_pad)
    out_a_s = _dsa_sample(aq_s, iq_s, iw_s, ik2_s, akv_s, cache_idx_k[0],
                          cache_a_k[0].reshape(bs, past, HKV_A * HEAD_DIM),
                          cache_a_v[0].reshape(bs, past, HKV_A * HEAD_DIM), bias_s, past)
    fkt_c = _cumsum_lanes(jnp.swapaxes(cache_b_logf[0].astype(F32), 1, 2))
    tot = fkt_c[:, :, past - 1].reshape(bs, 1, H_B)
    out_b_s = _fox_sample(bq_s, bk_s, bv_s, cache_b_k[0].reshape(bs, past, H_B * HEAD_DIM),
                          cache_b_v[0].reshape(bs, past, H_B * HEAD_DIM), fkt_c, tot,
                          logf_s, jnp.swapaxes(logf_s, 1, 2), past)
    y_s = _tail(x_sample.reshape(n_s, d), out_a_s.reshape(n_s, 512), out_b_s.reshape(n_s, 512),
                wo_a, wo_b, g_m, wup, wdn, g_f, n_s).reshape(bs, ts, d)

    def rows(a, heads, b, t):
        return a.reshape(1, b, t, heads, HEAD_DIM)

    return (y_p, y_s,
            rows(ak32, HKV_A, bp, tp), rows(av32, HKV_A, bp, tp), ik32.reshape(1, bp, tp, D_IDX),
            rows(bk32, H_B, bp, tp), rows(bv32, H_B, bp, tp), logf.reshape(1, bp, tp, H_B),
            rows(ak32_s, HKV_A, bs, ts), rows(av32_s, HKV_A, bs, ts), ik32_s.reshape(1, bs, ts, D_IDX),
            rows(bk32_s, H_B, bs, ts), rows(bv32_s, H_B, bs, ts), logf_s.reshape(1, bs, ts, H_B))
```

```python
import functools
import math

import numpy as np
import jax
import jax.numpy as jnp
from jax import lax
from jax.experimental import pallas as pl
from jax.experimental.pallas import tpu as pltpu

F32 = jnp.float32
BF16 = jnp.bfloat16
I32 = jnp.int32

D_MODEL = 1024
CHUNK = 64
HEAD_DIM = 64
H_A = 8
HKV_A = 2
H_B = 8
H_IDX = 8
D_IDX = 64
K_TOP_MAX = 256
NUM_BUCKETS = 32
MAX_DISTANCE = 128
D_FF = 4 * D_MODEL
EPS = 1e-6
IN_SPLITS = (H_A * HEAD_DIM, HKV_A * HEAD_DIM, HKV_A * HEAD_DIM, H_IDX * D_IDX, D_IDX, H_IDX,
             H_B * HEAD_DIM, H_B * HEAD_DIM, H_B * HEAD_DIM, H_B)

LANES = 128
SUBLANES = 8
VMEM_LIMIT = 56 * 1024 * 1024

TQ = 256
TK = 256
NEG = -1e30
LOG2E = 1.4426950408889634
KEY_NEG_INF = -2139095041
INT_MIN = -2147483648

C_AQ, C_IQ, C_BQ, C_AKV, C_BK, C_BV, C_IK2, C_SM, N_CAT = 0, 512, 1024, 1536, 1792, 2304, 2816, 2944, 3072
AQ_HEAD_ORDER = (0, 4, 1, 5, 2, 6, 3, 7)

NT_DIMS = (((1,), (1,)), ((), ()))


def _cparams(*sem):
    return pltpu.CompilerParams(dimension_semantics=sem, vmem_limit_bytes=VMEM_LIMIT)


def _resident(shape):
    nd = len(shape)
    return pl.BlockSpec(shape, lambda *_: (0,) * nd, pipeline_mode=pl.Buffered(1))


def _rms(x, g):
    ms = jnp.mean(x * x, axis=-1, keepdims=True)
    return (x * lax.rsqrt(ms + EPS)) * g


def _for_blocks(n, body):
    def pair(k, c):
        body(2 * k, 0)
        body(2 * k + 1, 1)
        return c

    lax.fori_loop(0, n // 2, pair, 0)

    @pl.when(n % 2 == 1)
    def _():
        body(n - 1, 0)


def _normalize_pair(a_lo, a_hi, low):
    num = jnp.where(low, a_lo, a_hi)
    den = pltpu.roll(jnp.where(low, a_hi, a_lo), HEAD_DIM, axis=1)
    return num / den


def _sortable(x):
    b = lax.bitcast_convert_type(x, I32)
    return b ^ (lax.shift_right_arithmetic(b, 31) & 0x7FFFFFFF)


def _t5_bucket_np(rel):
    half = NUM_BUCKETS // 2
    max_exact = half // 2
    ret = np.where(rel > 0, half, 0)
    n = np.abs(rel)
    n_f = np.maximum(n, max_exact).astype(np.float64)
    large = max_exact + (np.log(n_f / max_exact) / math.log(MAX_DISTANCE / max_exact)
                         * (half - max_exact)).astype(np.int32)
    large = np.minimum(large, half - 1)
    return (ret + np.where(n < max_exact, n, large)).astype(np.int32)


def _inproj_kernel(x_ref, g_ref, w_ref, bf_ref,
                   aq_ref, iq_ref, bq_ref, akv_ref, bk_ref, bv_ref, ik2_ref,
                   ak32_ref, av32_ref, ik32_ref, bk32_ref, bv32_ref, logf_ref, iw_ref):
    h = _rms(x_ref[0], g_ref[...])
    p = jnp.dot(h.astype(BF16), w_ref[...], preferred_element_type=F32)
    qscale = HEAD_DIM ** -0.5 * LOG2E
    aq_ref[0] = (p[:, C_AQ:C_AQ + 512] * qscale).astype(BF16)
    iq_ref[0] = (p[:, C_IQ:C_IQ + 512] * (D_IDX ** -0.5)).astype(BF16)
    bq_ref[0] = (p[:, C_BQ:C_BQ + 512] * qscale).astype(BF16)
    low = lax.broadcasted_iota(I32, (p.shape[0], LANES), 1) < HEAD_DIM
    av = p[:, C_AKV + LANES:C_AKV + 2 * LANES]
    akv_ref[0, :, 0:LANES] = p[:, C_AKV:C_AKV + LANES].astype(BF16)
    akv_ref[0, :, LANES:2 * LANES] = jnp.where(low, av, 1.0).astype(BF16)
    akv_ref[0, :, 2 * LANES:3 * LANES] = jnp.where(low, 1.0, av).astype(BF16)
    bk_ref[0] = p[:, C_BK:C_BK + 512].astype(BF16)
    for q in range(4):
        bv = p[:, C_BV + LANES * q:C_BV + LANES * (q + 1)]
        bv_ref[0, :, 2 * q * LANES:(2 * q + 1) * LANES] = jnp.where(low, bv, 1.0).astype(BF16)
        bv_ref[0, :, (2 * q + 1) * LANES:(2 * q + 2) * LANES] = jnp.where(low, 1.0, bv).astype(BF16)
    ik2_ref[0] = p[:, C_IK2:C_IK2 + 128].astype(BF16)
    ak32_ref[0] = p[:, C_AKV:C_AKV + 128]
    av32_ref[0] = p[:, C_AKV + 128:C_AKV + 256]
    ik32_ref[0] = p[:, C_IK2:C_IK2 + 64]
    bk32_ref[0] = p[:, C_BK:C_BK + 512]
    bv32_ref[0] = p[:, C_BV:C_BV + 512]
    iw_ref[0] = p[:, C_SM:C_SM + 8] * (H_IDX ** -0.5)
    z = p[:, C_SM + 8:C_SM + 16] + bf_ref[...]
    logf_ref[0] = jnp.minimum(z, 0.0) - jnp.log1p(jnp.exp(-jnp.abs(z)))


def _inproj(x, g, w_cat, b_f, tm):
    b, t, d = x.shape
    grid = (b, t // tm)
    row = lambda c: pl.BlockSpec((1, tm, c), lambda i, j: (i, j, 0))
    outs = [(512, BF16), (512, BF16), (512, BF16), (384, BF16), (512, BF16), (1024, BF16), (128, BF16),
            (128, F32), (128, F32), (64, F32), (512, F32), (512, F32), (8, F32), (8, F32)]
    return pl.pallas_call(
        _inproj_kernel,
        grid=grid,
        in_specs=[row(d), _resident((1, d)), _resident((d, N_CAT)), _resident((1, H_B))],
        out_specs=[row(c) for c, _ in outs],
        out_shape=[jax.ShapeDtypeStruct((b, t, c), dt) for c, dt in outs],
        compiler_params=_cparams("parallel", "parallel"),
        name="inproj",
    )(x, g, w_cat, b_f)


CS_BLK = 512


def _cumsum_kernel(x_ref, out_ref, carry_ref):
    @pl.when(pl.program_id(0) == 0)
    def _():
        carry_ref[...] = jnp.zeros_like(carry_ref)

    x = x_ref[...] * LOG2E
    hi = x.astype(BF16)
    r1 = x - hi.astype(F32)
    mid = r1.astype(BF16)
    lo = (r1 - mid.astype(F32)).astype(BF16)
    src = lax.broadcasted_iota(I32, (CS_BLK, CS_BLK), 0)
    dst = lax.broadcasted_iota(I32, (CS_BLK, CS_BLK), 1)
    tri = jnp.where(src <= dst, 1.0, 0.0).astype(BF16)
    cs = (jnp.dot(hi, tri, preferred_element_type=F32)
          + jnp.dot(mid, tri, preferred_element_type=F32)
          + jnp.dot(lo, tri, preferred_element_type=F32))
    out = cs + carry_ref[...]
    out_ref[...] = out
    carry_ref[...] = jnp.broadcast_to(out[:, CS_BLK - 1:CS_BLK], carry_ref.shape)


def _cumsum_lanes(x_t):
    b, h, l = x_t.shape
    spec = pl.BlockSpec((b * h, CS_BLK), lambda j: (0, j))
    return pl.pallas_call(
        _cumsum_kernel,
        grid=(l // CS_BLK,),
        in_specs=[spec],
        out_specs=spec,
        out_shape=jax.ShapeDtypeStruct((b * h, l), F32),
        scratch_shapes=[pltpu.VMEM((b * h, CS_BLK), F32)],
        compiler_params=_cparams("arbitrary"),
        name="cumsum",
    )(x_t.reshape(b * h, l)).reshape(b, h, l)


def _bias_kernel(rb_ref, bucket_ref, out_ref):
    b = bucket_ref[0]
    for h in range(H_A):
        acc = jnp.zeros(b.shape, F32)
        for k in range(NUM_BUCKETS):
            acc = jnp.where(b == k, rb_ref[k, h] * LOG2E, acc)
        out_ref[0, h] = acc


def _bias_tiles(rel_bias, bucket):
    n, r, c = bucket.shape
    return pl.pallas_call(
        _bias_kernel,
        grid=(n,),
        in_specs=[pl.BlockSpec(memory_space=pltpu.SMEM),
                  pl.BlockSpec((1, r, c), lambda i: (i, 0, 0))],
        out_specs=pl.BlockSpec((1, H_A, r, c), lambda i: (i, 0, 0, 0)),
        out_shape=jax.ShapeDtypeStruct((n, H_A, r, c), F32),
        compiler_params=_cparams("parallel"),
        name="t5_bias",
    )(rel_bias, bucket)


def _dsa_prompt_kernel(rb_ref, aq_ref, iq_ref, iw_ref, ik2_ref, akv_ref, bias_ref, out_ref,
                       qi_ref, qa_ref, iwrep_ref, key_ref, keyt_ref, selb_ref, lg_ref, mx_ref,
                       p_ref, acc_ref, thr_ref, *, k_top, far_bucket, seq_len):
    i = pl.program_id(1)
    nkb = i + 1
    lane = lax.broadcasted_iota(I32, (TQ, LANES), 1)
    low = lane < HEAD_DIM

    for p in range(4):
        s_i = iq_ref[0, :, LANES * p:LANES * (p + 1)].astype(F32)
        qi_ref[(2 * p) * TQ:(2 * p + 1) * TQ, :] = jnp.where(low, s_i, 0.0).astype(BF16)
        qi_ref[(2 * p + 1) * TQ:(2 * p + 2) * TQ, :] = jnp.where(low, 0.0, s_i).astype(BF16)
        s_a = aq_ref[0, :, LANES * p:LANES * (p + 1)].astype(F32)
        qa_ref[p * TQ:(p + 1) * TQ, :] = jnp.where(low, s_a, 0.0).astype(BF16)
        qa_ref[(p + 4) * TQ:(p + 5) * TQ, :] = jnp.where(low, 0.0, s_a).astype(BF16)
    iw = iw_ref[0]
    for h in range(H_IDX):
        iwrep_ref[h] = jnp.broadcast_to(iw[:, h:h + 1], (TQ, TK))

    def blk(j):
        return pl.ds(pl.multiple_of(j * TK, TK), TK)

    def score_blk(j, slot):
        kb = ik2_ref[0, blk(j), :]
        acc = jnp.zeros((TQ, TK), F32)
        for h in range(H_IDX):
            d = lax.dot_general(qi_ref[h * TQ:(h + 1) * TQ, :], kb, NT_DIMS, preferred_element_type=F32)
            acc = acc + iwrep_ref[h] * jnp.maximum(d, 0.0)
        key_ref[:, blk(j)] = _sortable(acc)
        keyt_ref[blk(j), :] = _sortable(acc.T)

    _for_blocks(nkb, score_blk)

    r2 = lax.broadcasted_iota(I32, (TQ, TK), 0)
    c2 = lax.broadcasted_iota(I32, (TQ, TK), 1)
    key_ref[:, blk(i)] = jnp.where((c2 // CHUNK) <= (r2 // CHUNK), key_ref[:, blk(i)], KEY_NEG_INF)
    keyt_ref[blk(i), :] = jnp.where((r2 // CHUNK) <= (c2 // CHUNK), keyt_ref[blk(i), :], KEY_NEG_INF)

    sub = lax.broadcasted_iota(I32, (SUBLANES, TQ), 0)

    def count(pred):
        def body(j, accs):
            base = pl.multiple_of(j * TK, TK)
            accs = list(accs)
            kblk = keyt_ref[pl.ds(base, TK), :]
            for g in range(TK // SUBLANES):
                kb = kblk[SUBLANES * g:SUBLANES * (g + 1), :]
                hit = jnp.where(pred(kb, base + SUBLANES * g + sub), 1.0, 0.0)
                accs[g % len(accs)] = accs[g % len(accs)] + hit
            return tuple(accs)
        zero = jnp.zeros((SUBLANES, TQ), F32)
        a = lax.fori_loop(0, nkb, body, (zero, zero, zero, zero))
        tot = jnp.sum((a[0] + a[1]) + (a[2] + a[3]), axis=0, keepdims=True)
        return jnp.broadcast_to(tot, (SUBLANES, TQ))

    kf = float(k_top)
    c0 = count(lambda kb, idx: kb >= 0)
    ans = jnp.where(c0 >= kf, 0, INT_MIN).astype(I32)

    def bit_body(t, ans):
        cand = ans | lax.shift_left(jnp.int32(1), 30 - t)
        c = count(lambda kb, idx: kb >= cand)
        return jnp.where(c >= kf, cand, ans)

    ans = lax.fori_loop(0, 31, bit_body, ans)

    cgt = count(lambda kb, idx: kb > ans)
    ceq = count(lambda kb, idx: kb == ans)
    need = kf - cgt
    tie = (ceq > need) & (ans > KEY_NEG_INF)
    flag = jnp.max(jnp.where(tie[0:1, :], 1.0, 0.0), axis=1, keepdims=True)
    thr_ref[...] = jnp.full((SUBLANES, TQ), seq_len, I32)

    @pl.when(flag[0, 0] > 0.0)
    def _():
        nbits = int(seq_len - 1).bit_length()

        def tie_body(t, m):
            cand = m | lax.shift_left(jnp.int32(1), nbits - 1 - t)
            c = count(lambda kb, idx: (kb == ans) & (idx < cand))
            return jnp.where(c < need, cand, m)

        m = lax.fori_loop(0, nbits, tie_body, jnp.zeros((SUBLANES, TQ), I32))
        thr_ref[...] = jnp.where(tie, m, seq_len)

    def to_rows(x):
        return jnp.broadcast_to(x[0:1, :], (LANES, TQ)).T

    ans_r = to_rows(ans)
    thr_r = to_rows(thr_ref[...])

    def selb_blk(j, c):
        kb = key_ref[:, blk(j)]
        col = j * TK + lane
        for half in range(2):
            kh = kb[:, half * LANES:(half + 1) * LANES]
            ch = col + half * LANES
            sel = ((kh > ans_r) | ((kh == ans_r) & (ch <= thr_r))) & (kh > KEY_NEG_INF)
            selb_ref[:, pl.ds(pl.multiple_of(j * TK + half * LANES, LANES), LANES)] = jnp.where(sel, 0.0, NEG)
        return c

    lax.fori_loop(0, nkb, selb_blk, 0)

    for h in range(H_A):
        mx_ref[h] = jnp.full((TQ, LANES), NEG, F32)
    acc_ref[...] = jnp.zeros_like(acc_ref)

    def pass_a(j, bias_of_head):
        kb = akv_ref[0, blk(j), 0:LANES]
        sb = selb_ref[:, blk(j)]
        for h in range(H_A):
            s = lax.dot_general(qa_ref[h * TQ:(h + 1) * TQ, :], kb, NT_DIMS, preferred_element_type=F32)
            lg = s + sb + bias_of_head(h)
            lg_ref[h, :, blk(j)] = lg
            mx_ref[h] = jnp.maximum(mx_ref[h], jnp.maximum(lg[:, :LANES], lg[:, LANES:]))

    _for_blocks(jnp.maximum(i - 1, 0), lambda j, slot: pass_a(j, lambda h: rb_ref[far_bucket, h] * LOG2E))

    @pl.when(i >= 1)
    def _():
        pass_a(i - 1, lambda h: bias_ref[0, h])

    pass_a(i, lambda h: bias_ref[1, h])

    for h in range(H_A):
        mx_ref[h] = jnp.broadcast_to(jnp.max(mx_ref[h], axis=1, keepdims=True), (TQ, LANES))

    half = (H_A // 2) * TQ

    def pass_b(j, slot):
        for h in range(H_A):
            lg = lg_ref[h, :, blk(j)]
            m = mx_ref[h]
            p_ref[slot, h * TQ:(h + 1) * TQ, 0:LANES] = jnp.exp2(lg[:, :LANES] - m).astype(BF16)
            p_ref[slot, h * TQ:(h + 1) * TQ, LANES:TK] = jnp.exp2(lg[:, LANES:] - m).astype(BF16)
        for g in range(HKV_A):
            acc_ref[g * half:(g + 1) * half, :] += jnp.dot(
                p_ref[slot, g * half:(g + 1) * half, :], akv_ref[0, blk(j), (g + 1) * LANES:(g + 2) * LANES],
                preferred_element_type=F32)

    _for_blocks(nkb, pass_b)

    for p in range(4):
        out_ref[0, :, LANES * p:LANES * (p + 1)] = _normalize_pair(
            acc_ref[p * TQ:(p + 1) * TQ, :], acc_ref[(p + 4) * TQ:(p + 5) * TQ, :], low).astype(BF16)


def _dsa_prompt(rel_bias, aq, iq, iw, ik2, akv, bias_tiles, far_bucket):
    b, t, _ = aq.shape
    k_top = min(K_TOP_MAX, t // 4)
    qrow = lambda c: pl.BlockSpec((1, TQ, c), lambda i, j: (i, j, 0))
    full = lambda c: pl.BlockSpec((1, t, c), lambda i, j: (i, 0, 0))
    kern = functools.partial(_dsa_prompt_kernel, k_top=k_top, far_bucket=far_bucket, seq_len=t)
    return pl.pallas_call(
        kern,
        grid=(b, t // TQ),
        in_specs=[pl.BlockSpec(memory_space=pltpu.SMEM),
                  qrow(512), qrow(512), qrow(8), full(128), full(384),
                  _resident((2, H_A, TQ, TK))],
        out_specs=qrow(512),
        out_shape=jax.ShapeDtypeStruct((b, t, 512), BF16),
        scratch_shapes=[
            pltpu.VMEM((H_IDX * TQ, LANES), BF16),
            pltpu.VMEM((H_A * TQ, LANES), BF16),
            pltpu.VMEM((H_IDX, TQ, TK), F32),
            pltpu.VMEM((TQ, t), I32),
            pltpu.VMEM((t, TQ), I32),
            pltpu.VMEM((TQ, t), F32),
            pltpu.VMEM((H_A, TQ, t), F32),
            pltpu.VMEM((H_A, TQ, LANES), F32),
            pltpu.VMEM((2, H_A * TQ, TK), BF16),
            pltpu.VMEM((H_A * TQ, LANES), F32),
            pltpu.VMEM((SUBLANES, TQ), I32),
        ],
        compiler_params=_cparams("parallel", "parallel"),
        name="dsa_prompt",
    )(rel_bias, aq, iq, iw, ik2, akv, bias_tiles)


def _fox_prompt_kernel(bq_ref, bk_ref, bv_ref, fq_ref, fkt_ref, out_ref,
                       qb_ref, fqrep_ref, lg_ref, mx_ref, p_ref, acc_ref):
    i = pl.program_id(1)
    lane = lax.broadcasted_iota(I32, (TQ, LANES), 1)
    low = lane < HEAD_DIM

    for p in range(4):
        s_b = bq_ref[0, :, LANES * p:LANES * (p + 1)].astype(F32)
        qb_ref[(2 * p) * TQ:(2 * p + 1) * TQ, :] = jnp.where(low, s_b, 0.0).astype(BF16)
        qb_ref[(2 * p + 1) * TQ:(2 * p + 2) * TQ, :] = jnp.where(low, 0.0, s_b).astype(BF16)
    fq = fq_ref[0]
    for h in range(H_B):
        fqrep_ref[h] = jnp.broadcast_to(fq[:, h:h + 1], (TQ, LANES))
        mx_ref[h] = jnp.full((TQ, LANES), NEG, F32)
    acc_ref[...] = jnp.zeros_like(acc_ref)

    def blk(j):
        return pl.ds(pl.multiple_of(j * TK, TK), TK)

    r2 = lax.broadcasted_iota(I32, (TQ, LANES), 0)

    def pass_a(j, diagonal):
        for p in range(4):
            s = lax.dot_general(qb_ref[(2 * p) * TQ:(2 * p + 2) * TQ, :],
                                bk_ref[0, blk(j), LANES * p:LANES * (p + 1)],
                                NT_DIMS, preferred_element_type=F32)
            for e in range(2):
                h = 2 * p + e
                fk = fkt_ref[0, h:h + 1, blk(j)]
                halves = []
                for half in range(2):
                    lg = (s[e * TQ:(e + 1) * TQ, half * LANES:(half + 1) * LANES]
                          + (fqrep_ref[h] - fk[:, half * LANES:(half + 1) * LANES]))
                    if diagonal:
                        lg = jnp.where(lane + half * LANES <= r2, lg, NEG)
                    lg_ref[h, :, pl.ds(pl.multiple_of(j * TK + half * LANES, LANES), LANES)] = lg
                    halves.append(lg)
                mx_ref[h] = jnp.maximum(mx_ref[h], jnp.maximum(halves[0], halves[1]))

    _for_blocks(i, lambda j, slot: pass_a(j, False))
    pass_a(i, True)

    for h in range(H_B):
        mx_ref[h] = jnp.broadcast_to(jnp.max(mx_ref[h], axis=1, keepdims=True), (TQ, LANES))

    def pass_b(j, slot):
        for h in range(H_B):
            lg = lg_ref[h, :, blk(j)]
            m = mx_ref[h]
            p_ref[slot, h * TQ:(h + 1) * TQ, 0:LANES] = jnp.exp2(lg[:, :LANES] - m).astype(BF16)
            p_ref[slot, h * TQ:(h + 1) * TQ, LANES:TK] = jnp.exp2(lg[:, LANES:] - m).astype(BF16)
            acc_ref[h * TQ:(h + 1) * TQ, :] += jnp.dot(
                p_ref[slot, h * TQ:(h + 1) * TQ, :], bv_ref[0, blk(j), LANES * h:LANES * (h + 1)],
                preferred_element_type=F32)

    _for_blocks(i + 1, pass_b)

    for p in range(4):
        out_ref[0, :, LANES * p:LANES * (p + 1)] = _normalize_pair(
            acc_ref[(2 * p) * TQ:(2 * p + 1) * TQ, :], acc_ref[(2 * p + 1) * TQ:(2 * p + 2) * TQ, :], low).astype(BF16)


def _fox_prompt(bq, bk, bv, fq, fkt):
    b, t, _ = bq.shape
    qrow = lambda c: pl.BlockSpec((1, TQ, c), lambda i, j: (i, j, 0))
    full = lambda c: pl.BlockSpec((1, t, c), lambda i, j: (i, 0, 0))
    return pl.pallas_call(
        _fox_prompt_kernel,
        grid=(b, t // TQ),
        in_specs=[qrow(512), full(512), full(1024), qrow(8),
                  pl.BlockSpec((1, H_B, t), lambda i, j: (i, 0, 0))],
        out_specs=qrow(512),
        out_shape=jax.ShapeDtypeStruct((b, t, 512), BF16),
        scratch_shapes=[
            pltpu.VMEM((H_B * TQ, LANES), BF16),
            pltpu.VMEM((H_B, TQ, LANES), F32),
            pltpu.VMEM((H_B, TQ, t), F32),
            pltpu.VMEM((H_B, TQ, LANES), F32),
            pltpu.VMEM((2, H_B * TQ, TK), BF16),
            pltpu.VMEM((H_B * TQ, LANES), F32),
        ],
        compiler_params=_cparams("parallel", "parallel"),
        name="fox_prompt",
    )(bq, bk, bv, fq, fkt)


FF_BLK = 1024


def _tail_kernel(x_ref, oa_ref, ob_ref, woa_ref, wob_ref, gm_ref, wup_ref, wdn_ref, gf_ref, y_ref):
    mixed = (jnp.dot(oa_ref[...], woa_ref[...], preferred_element_type=F32)
             + jnp.dot(ob_ref[...], wob_ref[...], preferred_element_type=F32))
    x1 = x_ref[...] + mixed
    h2 = _rms(x1, gm_ref[...]).astype(BF16)
    acc = x1
    for c in range(D_FF // FF_BLK):
        u = jnp.dot(h2, wup_ref[:, c * FF_BLK:(c + 1) * FF_BLK], preferred_element_type=F32)
        u = jnp.square(jnp.maximum(u, 0.0)).astype(BF16)
        acc = acc + jnp.dot(u, wdn_ref[c * FF_BLK:(c + 1) * FF_BLK, :], preferred_element_type=F32)
    y_ref[...] = _rms(acc, gf_ref[...])


def _tail(x, oa, ob, woa, wob, g_mlp, w_up, w_down, g_final, tm):
    n, d = x.shape
    row = lambda c: pl.BlockSpec((tm, c), lambda i: (i, 0))
    return pl.pallas_call(
        _tail_kernel,
        grid=(n // tm,),
        in_specs=[row(d), row(512), row(512), _resident((512, d)), _resident((512, d)), _resident((1, d)),
                  _resident((d, D_FF)), _resident((D_FF, d)), _resident((1, d))],
        out_specs=row(d),
        out_shape=jax.ShapeDtypeStruct((n, d), F32),
        compiler_params=_cparams("parallel"),
        name="tail",
    )(x, oa, ob, woa, wob, g_mlp, w_up, w_down, g_final)


def _dsa_sample_kernel(aq_ref, iq_ref, iw_ref, ik2_ref, akv_ref, cik_ref, cak_ref, cav_ref, bias_ref, out_ref,
                       ik_all, ak_all, av_all, *, past, n_new, l_pad, k_top):
    l_all = past + n_new
    ik_all[0:past, :] = cik_ref[0].astype(BF16)
    ik_all[past:l_all, :] = ik2_ref[0][:, 0:D_IDX]
    ik_all[l_all:l_pad, :] = jnp.zeros((l_pad - l_all, D_IDX), BF16)
    ak_all[0:past, :] = cak_ref[0].astype(BF16)
    ak_all[past:l_all, :] = akv_ref[0][:, 0:LANES]
    ak_all[l_all:l_pad, :] = jnp.zeros((l_pad - l_all, LANES), BF16)
    av_all[0:past, :] = cav_ref[0].astype(BF16)
    lane = lax.broadcasted_iota(I32, (n_new, LANES), 1)
    low = lane < HEAD_DIM
    av_new = jnp.where(low, akv_ref[0][:, LANES:2 * LANES].astype(F32), akv_ref[0][:, 2 * LANES:3 * LANES].astype(F32))
    av_all[past:l_all, :] = av_new.astype(BF16)
    av_all[l_all:l_pad, :] = jnp.zeros((l_pad - l_all, LANES), BF16)

    iq32 = iq_ref[0].astype(F32)
    qi = jnp.concatenate([iq32[:, D_IDX * h:D_IDX * (h + 1)] for h in range(H_IDX)], axis=0).astype(BF16)
    d = lax.dot_general(qi, ik_all[...], NT_DIMS, preferred_element_type=F32)
    iw = iw_ref[0]
    score = jnp.zeros((n_new, l_pad), F32)
    for h in range(H_IDX):
        score = score + iw[:, h:h + 1] * jnp.maximum(d[h * n_new:(h + 1) * n_new, :], 0.0)
    row = lax.broadcasted_iota(I32, (n_new, l_pad), 0)
    col = lax.broadcasted_iota(I32, (n_new, l_pad), 1)
    adm = (col < l_all) & ((col // CHUNK) <= ((past + row) // CHUNK))
    key = jnp.where(adm, _sortable(score), KEY_NEG_INF)

    def count(mask):
        return jnp.sum(jnp.where(mask, 1.0, 0.0), axis=1, keepdims=True)

    kf = float(k_top)
    ans = jnp.where(count(key >= 0) >= kf, 0, INT_MIN).astype(I32)

    def bit_body(t, ans):
        cand = ans | lax.shift_left(jnp.int32(1), 30 - t)
        return jnp.where(count(key >= cand) >= kf, cand, ans)

    ans = lax.fori_loop(0, 31, bit_body, ans)
    eq = key == ans
    need = kf - count(key > ans)
    tie = (count(eq) > need) & (ans > KEY_NEG_INF)
    flag = jnp.max(jnp.where(tie, 1.0, 0.0), axis=0, keepdims=True)
    nbits = int(l_pad - 1).bit_length()

    def tie_search(_):
        def tie_body(t, m):
            cand = m | lax.shift_left(jnp.int32(1), nbits - 1 - t)
            return jnp.where(count(eq & (col < cand)) < need, cand, m)
        m = lax.fori_loop(0, nbits, tie_body, jnp.zeros((n_new, 1), I32))
        return jnp.where(tie, m, l_pad)

    thr = lax.cond(flag[0, 0] > 0.0, tie_search, lambda _: jnp.full((n_new, 1), l_pad, I32), 0)
    sel = ((key > ans) | (eq & (col <= thr))) & (key > KEY_NEG_INF)
    selb = jnp.where(sel, 0.0, NEG)

    aq32 = aq_ref[0].astype(F32)
    slabs = [aq32[:, LANES * p:LANES * (p + 1)] for p in range(4)]
    qa = jnp.concatenate([jnp.where(low, s, 0.0) for s in slabs] + [jnp.where(low, 0.0, s) for s in slabs],
                         axis=0).astype(BF16)
    s = lax.dot_general(qa, ak_all[...], NT_DIMS, preferred_element_type=F32)
    lg = s + bias_ref[...] + jnp.concatenate([selb] * H_A, axis=0)
    m = jnp.max(lg, axis=1, keepdims=True)
    p = jnp.exp2(lg - m)
    l = jnp.sum(p, axis=1, keepdims=True)
    o = jnp.dot(p.astype(BF16), av_all[...], preferred_element_type=F32) / l
    for q in range(4):
        out_ref[0, :, LANES * q:LANES * (q + 1)] = jnp.where(
            low, o[q * n_new:(q + 1) * n_new, :], o[(q + 4) * n_new:(q + 5) * n_new, :]).astype(BF16)


def _dsa_sample(aq, iq, iw, ik2, akv, cik, cak, cav, bias_rows, past):
    b, n_new, _ = aq.shape
    l_pad = bias_rows.shape[1]
    k_top = min(K_TOP_MAX, (past + n_new) // 4)
    new = lambda c: pl.BlockSpec((1, n_new, c), lambda i: (i, 0, 0))
    old = lambda c: pl.BlockSpec((1, past, c), lambda i: (i, 0, 0))
    kern = functools.partial(_dsa_sample_kernel, past=past, n_new=n_new, l_pad=l_pad, k_top=k_top)
    return pl.pallas_call(
        kern,
        grid=(b,),
        in_specs=[new(512), new(512), new(8), new(128), new(384), old(D_IDX), old(LANES), old(LANES),
                  _resident((H_A * n_new, l_pad))],
        out_specs=new(512),
        out_shape=jax.ShapeDtypeStruct((b, n_new, 512), BF16),
        scratch_shapes=[pltpu.VMEM((l_pad, D_IDX), BF16), pltpu.VMEM((l_pad, LANES), BF16),
                        pltpu.VMEM((l_pad, LANES), BF16)],
        compiler_params=_cparams("parallel"),
        name="dsa_sample",
    )(aq, iq, iw, ik2, akv, cik, cak, cav, bias_rows)


def _fox_sample_kernel(bq_ref, bkn_ref, bvn_ref, ck_ref, cv_ref, fkt_ref, tot_ref, lf_ref, lft_ref, out_ref,
                       kc_ref, vc_ref, *, past, n_new):
    lf = lf_ref[0] * LOG2E
    lft = lft_ref[0] * LOG2E
    rown = lax.broadcasted_iota(I32, (n_new, H_B), 0)
    lanen = lax.broadcasted_iota(I32, (H_B, n_new), 1)
    fq = jnp.broadcast_to(tot_ref[0], (n_new, H_B))
    fqt = jnp.broadcast_to(fkt_ref[0][:, past - 1:past], (H_B, n_new))
    for s in range(n_new):
        fq = fq + jnp.where(rown >= s, lf[s:s + 1, :], 0.0)
        fqt = fqt + jnp.where(lanen >= s, lft[:, s:s + 1], 0.0)

    def head_rows(ref, h):
        return ref[0, pl.ds(h, past, stride=H_B), :]

    for q in range(H_B // 2):
        kc_ref[:, LANES * q:LANES * (q + 1)] = jnp.concatenate(
            [head_rows(ck_ref, 2 * q), head_rows(ck_ref, 2 * q + 1)], axis=1).astype(BF16)
        vc_ref[:, LANES * q:LANES * (q + 1)] = jnp.concatenate(
            [head_rows(cv_ref, 2 * q), head_rows(cv_ref, 2 * q + 1)], axis=1).astype(BF16)
    kc = kc_ref[...]
    vc = vc_ref[...]
    bq32 = bq_ref[0].astype(F32)
    head_of_lane = lax.broadcasted_iota(I32, (n_new, H_B * HEAD_DIM), 1) // HEAD_DIM
    qb = jnp.concatenate([jnp.where(head_of_lane == h, bq32, 0.0) for h in range(H_B)], axis=0).astype(BF16)
    s_past = lax.dot_general(qb, kc, NT_DIMS, preferred_element_type=F32)
    s_new = lax.dot_general(qb, bkn_ref[0], NT_DIMS, preferred_element_type=F32)
    fkt = fkt_ref[0]
    fq_col = jnp.concatenate([fq[:, h:h + 1] for h in range(H_B)], axis=0)
    fk_past = jnp.concatenate([jnp.broadcast_to(fkt[h:h + 1, :], (n_new, past)) for h in range(H_B)], axis=0)
    fk_new = jnp.concatenate([jnp.broadcast_to(fqt[h:h + 1, :], (n_new, n_new)) for h in range(H_B)], axis=0)
    lg_past = s_past + (fq_col - fk_past)
    trow = lax.broadcasted_iota(I32, (H_B * n_new, n_new), 0) % n_new
    tcol = lax.broadcasted_iota(I32, (H_B * n_new, n_new), 1)
    lg_new = jnp.where(tcol <= trow, s_new + (fq_col - fk_new), NEG)
    m = jnp.maximum(jnp.max(lg_past, axis=1, keepdims=True), jnp.max(lg_new, axis=1, keepdims=True))
    p_past = jnp.exp2(lg_past - m)
    p_new = jnp.exp2(lg_new - m)
    l = jnp.sum(p_past, axis=1, keepdims=True) + jnp.sum(p_new, axis=1, keepdims=True)
    o = (jnp.dot(p_past.astype(BF16), vc, preferred_element_type=F32)
         + jnp.dot(p_new.astype(BF16), bvn_ref[0].astype(BF16), preferred_element_type=F32)) / l
    out = jnp.zeros((n_new, H_B * HEAD_DIM), F32)
    for h in range(H_B):
        out = out + jnp.where(head_of_lane == h, o[h * n_new:(h + 1) * n_new, :], 0.0)
    out_ref[0] = out.astype(BF16)


def _fox_sample(bq, bkn, bvn, ck, cv, fkt, tot, lf, lft, past):
    b, n_new, _ = bq.shape
    new = lambda c: pl.BlockSpec((1, n_new, c), lambda i: (i, 0, 0))
    old = pl.BlockSpec((1, past * H_B, HEAD_DIM), lambda i: (i, 0, 0))
    kern = functools.partial(_fox_sample_kernel, past=past, n_new=n_new)
    return pl.pallas_call(
        kern,
        grid=(b,),
        in_specs=[new(512), new(512), new(512), old, old,
                  pl.BlockSpec((1, H_B, past), lambda i: (i, 0, 0)),
                  pl.BlockSpec((1, 1, H_B), lambda i: (i, 0, 0)),
                  new(H_B),
                  pl.BlockSpec((1, H_B, n_new), lambda i: (i, 0, 0))],
        out_specs=new(512),
        out_shape=jax.ShapeDtypeStruct((b, n_new, 512), BF16),
        scratch_shapes=[pltpu.VMEM((past, H_B * HEAD_DIM), BF16), pltpu.VMEM((past, H_B * HEAD_DIM), BF16)],
        compiler_params=_cparams("parallel"),
        name="fox_sample",
    )(bq, bkn, bvn, ck, cv, fkt, tot, lf, lft)


def _prep_weights(w_in, w_o, w_up, w_down):
    offs = np.concatenate([[0], np.cumsum(IN_SPLITS)])
    seg = lambda k: w_in[:, int(offs[k]):int(offs[k + 1])]
    a_q, a_k, a_v, i_q, i_k, i_w, b_q, b_k, b_v, f_z = (seg(k) for k in range(10))
    a_q_pairs = jnp.concatenate([a_q[:, HEAD_DIM * h:HEAD_DIM * (h + 1)] for h in AQ_HEAD_ORDER], axis=1)
    pad = jnp.zeros((w_in.shape[0], N_CAT - C_SM - 16), w_in.dtype)
    w_cat = jnp.concatenate([a_q_pairs, i_q, b_q, a_k, a_v, b_k, b_v, i_k, i_k, i_w, f_z, pad], axis=1)
    wo_a = jnp.concatenate([w_o[HEAD_DIM * h:HEAD_DIM * (h + 1)] for h in AQ_HEAD_ORDER], axis=0)
    wo_b = w_o[H_A * HEAD_DIM:]
    return (w_cat.astype(BF16), wo_a.astype(BF16), wo_b.astype(BF16), w_up.astype(BF16), w_down.astype(BF16))


def kernel(x_prompt, x_sample, cache_a_k, cache_a_v, cache_idx_k, cache_b_k, cache_b_v, cache_b_logf,
           w_in, w_o, b_f, rel_bias, g_attn, w_up, w_down, g_mlp, g_final):
    assert w_in.shape[0] == 1, "single-layer trunk"
    bp, tp, d = x_prompt.shape
    bs, ts, _ = x_sample.shape
    past = cache_a_k.shape[2]
    w_cat, wo_a, wo_b, wup, wdn = _prep_weights(w_in[0], w_o[0], w_up[0], w_down[0])
    g_a = g_attn[0].reshape(1, d)
    g_m = g_mlp[0].reshape(1, d)
    g_f = g_final.reshape(1, d)
    bf = b_f[0].reshape(1, H_B)
    rel_bias = rel_bias.astype(F32)

    (aq, iq, bq, akv, bk, bv, ik2, ak32, av32, ik32, bk32, bv32, logf, iw) = _inproj(x_prompt, g_a, w_cat, bf, 512)
    fkt = _cumsum_lanes(jnp.swapaxes(logf, 1, 2))
    fq = jnp.swapaxes(fkt, 1, 2)
    r = np.arange(TQ)[:, None]
    c = np.arange(TK)[None, :]
    bucket = _t5_bucket_np(np.stack([c - TK - r, c - r]))
    far_bucket = int(_t5_bucket_np(np.array(-TK - 1)))
    assert far_bucket == int(_t5_bucket_np(np.array(-tp)))
    bias_tiles = _bias_tiles(rel_bias, jnp.asarray(bucket))
    out_a = _dsa_prompt(rel_bias, aq, iq, iw, ik2, akv, bias_tiles, far_bucket)
    out_b = _fox_prompt(bq, bk, bv, fq, fkt)
    y_p = _tail(x_prompt.reshape(bp * tp, d), out_a.reshape(bp * tp, 512), out_b.reshape(bp * tp, 512),
                wo_a, wo_b, g_m, wup, wdn, g_f, 512).reshape(bp, tp, d)

    n_s = bs * ts
    outs = _inproj(x_sample.reshape(1, n_s, d), g_a, w_cat, bf, n_s)
    (aq_s, iq_s, bq_s, akv_s, bk_s, bv_s, ik2_s, ak32_s, av32_s, ik32_s, bk32_s, bv32_s, logf_s, iw_s) = (
        o.reshape(bs, ts, o.shape[-1]) for o in outs)
    l_all = past + ts
    l_pad = -(-l_all // LANES) * LANES
    rel_s = np.arange(l_pad)[None, :] - (past + np.arange(ts))[:, None]
    bias_s = _bias_tiles(rel_bias, jnp.asarray(_t5_bucket_np(rel_s))[None])[0].reshape(H_A * ts, l_pad)
    out_a_s = _dsa_sample(aq_s, iq_s, iw_s, ik2_s, akv_s, cache_idx_k[0],
                          cache_a_k[0].reshape(bs, past, HKV_A * HEAD_DIM),
                          cache_a_v[0].reshape(bs, past, HKV_A * HEAD_DIM), bias_s, past)
    fkt_c = _cumsum_lanes(jnp.swapaxes(cache_b_logf[0].astype(F32), 1, 2))
    tot = fkt_c[:, :, past - 1].reshape(bs, 1, H_B)
    out_b_s = _fox_sample(bq_s, bk_s, bv32_s, cache_b_k.reshape(bs, past * H_B, HEAD_DIM),
                          cache_b_v.reshape(bs, past * H_B, HEAD_DIM), fkt_c, tot,
                          logf_s, jnp.swapaxes(logf_s, 1, 2), past)
    y_s = _tail(x_sample.reshape(n_s, d), out_a_s.reshape(n_s, 512), out_b_s.reshape(n_s, 512),
                wo_a, wo_b, g_m, wup, wdn, g_f, n_s).reshape(bs, ts, d)

    def rows(a, heads, b, t):
        return a.reshape(1, b, t, heads, HEAD_DIM)

    return (y_p, y_s,
            rows(ak32, HKV_A, bp, tp), rows(av32, HKV_A, bp, tp), ik32.reshape(1, bp, tp, D_IDX),
            rows(bk32, H_B, bp, tp), rows(bv32, H_B, bp, tp), logf.reshape(1, bp, tp, H_B),
            rows(ak32_s, HKV_A, bs, ts), rows(av32_s, HKV_A, bs, ts), ik32_s.reshape(1, bs, ts, D_IDX),
            rows(bk32_s, H_B, bs, ts), rows(bv32_s, H_B, bs, ts), logf_s.reshape(1, bs, ts, H_B))
```

```python
import functools
import math

import numpy as np
import jax
import jax.numpy as jnp
from jax import lax
from jax.experimental import pallas as pl
from jax.experimental.pallas import tpu as pltpu

F32 = jnp.float32
BF16 = jnp.bfloat16
I32 = jnp.int32

D_MODEL = 1024
CHUNK = 64
HEAD_DIM = 64
H_A = 8
HKV_A = 2
H_B = 8
H_IDX = 8
D_IDX = 64
K_TOP_MAX = 256
NUM_BUCKETS = 32
MAX_DISTANCE = 128
D_FF = 4 * D_MODEL
EPS = 1e-6
IN_SPLITS = (H_A * HEAD_DIM, HKV_A * HEAD_DIM, HKV_A * HEAD_DIM, H_IDX * D_IDX, D_IDX, H_IDX,
             H_B * HEAD_DIM, H_B * HEAD_DIM, H_B * HEAD_DIM, H_B)

LANES = 128
SUBLANES = 8
VMEM_LIMIT = 56 * 1024 * 1024

TQ = 256
TK = 256
NEG = -1e30
LOG2E = 1.4426950408889634
KEY_NEG_INF = -2139095041
INT_MIN = -2147483648

C_AQ, C_IQ, C_BQ, C_AKV, C_BK, C_BV, C_IK2, C_SM, N_CAT = 0, 512, 1024, 1536, 1792, 2304, 2816, 2944, 3072
AQ_HEAD_ORDER = (0, 4, 1, 5, 2, 6, 3, 7)

NT_DIMS = (((1,), (1,)), ((), ()))


def _cparams(*sem):
    return pltpu.CompilerParams(dimension_semantics=sem, vmem_limit_bytes=VMEM_LIMIT)


def _resident(shape):
    nd = len(shape)
    return pl.BlockSpec(shape, lambda *_: (0,) * nd, pipeline_mode=pl.Buffered(1))


def _rms(x, g):
    ms = jnp.mean(x * x, axis=-1, keepdims=True)
    return (x * lax.rsqrt(ms + EPS)) * g


def _for_blocks(n, body):
    def pair(k, c):
        body(2 * k, 0)
        body(2 * k + 1, 1)
        return c

    lax.fori_loop(0, n // 2, pair, 0)

    @pl.when(n % 2 == 1)
    def _():
        body(n - 1, 0)


def _normalize_pair(a_lo, a_hi, low):
    num = jnp.where(low, a_lo, a_hi)
    den = pltpu.roll(jnp.where(low, a_hi, a_lo), HEAD_DIM, axis=1)
    return num / den


def _sortable(x):
    b = lax.bitcast_convert_type(x, I32)
    return b ^ (lax.shift_right_arithmetic(b, 31) & 0x7FFFFFFF)


def _t5_bucket_np(rel):
    half = NUM_BUCKETS // 2
    max_exact = half // 2
    ret = np.where(rel > 0, half, 0)
    n = np.abs(rel)
    n_f = np.maximum(n, max_exact).astype(np.float64)
    large = max_exact + (np.log(n_f / max_exact) / math.log(MAX_DISTANCE / max_exact)
                         * (half - max_exact)).astype(np.int32)
    large = np.minimum(large, half - 1)
    return (ret + np.where(n < max_exact, n, large)).astype(np.int32)


def _inproj_kernel(x_ref, g_ref, w_ref, bf_ref,
                   aq_ref, iq_ref, bq_ref, akv_ref, bk_ref, bv_ref, ik2_ref,
                   ak32_ref, av32_ref, ik32_ref, bk32_ref, bv32_ref, logf_ref, iw_ref):
    h = _rms(x_ref[0], g_ref[...])
    p = jnp.dot(h.astype(BF16), w_ref[...], preferred_element_type=F32)
    qscale = HEAD_DIM ** -0.5 * LOG2E
    aq_ref[0] = (p[:, C_AQ:C_AQ + 512] * qscale).astype(BF16)
    iq_ref[0] = (p[:, C_IQ:C_IQ + 512] * (D_IDX ** -0.5)).astype(BF16)
    bq_ref[0] = (p[:, C_BQ:C_BQ + 512] * qscale).astype(BF16)
    low = lax.broadcasted_iota(I32, (p.shape[0], LANES), 1) < HEAD_DIM
    av = p[:, C_AKV + LANES:C_AKV + 2 * LANES]
    akv_ref[0, :, 0:LANES] = p[:, C_AKV:C_AKV + LANES].astype(BF16)
    akv_ref[0, :, LANES:2 * LANES] = jnp.where(low, av, 1.0).astype(BF16)
    akv_ref[0, :, 2 * LANES:3 * LANES] = jnp.where(low, 1.0, av).astype(BF16)
    bk_ref[0] = p[:, C_BK:C_BK + 512].astype(BF16)
    for q in range(4):
        bv = p[:, C_BV + LANES * q:C_BV + LANES * (q + 1)]
        bv_ref[0, :, 2 * q * LANES:(2 * q + 1) * LANES] = jnp.where(low, bv, 1.0).astype(BF16)
        bv_ref[0, :, (2 * q + 1) * LANES:(2 * q + 2) * LANES] = jnp.where(low, 1.0, bv).astype(BF16)
    ik2_ref[0] = p[:, C_IK2:C_IK2 + 128].astype(BF16)
    ak32_ref[0] = p[:, C_AKV:C_AKV + 128]
    av32_ref[0] = p[:, C_AKV + 128:C_AKV + 256]
    ik32_ref[0] = p[:, C_IK2:C_IK2 + 64]
    bk32_ref[0] = p[:, C_BK:C_BK + 512]
    bv32_ref[0] = p[:, C_BV:C_BV + 512]
    iw_ref[0] = p[:, C_SM:C_SM + 8] * (H_IDX ** -0.5)
    z = p[:, C_SM + 8:C_SM + 16] + bf_ref[...]
    logf_ref[0] = jnp.minimum(z, 0.0) - jnp.log1p(jnp.exp(-jnp.abs(z)))


def _inproj(x, g, w_cat, b_f, tm):
    b, t, d = x.shape
    grid = (b, t // tm)
    row = lambda c: pl.BlockSpec((1, tm, c), lambda i, j: (i, j, 0))
    outs = [(512, BF16), (512, BF16), (512, BF16), (384, BF16), (512, BF16), (1024, BF16), (128, BF16),
            (128, F32), (128, F32), (64, F32), (512, F32), (512, F32), (8, F32), (8, F32)]
    return pl.pallas_call(
        _inproj_kernel,
        grid=grid,
        in_specs=[row(d), _resident((1, d)), _resident((d, N_CAT)), _resident((1, H_B))],
        out_specs=[row(c) for c, _ in outs],
        out_shape=[jax.ShapeDtypeStruct((b, t, c), dt) for c, dt in outs],
        compiler_params=_cparams("parallel", "parallel"),
        name="inproj",
    )(x, g, w_cat, b_f)


CS_BLK = 512


def _cumsum_kernel(x_ref, out_ref, carry_ref):
    @pl.when(pl.program_id(0) == 0)
    def _():
        carry_ref[...] = jnp.zeros_like(carry_ref)

    x = x_ref[...] * LOG2E
    hi = x.astype(BF16)
    r1 = x - hi.astype(F32)
    mid = r1.astype(BF16)
    lo = (r1 - mid.astype(F32)).astype(BF16)
    src = lax.broadcasted_iota(I32, (CS_BLK, CS_BLK), 0)
    dst = lax.broadcasted_iota(I32, (CS_BLK, CS_BLK), 1)
    tri = jnp.where(src <= dst, 1.0, 0.0).astype(BF16)
    cs = (jnp.dot(hi, tri, preferred_element_type=F32)
          + jnp.dot(mid, tri, preferred_element_type=F32)
          + jnp.dot(lo, tri, preferred_element_type=F32))
    out = cs + carry_ref[...]
    out_ref[...] = out
    carry_ref[...] = jnp.broadcast_to(out[:, CS_BLK - 1:CS_BLK], carry_ref.shape)


def _cumsum_lanes(x_t):
    b, h, l = x_t.shape
    spec = pl.BlockSpec((b * h, CS_BLK), lambda j: (0, j))
    return pl.pallas_call(
        _cumsum_kernel,
        grid=(l // CS_BLK,),
        in_specs=[spec],
        out_specs=spec,
        out_shape=jax.ShapeDtypeStruct((b * h, l), F32),
        scratch_shapes=[pltpu.VMEM((b * h, CS_BLK), F32)],
        compiler_params=_cparams("arbitrary"),
        name="cumsum",
    )(x_t.reshape(b * h, l)).reshape(b, h, l)


def _bias_kernel(rb_ref, bucket_ref, out_ref):
    b = bucket_ref[0]
    for h in range(H_A):
        acc = jnp.zeros(b.shape, F32)
        for k in range(NUM_BUCKETS):
            acc = jnp.where(b == k, rb_ref[k, h] * LOG2E, acc)
        out_ref[0, h] = acc


def _bias_tiles(rel_bias, bucket):
    n, r, c = bucket.shape
    return pl.pallas_call(
        _bias_kernel,
        grid=(n,),
        in_specs=[pl.BlockSpec(memory_space=pltpu.SMEM),
                  pl.BlockSpec((1, r, c), lambda i: (i, 0, 0))],
        out_specs=pl.BlockSpec((1, H_A, r, c), lambda i: (i, 0, 0, 0)),
        out_shape=jax.ShapeDtypeStruct((n, H_A, r, c), F32),
        compiler_params=_cparams("parallel"),
        name="t5_bias",
    )(rel_bias, bucket)


def _dsa_prompt_kernel(rb_ref, aq_ref, iq_ref, iw_ref, ik2_ref, akv_ref, bias_ref, out_ref,
                       qi_ref, qa_ref, iwrep_ref, key_ref, keyt_ref, selb_ref, lg_ref, mx_ref,
                       p_ref, acc_ref, thr_ref, *, k_top, far_bucket, seq_len):
    i = pl.program_id(1)
    nkb = i + 1
    lane = lax.broadcasted_iota(I32, (TQ, LANES), 1)
    low = lane < HEAD_DIM

    for p in range(4):
        s_i = iq_ref[0, :, LANES * p:LANES * (p + 1)].astype(F32)
        qi_ref[(2 * p) * TQ:(2 * p + 1) * TQ, :] = jnp.where(low, s_i, 0.0).astype(BF16)
        qi_ref[(2 * p + 1) * TQ:(2 * p + 2) * TQ, :] = jnp.where(low, 0.0, s_i).astype(BF16)
        s_a = aq_ref[0, :, LANES * p:LANES * (p + 1)].astype(F32)
        qa_ref[p * TQ:(p + 1) * TQ, :] = jnp.where(low, s_a, 0.0).astype(BF16)
        qa_ref[(p + 4) * TQ:(p + 5) * TQ, :] = jnp.where(low, 0.0, s_a).astype(BF16)
    iw = iw_ref[0]
    for h in range(H_IDX):
        iwrep_ref[h] = jnp.broadcast_to(iw[:, h:h + 1], (TQ, TK))

    def blk(j):
        return pl.ds(pl.multiple_of(j * TK, TK), TK)

    def score_blk(j, slot):
        kb = ik2_ref[0, blk(j), :]
        acc = jnp.zeros((TQ, TK), F32)
        for h in range(H_IDX):
            d = lax.dot_general(qi_ref[h * TQ:(h + 1) * TQ, :], kb, NT_DIMS, preferred_element_type=F32)
            acc = acc + iwrep_ref[h] * jnp.maximum(d, 0.0)
        key_ref[:, blk(j)] = _sortable(acc)
        keyt_ref[blk(j), :] = _sortable(acc.T)

    _for_blocks(nkb, score_blk)

    r2 = lax.broadcasted_iota(I32, (TQ, TK), 0)
    c2 = lax.broadcasted_iota(I32, (TQ, TK), 1)
    key_ref[:, blk(i)] = jnp.where((c2 // CHUNK) <= (r2 // CHUNK), key_ref[:, blk(i)], KEY_NEG_INF)
    keyt_ref[blk(i), :] = jnp.where((r2 // CHUNK) <= (c2 // CHUNK), keyt_ref[blk(i), :], KEY_NEG_INF)

    sub = lax.broadcasted_iota(I32, (SUBLANES, TQ), 0)

    def count(pred):
        def body(j, accs):
            base = pl.multiple_of(j * TK, TK)
            accs = list(accs)
            kblk = keyt_ref[pl.ds(base, TK), :]
            for g in range(TK // SUBLANES):
                kb = kblk[SUBLANES * g:SUBLANES * (g + 1), :]
                hit = jnp.where(pred(kb, base + SUBLANES * g + sub), 1.0, 0.0)
                accs[g % len(accs)] = accs[g % len(accs)] + hit
            return tuple(accs)
        zero = jnp.zeros((SUBLANES, TQ), F32)
        a = lax.fori_loop(0, nkb, body, (zero, zero, zero, zero))
        tot = jnp.sum((a[0] + a[1]) + (a[2] + a[3]), axis=0, keepdims=True)
        return jnp.broadcast_to(tot, (SUBLANES, TQ))

    kf = float(k_top)
    c0 = count(lambda kb, idx: kb >= 0)
    ans = jnp.where(c0 >= kf, 0, INT_MIN).astype(I32)

    def bit_body(t, ans):
        cand = ans | lax.shift_left(jnp.int32(1), 30 - t)
        c = count(lambda kb, idx: kb >= cand)
        return jnp.where(c >= kf, cand, ans)

    ans = lax.fori_loop(0, 31, bit_body, ans)

    cgt = count(lambda kb, idx: kb > ans)
    ceq = count(lambda kb, idx: kb == ans)
    need = kf - cgt
    tie = (ceq > need) & (ans > KEY_NEG_INF)
    flag = jnp.max(jnp.where(tie[0:1, :], 1.0, 0.0), axis=1, keepdims=True)
    thr_ref[...] = jnp.full((SUBLANES, TQ), seq_len, I32)

    @pl.when(flag[0, 0] > 0.0)
    def _():
        nbits = int(seq_len - 1).bit_length()

        def tie_body(t, m):
            cand = m | lax.shift_left(jnp.int32(1), nbits - 1 - t)
            c = count(lambda kb, idx: (kb == ans) & (idx < cand))
            return jnp.where(c < need, cand, m)

        m = lax.fori_loop(0, nbits, tie_body, jnp.zeros((SUBLANES, TQ), I32))
        thr_ref[...] = jnp.where(tie, m, seq_len)

    def to_rows(x):
        return jnp.broadcast_to(x[0:1, :], (LANES, TQ)).T

    ans_r = to_rows(ans)
    thr_r = to_rows(thr_ref[...])

    def selb_blk(j, c):
        kb = key_ref[:, blk(j)]
        col = j * TK + lane
        for half in range(2):
            kh = kb[:, half * LANES:(half + 1) * LANES]
            ch = col + half * LANES
            sel = ((kh > ans_r) | ((kh == ans_r) & (ch <= thr_r))) & (kh > KEY_NEG_INF)
            selb_ref[:, pl.ds(pl.multiple_of(j * TK + half * LANES, LANES), LANES)] = jnp.where(sel, 0.0, NEG)
        return c

    lax.fori_loop(0, nkb, selb_blk, 0)

    for h in range(H_A):
        mx_ref[h] = jnp.full((TQ, LANES), NEG, F32)
    acc_ref[...] = jnp.zeros_like(acc_ref)

    def pass_a(j, bias_of_head):
        kb = akv_ref[0, blk(j), 0:LANES]
        sb = selb_ref[:, blk(j)]
        for h in range(H_A):
            s = lax.dot_general(qa_ref[h * TQ:(h + 1) * TQ, :], kb, NT_DIMS, preferred_element_type=F32)
            lg = s + sb + bias_of_head(h)
            lg_ref[h, :, blk(j)] = lg
            mx_ref[h] = jnp.maximum(mx_ref[h], jnp.maximum(lg[:, :LANES], lg[:, LANES:]))

    _for_blocks(jnp.maximum(i - 1, 0), lambda j, slot: pass_a(j, lambda h: rb_ref[far_bucket, h] * LOG2E))

    @pl.when(i >= 1)
    def _():
        pass_a(i - 1, lambda h: bias_ref[0, h])

    pass_a(i, lambda h: bias_ref[1, h])

    for h in range(H_A):
        mx_ref[h] = jnp.broadcast_to(jnp.max(mx_ref[h], axis=1, keepdims=True), (TQ, LANES))

    half = (H_A // 2) * TQ

    def pass_b(j, slot):
        for h in range(H_A):
            lg = lg_ref[h, :, blk(j)]
            m = mx_ref[h]
            p_ref[slot, h * TQ:(h + 1) * TQ, 0:LANES] = jnp.exp2(lg[:, :LANES] - m).astype(BF16)
            p_ref[slot, h * TQ:(h + 1) * TQ, LANES:TK] = jnp.exp2(lg[:, LANES:] - m).astype(BF16)
        for g in range(HKV_A):
            acc_ref[g * half:(g + 1) * half, :] += jnp.dot(
                p_ref[slot, g * half:(g + 1) * half, :], akv_ref[0, blk(j), (g + 1) * LANES:(g + 2) * LANES],
                preferred_element_type=F32)

    _for_blocks(nkb, pass_b)

    for p in range(4):
        out_ref[0, :, LANES * p:LANES * (p + 1)] = _normalize_pair(
            acc_ref[p * TQ:(p + 1) * TQ, :], acc_ref[(p + 4) * TQ:(p + 5) * TQ, :], low).astype(BF16)


def _dsa_prompt(rel_bias, aq, iq, iw, ik2, akv, bias_tiles, far_bucket):
    b, t, _ = aq.shape
    k_top = min(K_TOP_MAX, t // 4)
    qrow = lambda c: pl.BlockSpec((1, TQ, c), lambda i, j: (i, j, 0))
    full = lambda c: pl.BlockSpec((1, t, c), lambda i, j: (i, 0, 0))
    kern = functools.partial(_dsa_prompt_kernel, k_top=k_top, far_bucket=far_bucket, seq_len=t)
    return pl.pallas_call(
        kern,
        grid=(b, t // TQ),
        in_specs=[pl.BlockSpec(memory_space=pltpu.SMEM),
                  qrow(512), qrow(512), qrow(8), full(128), full(384),
                  _resident((2, H_A, TQ, TK))],
        out_specs=qrow(512),
        out_shape=jax.ShapeDtypeStruct((b, t, 512), BF16),
        scratch_shapes=[
            pltpu.VMEM((H_IDX * TQ, LANES), BF16),
            pltpu.VMEM((H_A * TQ, LANES), BF16),
            pltpu.VMEM((H_IDX, TQ, TK), F32),
            pltpu.VMEM((TQ, t), I32),
            pltpu.VMEM((t, TQ), I32),
            pltpu.VMEM((TQ, t), F32),
            pltpu.VMEM((H_A, TQ, t), F32),
            pltpu.VMEM((H_A, TQ, LANES), F32),
            pltpu.VMEM((2, H_A * TQ, TK), BF16),
            pltpu.VMEM((H_A * TQ, LANES), F32),
            pltpu.VMEM((SUBLANES, TQ), I32),
        ],
        compiler_params=_cparams("parallel", "parallel"),
        name="dsa_prompt",
    )(rel_bias, aq, iq, iw, ik2, akv, bias_tiles)


def _fox_prompt_kernel(bq_ref, bk_ref, bv_ref, fq_ref, fkt_ref, out_ref,
                       qb_ref, fqrep_ref, lg_ref, mx_ref, p_ref, acc_ref):
    i = pl.program_id(1)
    lane = lax.broadcasted_iota(I32, (TQ, LANES), 1)
    low = lane < HEAD_DIM

    for p in range(4):
        s_b = bq_ref[0, :, LANES * p:LANES * (p + 1)].astype(F32)
        qb_ref[(2 * p) * TQ:(2 * p + 1) * TQ, :] = jnp.where(low, s_b, 0.0).astype(BF16)
        qb_ref[(2 * p + 1) * TQ:(2 * p + 2) * TQ, :] = jnp.where(low, 0.0, s_b).astype(BF16)
    fq = fq_ref[0]
    for h in range(H_B):
        fqrep_ref[h] = jnp.broadcast_to(fq[:, h:h + 1], (TQ, LANES))
        mx_ref[h] = jnp.full((TQ, LANES), NEG, F32)
    acc_ref[...] = jnp.zeros_like(acc_ref)

    def blk(j):
        return pl.ds(pl.multiple_of(j * TK, TK), TK)

    r2 = lax.broadcasted_iota(I32, (TQ, LANES), 0)

    def pass_a(j, diagonal):
        for p in range(4):
            s = lax.dot_general(qb_ref[(2 * p) * TQ:(2 * p + 2) * TQ, :],
                                bk_ref[0, blk(j), LANES * p:LANES * (p + 1)],
                                NT_DIMS, preferred_element_type=F32)
            for e in range(2):
                h = 2 * p + e
                fk = fkt_ref[0, h:h + 1, blk(j)]
                halves = []
                for half in range(2):
                    lg = (s[e * TQ:(e + 1) * TQ, half * LANES:(half + 1) * LANES]
                          + (fqrep_ref[h] - fk[:, half * LANES:(half + 1) * LANES]))
                    if diagonal:
                        lg = jnp.where(lane + half * LANES <= r2, lg, NEG)
                    lg_ref[h, :, pl.ds(pl.multiple_of(j * TK + half * LANES, LANES), LANES)] = lg
                    halves.append(lg)
                mx_ref[h] = jnp.maximum(mx_ref[h], jnp.maximum(halves[0], halves[1]))

    _for_blocks(i, lambda j, slot: pass_a(j, False))
    pass_a(i, True)

    for h in range(H_B):
        mx_ref[h] = jnp.broadcast_to(jnp.max(mx_ref[h], axis=1, keepdims=True), (TQ, LANES))

    def pass_b(j, slot):
        for h in range(H_B):
            lg = lg_ref[h, :, blk(j)]
            m = mx_ref[h]
            p_ref[slot, h * TQ:(h + 1) * TQ, 0:LANES] = jnp.exp2(lg[:, :LANES] - m).astype(BF16)
            p_ref[slot, h * TQ:(h + 1) * TQ, LANES:TK] = jnp.exp2(lg[:, LANES:] - m).astype(BF16)
            acc_ref[h * TQ:(h + 1) * TQ, :] += jnp.dot(
                p_ref[slot, h * TQ:(h + 1) * TQ, :], bv_ref[0, blk(j), LANES * h:LANES * (h + 1)],
                preferred_element_type=F32)

    _for_blocks(i + 1, pass_b)

    for p in range(4):
        out_ref[0, :, LANES * p:LANES * (p + 1)] = _normalize_pair(
            acc_ref[(2 * p) * TQ:(2 * p + 1) * TQ, :], acc_ref[(2 * p + 1) * TQ:(2 * p + 2) * TQ, :], low).astype(BF16)


def _fox_prompt(bq, bk, bv, fq, fkt):
    b, t, _ = bq.shape
    qrow = lambda c: pl.BlockSpec((1, TQ, c), lambda i, j: (i, j, 0))
    full = lambda c: pl.BlockSpec((1, t, c), lambda i, j: (i, 0, 0))
    return pl.pallas_call(
        _fox_prompt_kernel,
        grid=(b, t // TQ),
        in_specs=[qrow(512), full(512), full(1024), qrow(8),
                  pl.BlockSpec((1, H_B, t), lambda i, j: (i, 0, 0))],
        out_specs=qrow(512),
        out_shape=jax.ShapeDtypeStruct((b, t, 512), BF16),
        scratch_shapes=[
            pltpu.VMEM((H_B * TQ, LANES), BF16),
            pltpu.VMEM((H_B, TQ, LANES), F32),
            pltpu.VMEM((H_B, TQ, t), F32),
            pltpu.VMEM((H_B, TQ, LANES), F32),
            pltpu.VMEM((2, H_B * TQ, TK), BF16),
            pltpu.VMEM((H_B * TQ, LANES), F32),
        ],
        compiler_params=_cparams("parallel", "parallel"),
        name="fox_prompt",
    )(bq, bk, bv, fq, fkt)


FF_BLK = 1024


def _tail_kernel(x_ref, oa_ref, ob_ref, woa_ref, wob_ref, gm_ref, wup_ref, wdn_ref, gf_ref, y_ref):
    mixed = (jnp.dot(oa_ref[...], woa_ref[...], preferred_element_type=F32)
             + jnp.dot(ob_ref[...], wob_ref[...], preferred_element_type=F32))
    x1 = x_ref[...] + mixed
    h2 = _rms(x1, gm_ref[...]).astype(BF16)
    acc = x1
    for c in range(D_FF // FF_BLK):
        u = jnp.dot(h2, wup_ref[:, c * FF_BLK:(c + 1) * FF_BLK], preferred_element_type=F32)
        u = jnp.square(jnp.maximum(u, 0.0)).astype(BF16)
        acc = acc + jnp.dot(u, wdn_ref[c * FF_BLK:(c + 1) * FF_BLK, :], preferred_element_type=F32)
    y_ref[...] = _rms(acc, gf_ref[...])


def _tail(x, oa, ob, woa, wob, g_mlp, w_up, w_down, g_final, tm):
    n, d = x.shape
    row = lambda c: pl.BlockSpec((tm, c), lambda i: (i, 0))
    return pl.pallas_call(
        _tail_kernel,
        grid=(n // tm,),
        in_specs=[row(d), row(512), row(512), _resident((512, d)), _resident((512, d)), _resident((1, d)),
                  _resident((d, D_FF)), _resident((D_FF, d)), _resident((1, d))],
        out_specs=row(d),
        out_shape=jax.ShapeDtypeStruct((n, d), F32),
        compiler_params=_cparams("parallel"),
        name="tail",
    )(x, oa, ob, woa, wob, g_mlp, w_up, w_down, g_final)


def _dsa_sample_kernel(aq_ref, iq_ref, iw_ref, ik2_ref, akv_ref, cikt_ref, cakt_ref, cavt_ref, bias_ref, out_ref,
                       *, past, n_new, l_pad, k_top):
    l_all = past + n_new
    n_tail = l_pad - past
    lane = lax.broadcasted_iota(I32, (n_new, LANES), 1)
    low = lane < HEAD_DIM

    def pad_rows(x):
        return jnp.concatenate([x, jnp.zeros((n_tail - n_new, x.shape[1]), F32)], axis=0).astype(BF16)

    akv = akv_ref[0].astype(F32)
    ik_new = pad_rows(ik2_ref[0].astype(F32)[:, 0:D_IDX])
    ak_new = pad_rows(akv[:, 0:LANES])
    av_new = pad_rows(jnp.where(low, akv[:, LANES:2 * LANES], akv[:, 2 * LANES:3 * LANES]))

    iq32 = iq_ref[0].astype(F32)
    qi = jnp.concatenate([iq32[:, D_IDX * h:D_IDX * (h + 1)] for h in range(H_IDX)], axis=0).astype(BF16)
    d = jnp.concatenate(
        [jnp.dot(qi, cikt_ref[0].astype(BF16), preferred_element_type=F32),
         lax.dot_general(qi, ik_new, NT_DIMS, preferred_element_type=F32)], axis=1)
    iw = iw_ref[0]
    score = jnp.zeros((n_new, l_pad), F32)
    for h in range(H_IDX):
        score = score + iw[:, h:h + 1] * jnp.maximum(d[h * n_new:(h + 1) * n_new, :], 0.0)
    row = lax.broadcasted_iota(I32, (n_new, l_pad), 0)
    col = lax.broadcasted_iota(I32, (n_new, l_pad), 1)
    adm = (col < l_all) & ((col // CHUNK) <= ((past + row) // CHUNK))
    key = jnp.where(adm, _sortable(score), KEY_NEG_INF)

    def count(mask):
        return jnp.sum(jnp.where(mask, 1.0, 0.0), axis=1, keepdims=True)

    kf = float(k_top)
    ans = jnp.where(count(key >= 0) >= kf, 0, INT_MIN).astype(I32)

    def bit_body(t, ans):
        cand = ans | lax.shift_left(jnp.int32(1), 30 - t)
        return jnp.where(count(key >= cand) >= kf, cand, ans)

    ans = lax.fori_loop(0, 31, bit_body, ans)
    eq = key == ans
    need = kf - count(key > ans)
    tie = (count(eq) > need) & (ans > KEY_NEG_INF)
    flag = jnp.max(jnp.where(tie, 1.0, 0.0), axis=0, keepdims=True)
    nbits = int(l_pad - 1).bit_length()

    def tie_search(_):
        def tie_body(t, m):
            cand = m | lax.shift_left(jnp.int32(1), nbits - 1 - t)
            return jnp.where(count(eq & (col < cand)) < need, cand, m)
        m = lax.fori_loop(0, nbits, tie_body, jnp.zeros((n_new, 1), I32))
        return jnp.where(tie, m, l_pad)

    thr = lax.cond(flag[0, 0] > 0.0, tie_search, lambda _: jnp.full((n_new, 1), l_pad, I32), 0)
    sel = ((key > ans) | (eq & (col <= thr))) & (key > KEY_NEG_INF)
    selb = jnp.where(sel, 0.0, NEG)

    aq32 = aq_ref[0].astype(F32)
    slabs = [aq32[:, LANES * p:LANES * (p + 1)] for p in range(4)]
    qa = jnp.concatenate([jnp.where(low, s, 0.0) for s in slabs] + [jnp.where(low, 0.0, s) for s in slabs],
                         axis=0).astype(BF16)
    s = jnp.concatenate(
        [jnp.dot(qa, cakt_ref[0].astype(BF16), preferred_element_type=F32),
         lax.dot_general(qa, ak_new, NT_DIMS, preferred_element_type=F32)], axis=1)
    lg = s + bias_ref[...] + jnp.concatenate([selb] * H_A, axis=0)
    m = jnp.max(lg, axis=1, keepdims=True)
    p = jnp.exp2(lg - m)
    l = jnp.sum(p, axis=1, keepdims=True)
    pb = p.astype(BF16)
    o = (lax.dot_general(pb[:, 0:past], cavt_ref[0].astype(BF16), NT_DIMS, preferred_element_type=F32)
         + jnp.dot(pb[:, past:l_pad], av_new, preferred_element_type=F32)) / l
    for q in range(4):
        out_ref[0, :, LANES * q:LANES * (q + 1)] = jnp.where(
            low, o[q * n_new:(q + 1) * n_new, :], o[(q + 4) * n_new:(q + 5) * n_new, :]).astype(BF16)


def _dsa_sample(aq, iq, iw, ik2, akv, cikt, cakt, cavt, bias_rows, past):
    b, n_new, _ = aq.shape
    l_pad = bias_rows.shape[1]
    assert past % LANES == 0 and l_pad == past + LANES and n_new <= LANES
    k_top = min(K_TOP_MAX, (past + n_new) // 4)
    new = lambda c: pl.BlockSpec((1, n_new, c), lambda i: (i, 0, 0))
    old = lambda c: pl.BlockSpec((1, c, past), lambda i: (i, 0, 0))
    kern = functools.partial(_dsa_sample_kernel, past=past, n_new=n_new, l_pad=l_pad, k_top=k_top)
    return pl.pallas_call(
        kern,
        grid=(b,),
        in_specs=[new(512), new(512), new(8), new(128), new(384), old(D_IDX), old(LANES), old(LANES),
                  _resident((H_A * n_new, l_pad))],
        out_specs=new(512),
        out_shape=jax.ShapeDtypeStruct((b, n_new, 512), BF16),
        compiler_params=_cparams("parallel"),
        name="dsa_sample",
    )(aq, iq, iw, ik2, akv, cikt, cakt, cavt, bias_rows)


def _fox_sample_kernel(bq_ref, bkn_ref, bvn_ref, ckt_ref, cvt_ref, fkt_ref, tot_ref, lf_ref, lft_ref, out_ref,
                       *, past, n_new):
    lf = lf_ref[0] * LOG2E
    lft = lft_ref[0] * LOG2E
    rown = lax.broadcasted_iota(I32, (n_new, H_B), 0)
    lanen = lax.broadcasted_iota(I32, (H_B, n_new), 1)
    fq = jnp.broadcast_to(tot_ref[0], (n_new, H_B))
    fqt = jnp.broadcast_to(fkt_ref[0][:, past - 1:past], (H_B, n_new))
    for s in range(n_new):
        fq = fq + jnp.where(rown >= s, lf[s:s + 1, :], 0.0)
        fqt = fqt + jnp.where(lanen >= s, lft[:, s:s + 1], 0.0)

    kct = ckt_ref[0].astype(BF16)
    vct = cvt_ref[0].astype(BF16)
    bq32 = bq_ref[0].astype(F32)
    head_of_lane = lax.broadcasted_iota(I32, (n_new, H_B * HEAD_DIM), 1) // HEAD_DIM
    qb = jnp.concatenate([jnp.where(head_of_lane == h, bq32, 0.0) for h in range(H_B)], axis=0).astype(BF16)
    s_past = jnp.dot(qb, kct, preferred_element_type=F32)
    s_new = lax.dot_general(qb, bkn_ref[0], NT_DIMS, preferred_element_type=F32)
    fkt = fkt_ref[0]
    fq_col = jnp.concatenate([fq[:, h:h + 1] for h in range(H_B)], axis=0)
    fk_past = jnp.concatenate([jnp.broadcast_to(fkt[h:h + 1, :], (n_new, past)) for h in range(H_B)], axis=0)
    fk_new = jnp.concatenate([jnp.broadcast_to(fqt[h:h + 1, :], (n_new, n_new)) for h in range(H_B)], axis=0)
    lg_past = s_past + (fq_col - fk_past)
    trow = lax.broadcasted_iota(I32, (H_B * n_new, n_new), 0) % n_new
    tcol = lax.broadcasted_iota(I32, (H_B * n_new, n_new), 1)
    lg_new = jnp.where(tcol <= trow, s_new + (fq_col - fk_new), NEG)
    m = jnp.maximum(jnp.max(lg_past, axis=1, keepdims=True), jnp.max(lg_new, axis=1, keepdims=True))
    p_past = jnp.exp2(lg_past - m)
    p_new = jnp.exp2(lg_new - m)
    l = jnp.sum(p_past, axis=1, keepdims=True) + jnp.sum(p_new, axis=1, keepdims=True)
    o = (lax.dot_general(p_past.astype(BF16), vct, NT_DIMS, preferred_element_type=F32)
         + jnp.dot(p_new.astype(BF16), bvn_ref[0].astype(BF16), preferred_element_type=F32)) / l
    out = jnp.zeros((n_new, H_B * HEAD_DIM), F32)
    for h in range(H_B):
        out = out + jnp.where(head_of_lane == h, o[h * n_new:(h + 1) * n_new, :], 0.0)
    out_ref[0] = out.astype(BF16)


def _fox_sample(bq, bkn, bvn, ckt, cvt, fkt, tot, lf, lft, past):
    b, n_new, _ = bq.shape
    new = lambda c: pl.BlockSpec((1, n_new, c), lambda i: (i, 0, 0))
    old = pl.BlockSpec((1, H_B * HEAD_DIM, past), lambda i: (i, 0, 0))
    kern = functools.partial(_fox_sample_kernel, past=past, n_new=n_new)
    return pl.pallas_call(
        kern,
        grid=(b,),
        in_specs=[new(512), new(512), new(512), old, old,
                  pl.BlockSpec((1, H_B, past), lambda i: (i, 0, 0)),
                  pl.BlockSpec((1, 1, H_B), lambda i: (i, 0, 0)),
                  new(H_B),
                  pl.BlockSpec((1, H_B, n_new), lambda i: (i, 0, 0))],
        out_specs=new(512),
        out_shape=jax.ShapeDtypeStruct((b, n_new, 512), BF16),
        compiler_params=_cparams("parallel"),
        name="fox_sample",
    )(bq, bkn, bvn, ckt, cvt, fkt, tot, lf, lft)


def _prep_weights(w_in, w_o, w_up, w_down):
    offs = np.concatenate([[0], np.cumsum(IN_SPLITS)])
    seg = lambda k: w_in[:, int(offs[k]):int(offs[k + 1])]
    a_q, a_k, a_v, i_q, i_k, i_w, b_q, b_k, b_v, f_z = (seg(k) for k in range(10))
    a_q_pairs = jnp.concatenate([a_q[:, HEAD_DIM * h:HEAD_DIM * (h + 1)] for h in AQ_HEAD_ORDER], axis=1)
    pad = jnp.zeros((w_in.shape[0], N_CAT - C_SM - 16), w_in.dtype)
    w_cat = jnp.concatenate([a_q_pairs, i_q, b_q, a_k, a_v, b_k, b_v, i_k, i_k, i_w, f_z, pad], axis=1)
    wo_a = jnp.concatenate([w_o[HEAD_DIM * h:HEAD_DIM * (h + 1)] for h in AQ_HEAD_ORDER], axis=0)
    wo_b = w_o[H_A * HEAD_DIM:]
    return (w_cat.astype(BF16), wo_a.astype(BF16), wo_b.astype(BF16), w_up.astype(BF16), w_down.astype(BF16))


def kernel(x_prompt, x_sample, cache_a_k, cache_a_v, cache_idx_k, cache_b_k, cache_b_v, cache_b_logf,
           w_in, w_o, b_f, rel_bias, g_attn, w_up, w_down, g_mlp, g_final):
    assert w_in.shape[0] == 1, "single-layer trunk"
    bp, tp, d = x_prompt.shape
    bs, ts, _ = x_sample.shape
    past = cache_a_k.shape[2]
    w_cat, wo_a, wo_b, wup, wdn = _prep_weights(w_in[0], w_o[0], w_up[0], w_down[0])
    g_a = g_attn[0].reshape(1, d)
    g_m = g_mlp[0].reshape(1, d)
    g_f = g_final.reshape(1, d)
    bf = b_f[0].reshape(1, H_B)
    rel_bias = rel_bias.astype(F32)

    (aq, iq, bq, akv, bk, bv, ik2, ak32, av32, ik32, bk32, bv32, logf, iw) = _inproj(x_prompt, g_a, w_cat, bf, 512)
    fkt = _cumsum_lanes(jnp.swapaxes(logf, 1, 2))
    fq = jnp.swapaxes(fkt, 1, 2)
    r = np.arange(TQ)[:, None]
    c = np.arange(TK)[None, :]
    bucket = _t5_bucket_np(np.stack([c - TK - r, c - r]))
    far_bucket = int(_t5_bucket_np(np.array(-TK - 1)))
    assert far_bucket == int(_t5_bucket_np(np.array(-tp)))
    bias_tiles = _bias_tiles(rel_bias, jnp.asarray(bucket))
    out_a = _dsa_prompt(rel_bias, aq, iq, iw, ik2, akv, bias_tiles, far_bucket)
    out_b = _fox_prompt(bq, bk, bv, fq, fkt)
    y_p = _tail(x_prompt.reshape(bp * tp, d), out_a.reshape(bp * tp, 512), out_b.reshape(bp * tp, 512),
                wo_a, wo_b, g_m, wup, wdn, g_f, 512).reshape(bp, tp, d)

    n_s = bs * ts
    outs = _inproj(x_sample.reshape(1, n_s, d), g_a, w_cat, bf, n_s)
    (aq_s, iq_s, bq_s, akv_s, bk_s, bv_s, ik2_s, ak32_s, av32_s, ik32_s, bk32_s, bv32_s, logf_s, iw_s) = (
        o.reshape(bs, ts, o.shape[-1]) for o in outs)
    l_all = past + ts
    l_pad = -(-l_all // LANES) * LANES
    rel_s = np.arange(l_pad)[None, :] - (past + np.arange(ts))[:, None]
    bias_s = _bias_tiles(rel_bias, jnp.asarray(_t5_bucket_np(rel_s))[None])[0].reshape(H_A * ts, l_pad)
    def feature_major(cache):
        c = cache[0]
        c = jnp.transpose(c, (0, 2, 3, 1)) if c.ndim == 4 else jnp.transpose(c, (0, 2, 1))
        return c.reshape(bs, -1, past)

    out_a_s = _dsa_sample(aq_s, iq_s, iw_s, ik2_s, akv_s, feature_major(cache_idx_k),
                          feature_major(cache_a_k), feature_major(cache_a_v), bias_s, past)
    fkt_c = _cumsum_lanes(jnp.swapaxes(cache_b_logf[0].astype(F32), 1, 2))
    tot = fkt_c[:, :, past - 1].reshape(bs, 1, H_B)
    out_b_s = _fox_sample(bq_s, bk_s, bv32_s, feature_major(cache_b_k), feature_major(cache_b_v), fkt_c, tot,
                          logf_s, jnp.swapaxes(logf_s, 1, 2), past)
    y_s = _tail(x_sample.reshape(n_s, d), out_a_s.reshape(n_s, 512), out_b_s.reshape(n_s, 512),
                wo_a, wo_b, g_m, wup, wdn, g_f, n_s).reshape(bs, ts, d)

    def rows(a, heads, b, t):
        return a.reshape(1, b, t, heads, HEAD_DIM)

    return (y_p, y_s,
            rows(ak32, HKV_A, bp, tp), rows(av32, HKV_A, bp, tp), ik32.reshape(1, bp, tp, D_IDX),
            rows(bk32, H_B, bp, tp), rows(bv32, H_B, bp, tp), logf.reshape(1, bp, tp, H_B),
            rows(ak32_s, HKV_A, bs, ts), rows(av32_s, HKV_A, bs, ts), ik32_s.reshape(1, bs, ts, D_IDX),
            rows(bk32_s, H_B, bs, ts), rows(bv32_s, H_B, bs, ts), logf_s.reshape(1, bs, ts, H_B))
```

```python
import functools
import math

import numpy as np
import jax
import jax.numpy as jnp
from jax import lax
from jax.experimental import pallas as pl
from jax.experimental.pallas import tpu as pltpu

F32 = jnp.float32
BF16 = jnp.bfloat16
I32 = jnp.int32

D_MODEL = 1024
CHUNK = 64
HEAD_DIM = 64
H_A = 8
HKV_A = 2
H_B = 8
H_IDX = 8
D_IDX = 64
K_TOP_MAX = 256
NUM_BUCKETS = 32
MAX_DISTANCE = 128
D_FF = 4 * D_MODEL
EPS = 1e-6
IN_SPLITS = (H_A * HEAD_DIM, HKV_A * HEAD_DIM, HKV_A * HEAD_DIM, H_IDX * D_IDX, D_IDX, H_IDX,
             H_B * HEAD_DIM, H_B * HEAD_DIM, H_B * HEAD_DIM, H_B)

LANES = 128
SUBLANES = 8
VMEM_LIMIT = 56 * 1024 * 1024

TQ = 256
TK = 256
NEG = -1e30
LOG2E = 1.4426950408889634
KEY_NEG_INF = -2139095041
INT_MIN = -2147483648

C_AQ, C_IQ, C_BQ, C_AKV, C_BK, C_BV, C_IK2, C_SM, N_CAT = 0, 512, 1024, 1536, 1792, 2304, 2816, 2944, 3072
AQ_HEAD_ORDER = (0, 4, 1, 5, 2, 6, 3, 7)

NT_DIMS = (((1,), (1,)), ((), ()))


def _cparams(*sem):
    return pltpu.CompilerParams(dimension_semantics=sem, vmem_limit_bytes=VMEM_LIMIT)


def _resident(shape):
    nd = len(shape)
    return pl.BlockSpec(shape, lambda *_: (0,) * nd, pipeline_mode=pl.Buffered(1))


def _rms(x, g):
    ms = jnp.mean(x * x, axis=-1, keepdims=True)
    return (x * lax.rsqrt(ms + EPS)) * g


def _for_blocks(n, body):
    def pair(k, c):
        body(2 * k, 0)
        body(2 * k + 1, 1)
        return c

    lax.fori_loop(0, n // 2, pair, 0)

    @pl.when(n % 2 == 1)
    def _():
        body(n - 1, 0)


def _normalize_pair(a_lo, a_hi, low):
    num = jnp.where(low, a_lo, a_hi)
    den = pltpu.roll(jnp.where(low, a_hi, a_lo), HEAD_DIM, axis=1)
    return num / den


def _key_to_float(k):
    return lax.bitcast_convert_type(k ^ (lax.shift_right_arithmetic(k, 31) & 0x7FFFFFFF), F32)


def _t5_bucket_np(rel):
    half = NUM_BUCKETS // 2
    max_exact = half // 2
    ret = np.where(rel > 0, half, 0)
    n = np.abs(rel)
    n_f = np.maximum(n, max_exact).astype(np.float64)
    large = max_exact + (np.log(n_f / max_exact) / math.log(MAX_DISTANCE / max_exact)
                         * (half - max_exact)).astype(np.int32)
    large = np.minimum(large, half - 1)
    return (ret + np.where(n < max_exact, n, large)).astype(np.int32)


def _inproj_kernel(x_ref, g_ref, w_ref, bf_ref,
                   aq_ref, iq_ref, bq_ref, akv_ref, bk_ref, bv_ref, ik2_ref,
                   ak32_ref, av32_ref, ik32_ref, bk32_ref, bv32_ref, logf_ref, iw_ref):
    h = _rms(x_ref[0], g_ref[...])
    p = jnp.dot(h.astype(BF16), w_ref[...], preferred_element_type=F32)
    qscale = HEAD_DIM ** -0.5 * LOG2E
    aq_ref[0] = (p[:, C_AQ:C_AQ + 512] * qscale).astype(BF16)
    iq_ref[0] = (p[:, C_IQ:C_IQ + 512] * (D_IDX ** -0.5)).astype(BF16)
    bq_ref[0] = (p[:, C_BQ:C_BQ + 512] * qscale).astype(BF16)
    low = lax.broadcasted_iota(I32, (p.shape[0], LANES), 1) < HEAD_DIM
    av = p[:, C_AKV + LANES:C_AKV + 2 * LANES]
    akv_ref[0, :, 0:LANES] = p[:, C_AKV:C_AKV + LANES].astype(BF16)
    akv_ref[0, :, LANES:2 * LANES] = jnp.where(low, av, 1.0).astype(BF16)
    akv_ref[0, :, 2 * LANES:3 * LANES] = jnp.where(low, 1.0, av).astype(BF16)
    bk_ref[0] = p[:, C_BK:C_BK + 512].astype(BF16)
    for q in range(4):
        bv = p[:, C_BV + LANES * q:C_BV + LANES * (q + 1)]
        bv_ref[0, :, 2 * q * LANES:(2 * q + 1) * LANES] = jnp.where(low, bv, 1.0).astype(BF16)
        bv_ref[0, :, (2 * q + 1) * LANES:(2 * q + 2) * LANES] = jnp.where(low, 1.0, bv).astype(BF16)
    ik2_ref[0] = p[:, C_IK2:C_IK2 + 128].astype(BF16)
    ak32_ref[0] = p[:, C_AKV:C_AKV + 128]
    av32_ref[0] = p[:, C_AKV + 128:C_AKV + 256]
    ik32_ref[0] = p[:, C_IK2:C_IK2 + 64]
    bk32_ref[0] = p[:, C_BK:C_BK + 512]
    bv32_ref[0] = p[:, C_BV:C_BV + 512]
    iw_ref[0] = p[:, C_SM:C_SM + 8] * (H_IDX ** -0.5)
    z = p[:, C_SM + 8:C_SM + 16] + bf_ref[...]
    logf_ref[0] = jnp.minimum(z, 0.0) - jnp.log1p(jnp.exp(-jnp.abs(z)))


def _inproj(x, g, w_cat, b_f, tm):
    b, t, d = x.shape
    grid = (b, t // tm)
    row = lambda c: pl.BlockSpec((1, tm, c), lambda i, j: (i, j, 0))
    outs = [(512, BF16), (512, BF16), (512, BF16), (384, BF16), (512, BF16), (1024, BF16), (128, BF16),
            (128, F32), (128, F32), (64, F32), (512, F32), (512, F32), (8, F32), (8, F32)]
    return pl.pallas_call(
        _inproj_kernel,
        grid=grid,
        in_specs=[row(d), _resident((1, d)), _resident((d, N_CAT)), _resident((1, H_B))],
        out_specs=[row(c) for c, _ in outs],
        out_shape=[jax.ShapeDtypeStruct((b, t, c), dt) for c, dt in outs],
        compiler_params=_cparams("parallel", "parallel"),
        name="inproj",
    )(x, g, w_cat, b_f)


CS_BLK = 512


def _cumsum_kernel(x_ref, out_ref, carry_ref):
    @pl.when(pl.program_id(0) == 0)
    def _():
        carry_ref[...] = jnp.zeros_like(carry_ref)

    x = x_ref[...] * LOG2E
    hi = x.astype(BF16)
    r1 = x - hi.astype(F32)
    mid = r1.astype(BF16)
    lo = (r1 - mid.astype(F32)).astype(BF16)
    src = lax.broadcasted_iota(I32, (CS_BLK, CS_BLK), 0)
    dst = lax.broadcasted_iota(I32, (CS_BLK, CS_BLK), 1)
    tri = jnp.where(src <= dst, 1.0, 0.0).astype(BF16)
    cs = (jnp.dot(hi, tri, preferred_element_type=F32)
          + jnp.dot(mid, tri, preferred_element_type=F32)
          + jnp.dot(lo, tri, preferred_element_type=F32))
    out = cs + carry_ref[...]
    out_ref[...] = out
    carry_ref[...] = jnp.broadcast_to(out[:, CS_BLK - 1:CS_BLK], carry_ref.shape)


def _cumsum_lanes(x_t):
    b, h, l = x_t.shape
    spec = pl.BlockSpec((b * h, CS_BLK), lambda j: (0, j))
    return pl.pallas_call(
        _cumsum_kernel,
        grid=(l // CS_BLK,),
        in_specs=[spec],
        out_specs=spec,
        out_shape=jax.ShapeDtypeStruct((b * h, l), F32),
        scratch_shapes=[pltpu.VMEM((b * h, CS_BLK), F32)],
        compiler_params=_cparams("arbitrary"),
        name="cumsum",
    )(x_t.reshape(b * h, l)).reshape(b, h, l)


def _bias_kernel(rb_ref, bucket_ref, out_ref):
    b = bucket_ref[0]
    for h in range(H_A):
        acc = jnp.zeros(b.shape, F32)
        for k in range(NUM_BUCKETS):
            acc = jnp.where(b == k, rb_ref[k, h] * LOG2E, acc)
        out_ref[0, h] = acc


def _bias_tiles(rel_bias, bucket):
    n, r, c = bucket.shape
    return pl.pallas_call(
        _bias_kernel,
        grid=(n,),
        in_specs=[pl.BlockSpec(memory_space=pltpu.SMEM),
                  pl.BlockSpec((1, r, c), lambda i: (i, 0, 0))],
        out_specs=pl.BlockSpec((1, H_A, r, c), lambda i: (i, 0, 0, 0)),
        out_shape=jax.ShapeDtypeStruct((n, H_A, r, c), F32),
        compiler_params=_cparams("parallel"),
        name="t5_bias",
    )(rel_bias, bucket)


def _dsa_prompt_kernel(rb_ref, aq_ref, iq_ref, iw_ref, ik2_ref, akv_ref, bias_ref, out_ref,
                       qi_ref, qa_ref, iwrep_ref, sc_ref, sct_ref, selb_ref, lg_ref, mx_ref,
                       p_ref, acc_ref, thr_ref, *, k_top, far_bucket, seq_len):
    i = pl.program_id(1)
    nkb = i + 1
    lane = lax.broadcasted_iota(I32, (TQ, LANES), 1)
    low = lane < HEAD_DIM

    for p in range(4):
        s_i = iq_ref[0, :, LANES * p:LANES * (p + 1)].astype(F32)
        qi_ref[(2 * p) * TQ:(2 * p + 1) * TQ, :] = jnp.where(low, s_i, 0.0).astype(BF16)
        qi_ref[(2 * p + 1) * TQ:(2 * p + 2) * TQ, :] = jnp.where(low, 0.0, s_i).astype(BF16)
        s_a = aq_ref[0, :, LANES * p:LANES * (p + 1)].astype(F32)
        qa_ref[p * TQ:(p + 1) * TQ, :] = jnp.where(low, s_a, 0.0).astype(BF16)
        qa_ref[(p + 4) * TQ:(p + 5) * TQ, :] = jnp.where(low, 0.0, s_a).astype(BF16)
    iw = iw_ref[0]
    for h in range(H_IDX):
        iwrep_ref[h] = jnp.broadcast_to(iw[:, h:h + 1], (TQ, TK))

    def blk(j):
        return pl.ds(pl.multiple_of(j * TK, TK), TK)

    r2 = lax.broadcasted_iota(I32, (TQ, TK), 0)
    c2 = lax.broadcasted_iota(I32, (TQ, TK), 1)

    def score_blk(j, slot, diagonal=False):
        kb = ik2_ref[0, blk(j), :]
        acc = jnp.zeros((TQ, TK), F32)
        for h in range(H_IDX):
            d = lax.dot_general(qi_ref[h * TQ:(h + 1) * TQ, :], kb, NT_DIMS, preferred_element_type=F32)
            acc = acc + iwrep_ref[h] * jnp.maximum(d, 0.0)
        if diagonal:
            acc = jnp.where((c2 // CHUNK) <= (r2 // CHUNK), acc, -jnp.inf)
        sc_ref[:, blk(j)] = acc
        sct_ref[blk(j), :] = acc.T

    _for_blocks(i, score_blk)
    score_blk(i, 0, diagonal=True)

    sub = lax.broadcasted_iota(I32, (SUBLANES, TQ), 0)

    def count(pred):
        def body(j, accs):
            base = pl.multiple_of(j * TK, TK)
            accs = list(accs)
            sblk = sct_ref[pl.ds(base, TK), :]
            for g in range(TK // SUBLANES):
                hit = jnp.where(pred(sblk[SUBLANES * g:SUBLANES * (g + 1), :], base + SUBLANES * g + sub), 1.0, 0.0)
                accs[g % len(accs)] = accs[g % len(accs)] + hit
            return tuple(accs)
        zero = jnp.zeros((SUBLANES, TQ), F32)
        a = lax.fori_loop(0, nkb, body, (zero, zero, zero, zero))
        tot = jnp.sum((a[0] + a[1]) + (a[2] + a[3]), axis=0, keepdims=True)
        return jnp.broadcast_to(tot, (SUBLANES, TQ))

    kf = float(k_top)
    c0 = count(lambda s, idx: s >= 0.0)
    ans_key = jnp.where(c0 >= kf, 0, INT_MIN).astype(I32)

    def bit_body(t, ans_key):
        cand_key = ans_key | lax.shift_left(jnp.int32(1), 30 - t)
        cand = _key_to_float(cand_key)
        c = count(lambda s, idx: s >= cand)
        return jnp.where(c >= kf, cand_key, ans_key)

    ans_key = lax.fori_loop(0, 31, bit_body, ans_key)
    ans = jnp.where(ans_key > KEY_NEG_INF, _key_to_float(ans_key), -jnp.inf)

    cgt = count(lambda s, idx: s > ans)
    ceq = count(lambda s, idx: s == ans)
    need = kf - cgt
    tie = (ceq > need) & (ans > -jnp.inf)
    flag = jnp.max(jnp.where(tie[0:1, :], 1.0, 0.0), axis=1, keepdims=True)
    thr_ref[...] = jnp.full((SUBLANES, TQ), seq_len, I32)

    @pl.when(flag[0, 0] > 0.0)
    def _():
        nbits = int(seq_len - 1).bit_length()

        def tie_body(t, m):
            cand = m | lax.shift_left(jnp.int32(1), nbits - 1 - t)
            c = count(lambda s, idx: (s == ans) & (idx < cand))
            return jnp.where(c < need, cand, m)

        m = lax.fori_loop(0, nbits, tie_body, jnp.zeros((SUBLANES, TQ), I32))
        thr_ref[...] = jnp.where(tie, m, seq_len)

    def to_rows(x):
        return jnp.broadcast_to(x[0:1, :], (LANES, TQ)).T

    ans_r = to_rows(ans)
    thr_r = to_rows(thr_ref[...])

    def selb_blk(j, c):
        sb = sc_ref[:, blk(j)]
        col = j * TK + lane
        for half in range(2):
            sh = sb[:, half * LANES:(half + 1) * LANES]
            ch = col + half * LANES
            sel = ((sh > ans_r) | ((sh == ans_r) & (ch <= thr_r))) & (sh > -jnp.inf)
            selb_ref[:, pl.ds(pl.multiple_of(j * TK + half * LANES, LANES), LANES)] = jnp.where(sel, 0.0, NEG)
        return c

    lax.fori_loop(0, nkb, selb_blk, 0)

    for h in range(H_A):
        mx_ref[h] = jnp.full((TQ, LANES), NEG, F32)
    acc_ref[...] = jnp.zeros_like(acc_ref)

    def pass_a(j, bias_of_head):
        kb = akv_ref[0, blk(j), 0:LANES]
        sb = selb_ref[:, blk(j)]
        for h in range(H_A):
            s = lax.dot_general(qa_ref[h * TQ:(h + 1) * TQ, :], kb, NT_DIMS, preferred_element_type=F32)
            lg = s + sb + bias_of_head(h)
            lg_ref[h, :, blk(j)] = lg
            mx_ref[h] = jnp.maximum(mx_ref[h], jnp.maximum(lg[:, :LANES], lg[:, LANES:]))

    _for_blocks(jnp.maximum(i - 1, 0), lambda j, slot: pass_a(j, lambda h: rb_ref[far_bucket, h] * LOG2E))

    @pl.when(i >= 1)
    def _():
        pass_a(i - 1, lambda h: bias_ref[0, h])

    pass_a(i, lambda h: bias_ref[1, h])

    for h in range(H_A):
        mx_ref[h] = jnp.broadcast_to(jnp.max(mx_ref[h], axis=1, keepdims=True), (TQ, LANES))

    half = (H_A // 2) * TQ

    def pass_b(j, slot):
        for h in range(H_A):
            lg = lg_ref[h, :, blk(j)]
            m = mx_ref[h]
            p_ref[slot, h * TQ:(h + 1) * TQ, 0:LANES] = jnp.exp2(lg[:, :LANES] - m).astype(BF16)
            p_ref[slot, h * TQ:(h + 1) * TQ, LANES:TK] = jnp.exp2(lg[:, LANES:] - m).astype(BF16)
        for g in range(HKV_A):
            acc_ref[g * half:(g + 1) * half, :] += jnp.dot(
                p_ref[slot, g * half:(g + 1) * half, :], akv_ref[0, blk(j), (g + 1) * LANES:(g + 2) * LANES],
                preferred_element_type=F32)

    _for_blocks(nkb, pass_b)

    for p in range(4):
        out_ref[0, :, LANES * p:LANES * (p + 1)] = _normalize_pair(
            acc_ref[p * TQ:(p + 1) * TQ, :], acc_ref[(p + 4) * TQ:(p + 5) * TQ, :], low).astype(BF16)


def _dsa_prompt(rel_bias, aq, iq, iw, ik2, akv, bias_tiles, far_bucket):
    b, t, _ = aq.shape
    k_top = min(K_TOP_MAX, t // 4)
    qrow = lambda c: pl.BlockSpec((1, TQ, c), lambda i, j: (i, j, 0))
    full = lambda c: pl.BlockSpec((1, t, c), lambda i, j: (i, 0, 0))
    kern = functools.partial(_dsa_prompt_kernel, k_top=k_top, far_bucket=far_bucket, seq_len=t)
    return pl.pallas_call(
        kern,
        grid=(b, t // TQ),
        in_specs=[pl.BlockSpec(memory_space=pltpu.SMEM),
                  qrow(512), qrow(512), qrow(8), full(128), full(384),
                  _resident((2, H_A, TQ, TK))],
        out_specs=qrow(512),
        out_shape=jax.ShapeDtypeStruct((b, t, 512), BF16),
        scratch_shapes=[
            pltpu.VMEM((H_IDX * TQ, LANES), BF16),
            pltpu.VMEM((H_A * TQ, LANES), BF16),
            pltpu.VMEM((H_IDX, TQ, TK), F32),
            pltpu.VMEM((TQ, t), F32),
            pltpu.VMEM((t, TQ), F32),
            pltpu.VMEM((TQ, t), F32),
            pltpu.VMEM((H_A, TQ, t), F32),
            pltpu.VMEM((H_A, TQ, LANES), F32),
            pltpu.VMEM((2, H_A * TQ, TK), BF16),
            pltpu.VMEM((H_A * TQ, LANES), F32),
            pltpu.VMEM((SUBLANES, TQ), I32),
        ],
        compiler_params=_cparams("parallel", "parallel"),
        name="dsa_prompt",
    )(rel_bias, aq, iq, iw, ik2, akv, bias_tiles)


def _fox_prompt_kernel(bq_ref, bk_ref, bv_ref, fq_ref, fkt_ref, out_ref,
                       qb_ref, fqrep_ref, lg_ref, mx_ref, p_ref, acc_ref):
    i = pl.program_id(1)
    lane = lax.broadcasted_iota(I32, (TQ, LANES), 1)
    low = lane < HEAD_DIM

    for p in range(4):
        s_b = bq_ref[0, :, LANES * p:LANES * (p + 1)].astype(F32)
        qb_ref[(2 * p) * TQ:(2 * p + 1) * TQ, :] = jnp.where(low, s_b, 0.0).astype(BF16)
        qb_ref[(2 * p + 1) * TQ:(2 * p + 2) * TQ, :] = jnp.where(low, 0.0, s_b).astype(BF16)
    fq = fq_ref[0]
    for h in range(H_B):
        fqrep_ref[h] = jnp.broadcast_to(fq[:, h:h + 1], (TQ, LANES))
        mx_ref[h] = jnp.full((TQ, LANES), NEG, F32)
    acc_ref[...] = jnp.zeros_like(acc_ref)

    def blk(j):
        return pl.ds(pl.multiple_of(j * TK, TK), TK)

    r2 = lax.broadcasted_iota(I32, (TQ, LANES), 0)

    def pass_a(j, diagonal):
        for p in range(4):
            s = lax.dot_general(qb_ref[(2 * p) * TQ:(2 * p + 2) * TQ, :],
                                bk_ref[0, blk(j), LANES * p:LANES * (p + 1)],
                                NT_DIMS, preferred_element_type=F32)
            for e in range(2):
                h = 2 * p + e
                fk = fkt_ref[0, h:h + 1, blk(j)]
                halves = []
                for half in range(2):
                    lg = (s[e * TQ:(e + 1) * TQ, half * LANES:(half + 1) * LANES]
                          + (fqrep_ref[h] - fk[:, half * LANES:(half + 1) * LANES]))
                    if diagonal:
                        lg = jnp.where(lane + half * LANES <= r2, lg, NEG)
                    lg_ref[h, :, pl.ds(pl.multiple_of(j * TK + half * LANES, LANES), LANES)] = lg
                    halves.append(lg)
                mx_ref[h] = jnp.maximum(mx_ref[h], jnp.maximum(halves[0], halves[1]))

    _for_blocks(i, lambda j, slot: pass_a(j, False))
    pass_a(i, True)

    for h in range(H_B):
        mx_ref[h] = jnp.broadcast_to(jnp.max(mx_ref[h], axis=1, keepdims=True), (TQ, LANES))

    def pass_b(j, slot):
        for h in range(H_B):
            lg = lg_ref[h, :, blk(j)]
            m = mx_ref[h]
            p_ref[slot, h * TQ:(h + 1) * TQ, 0:LANES] = jnp.exp2(lg[:, :LANES] - m).astype(BF16)
            p_ref[slot, h * TQ:(h + 1) * TQ, LANES:TK] = jnp.exp2(lg[:, LANES:] - m).astype(BF16)
            acc_ref[h * TQ:(h + 1) * TQ, :] += jnp.dot(
                p_ref[slot, h * TQ:(h + 1) * TQ, :], bv_ref[0, blk(j), LANES * h:LANES * (h + 1)],
                preferred_element_type=F32)

    _for_blocks(i + 1, pass_b)

    for p in range(4):
        out_ref[0, :, LANES * p:LANES * (p + 1)] = _normalize_pair(
            acc_ref[(2 * p) * TQ:(2 * p + 1) * TQ, :], acc_ref[(2 * p + 1) * TQ:(2 * p + 2) * TQ, :], low).astype(BF16)


def _fox_prompt(bq, bk, bv, fq, fkt):
    b, t, _ = bq.shape
    qrow = lambda c: pl.BlockSpec((1, TQ, c), lambda i, j: (i, j, 0))
    full = lambda c: pl.BlockSpec((1, t, c), lambda i, j: (i, 0, 0))
    return pl.pallas_call(
        _fox_prompt_kernel,
        grid=(b, t // TQ),
        in_specs=[qrow(512), full(512), full(1024), qrow(8),
                  pl.BlockSpec((1, H_B, t), lambda i, j: (i, 0, 0))],
        out_specs=qrow(512),
        out_shape=jax.ShapeDtypeStruct((b, t, 512), BF16),
        scratch_shapes=[
            pltpu.VMEM((H_B * TQ, LANES), BF16),
            pltpu.VMEM((H_B, TQ, LANES), F32),
            pltpu.VMEM((H_B, TQ, t), F32),
            pltpu.VMEM((H_B, TQ, LANES), F32),
            pltpu.VMEM((2, H_B * TQ, TK), BF16),
            pltpu.VMEM((H_B * TQ, LANES), F32),
        ],
        compiler_params=_cparams("parallel", "parallel"),
        name="fox_prompt",
    )(bq, bk, bv, fq, fkt)


FF_BLK = 1024


def _tail_kernel(x_ref, oa_ref, ob_ref, woa_ref, wob_ref, gm_ref, wup_ref, wdn_ref, gf_ref, y_ref):
    mixed = (jnp.dot(oa_ref[...], woa_ref[...], preferred_element_type=F32)
             + jnp.dot(ob_ref[...], wob_ref[...], preferred_element_type=F32))
    x1 = x_ref[...] + mixed
    h2 = _rms(x1, gm_ref[...]).astype(BF16)
    acc = x1
    for c in range(D_FF // FF_BLK):
        u = jnp.dot(h2, wup_ref[:, c * FF_BLK:(c + 1) * FF_BLK], preferred_element_type=F32)
        u = jnp.square(jnp.maximum(u, 0.0)).astype(BF16)
        acc = acc + jnp.dot(u, wdn_ref[c * FF_BLK:(c + 1) * FF_BLK, :], preferred_element_type=F32)
    y_ref[...] = _rms(acc, gf_ref[...])


def _tail(x, oa, ob, woa, wob, g_mlp, w_up, w_down, g_final, tm):
    n, d = x.shape
    row = lambda c: pl.BlockSpec((tm, c), lambda i: (i, 0))
    return pl.pallas_call(
        _tail_kernel,
        grid=(n // tm,),
        in_specs=[row(d), row(512), row(512), _resident((512, d)), _resident((512, d)), _resident((1, d)),
                  _resident((d, D_FF)), _resident((D_FF, d)), _resident((1, d))],
        out_specs=row(d),
        out_shape=jax.ShapeDtypeStruct((n, d), F32),
        compiler_params=_cparams("parallel"),
        name="tail",
    )(x, oa, ob, woa, wob, g_mlp, w_up, w_down, g_final)


def _dsa_sample_kernel(aq_ref, iq_ref, iw_ref, ik2_ref, akv_ref, cikt_ref, cakt_ref, cavt_ref, bias_ref, out_ref,
                       *, past, n_new, l_pad, k_top):
    l_all = past + n_new
    n_tail = l_pad - past
    lane = lax.broadcasted_iota(I32, (n_new, LANES), 1)
    low = lane < HEAD_DIM

    def pad_rows(x):
        return jnp.concatenate([x, jnp.zeros((n_tail - n_new, x.shape[1]), F32)], axis=0).astype(BF16)

    akv = akv_ref[0].astype(F32)
    ik_new = pad_rows(ik2_ref[0].astype(F32)[:, 0:D_IDX])
    ak_new = pad_rows(akv[:, 0:LANES])
    av_new = pad_rows(jnp.where(low, akv[:, LANES:2 * LANES], akv[:, 2 * LANES:3 * LANES]))

    iq32 = iq_ref[0].astype(F32)
    qi = jnp.concatenate([iq32[:, D_IDX * h:D_IDX * (h + 1)] for h in range(H_IDX)], axis=0).astype(BF16)
    d = jnp.concatenate(
        [jnp.dot(qi, cikt_ref[0].astype(BF16), preferred_element_type=F32),
         lax.dot_general(qi, ik_new, NT_DIMS, preferred_element_type=F32)], axis=1)
    iw = iw_ref[0]
    score = jnp.zeros((n_new, l_pad), F32)
    for h in range(H_IDX):
        score = score + iw[:, h:h + 1] * jnp.maximum(d[h * n_new:(h + 1) * n_new, :], 0.0)
    row = lax.broadcasted_iota(I32, (n_new, l_pad), 0)
    col = lax.broadcasted_iota(I32, (n_new, l_pad), 1)
    adm = (col < l_all) & ((col // CHUNK) <= ((past + row) // CHUNK))
    sc = jnp.where(adm, score, -jnp.inf)

    def count(mask):
        return jnp.sum(jnp.where(mask, 1.0, 0.0), axis=1, keepdims=True)

    kf = float(k_top)
    ans_key = jnp.where(count(sc >= 0.0) >= kf, 0, INT_MIN).astype(I32)

    def bit_body(t, ans_key):
        cand_key = ans_key | lax.shift_left(jnp.int32(1), 30 - t)
        return jnp.where(count(sc >= _key_to_float(cand_key)) >= kf, cand_key, ans_key)

    ans_key = lax.fori_loop(0, 31, bit_body, ans_key)
    ans = jnp.where(ans_key > KEY_NEG_INF, _key_to_float(ans_key), -jnp.inf)
    eq = sc == ans
    need = kf - count(sc > ans)
    tie = (count(eq) > need) & (ans > -jnp.inf)
    flag = jnp.max(jnp.where(tie, 1.0, 0.0), axis=0, keepdims=True)
    nbits = int(l_pad - 1).bit_length()

    def tie_search(_):
        def tie_body(t, m):
            cand = m | lax.shift_left(jnp.int32(1), nbits - 1 - t)
            return jnp.where(count(eq & (col < cand)) < need, cand, m)
        m = lax.fori_loop(0, nbits, tie_body, jnp.zeros((n_new, 1), I32))
        return jnp.where(tie, m, l_pad)

    thr = lax.cond(flag[0, 0] > 0.0, tie_search, lambda _: jnp.full((n_new, 1), l_pad, I32), 0)
    sel = ((sc > ans) | (eq & (col <= thr))) & (sc > -jnp.inf)
    selb = jnp.where(sel, 0.0, NEG)

    aq32 = aq_ref[0].astype(F32)
    slabs = [aq32[:, LANES * p:LANES * (p + 1)] for p in range(4)]
    qa = jnp.concatenate([jnp.where(low, s, 0.0) for s in slabs] + [jnp.where(low, 0.0, s) for s in slabs],
                         axis=0).astype(BF16)
    s = jnp.concatenate(
        [jnp.dot(qa, cakt_ref[0].astype(BF16), preferred_element_type=F32),
         lax.dot_general(qa, ak_new, NT_DIMS, preferred_element_type=F32)], axis=1)
    lg = s + bias_ref[...] + jnp.concatenate([selb] * H_A, axis=0)
    m = jnp.max(lg, axis=1, keepdims=True)
    p = jnp.exp2(lg - m)
    l = jnp.sum(p, axis=1, keepdims=True)
    pb = p.astype(BF16)
    o = (lax.dot_general(pb[:, 0:past], cavt_ref[0].astype(BF16), NT_DIMS, preferred_element_type=F32)
         + jnp.dot(pb[:, past:l_pad], av_new, preferred_element_type=F32)) / l
    for q in range(4):
        out_ref[0, :, LANES * q:LANES * (q + 1)] = jnp.where(
            low, o[q * n_new:(q + 1) * n_new, :], o[(q + 4) * n_new:(q + 5) * n_new, :]).astype(BF16)


def _dsa_sample(aq, iq, iw, ik2, akv, cikt, cakt, cavt, bias_rows, past):
    b, n_new, _ = aq.shape
    l_pad = bias_rows.shape[1]
    assert past % LANES == 0 and l_pad == past + LANES and n_new <= LANES
    k_top = min(K_TOP_MAX, (past + n_new) // 4)
    new = lambda c: pl.BlockSpec((1, n_new, c), lambda i: (i, 0, 0))
    old = lambda c: pl.BlockSpec((1, c, past), lambda i: (i, 0, 0))
    kern = functools.partial(_dsa_sample_kernel, past=past, n_new=n_new, l_pad=l_pad, k_top=k_top)
    return pl.pallas_call(
        kern,
        grid=(b,),
        in_specs=[new(512), new(512), new(8), new(128), new(384), old(D_IDX), old(LANES), old(LANES),
                  _resident((H_A * n_new, l_pad))],
        out_specs=new(512),
        out_shape=jax.ShapeDtypeStruct((b, n_new, 512), BF16),
        compiler_params=_cparams("parallel"),
        name="dsa_sample",
    )(aq, iq, iw, ik2, akv, cikt, cakt, cavt, bias_rows)


def _fox_sample_kernel(bq_ref, bkn_ref, bvn_ref, ckt_ref, cvt_ref, fkt_ref, tot_ref, lf_ref, lft_ref, out_ref,
                       *, past, n_new):
    lf = lf_ref[0] * LOG2E
    lft = lft_ref[0] * LOG2E
    rown = lax.broadcasted_iota(I32, (n_new, H_B), 0)
    lanen = lax.broadcasted_iota(I32, (H_B, n_new), 1)
    fq = jnp.broadcast_to(tot_ref[0], (n_new, H_B))
    fqt = jnp.broadcast_to(fkt_ref[0][:, past - 1:past], (H_B, n_new))
    for s in range(n_new):
        fq = fq + jnp.where(rown >= s, lf[s:s + 1, :], 0.0)
        fqt = fqt + jnp.where(lanen >= s, lft[:, s:s + 1], 0.0)

    kct = ckt_ref[0].astype(BF16)
    vct = cvt_ref[0].astype(BF16)
    bq32 = bq_ref[0].astype(F32)
    head_of_lane = lax.broadcasted_iota(I32, (n_new, H_B * HEAD_DIM), 1) // HEAD_DIM
    qb = jnp.concatenate([jnp.where(head_of_lane == h, bq32, 0.0) for h in range(H_B)], axis=0).astype(BF16)
    s_past = jnp.dot(qb, kct, preferred_element_type=F32)
    s_new = lax.dot_general(qb, bkn_ref[0], NT_DIMS, preferred_element_type=F32)
    fkt = fkt_ref[0]
    fq_col = jnp.concatenate([fq[:, h:h + 1] for h in range(H_B)], axis=0)
    fk_past = jnp.concatenate([jnp.broadcast_to(fkt[h:h + 1, :], (n_new, past)) for h in range(H_B)], axis=0)
    fk_new = jnp.concatenate([jnp.broadcast_to(fqt[h:h + 1, :], (n_new, n_new)) for h in range(H_B)], axis=0)
    lg_past = s_past + (fq_col - fk_past)
    trow = lax.broadcasted_iota(I32, (H_B * n_new, n_new), 0) % n_new
    tcol = lax.broadcasted_iota(I32, (H_B * n_new, n_new), 1)
    lg_new = jnp.where(tcol <= trow, s_new + (fq_col - fk_new), NEG)
    m = jnp.maximum(jnp.max(lg_past, axis=1, keepdims=True), jnp.max(lg_new, axis=1, keepdims=True))
    p_past = jnp.exp2(lg_past - m)
    p_new = jnp.exp2(lg_new - m)
    l = jnp.sum(p_past, axis=1, keepdims=True) + jnp.sum(p_new, axis=1, keepdims=True)
    o = (lax.dot_general(p_past.astype(BF16), vct, NT_DIMS, preferred_element_type=F32)
         + jnp.dot(p_new.astype(BF16), bvn_ref[0].astype(BF16), preferred_element_type=F32)) / l
    out = jnp.zeros((n_new, H_B * HEAD_DIM), F32)
    for h in range(H_B):
        out = out + jnp.where(head_of_lane == h, o[h * n_new:(h + 1) * n_new, :], 0.0)
    out_ref[0] = out.astype(BF16)


def _fox_sample(bq, bkn, bvn, ckt, cvt, fkt, tot, lf, lft, past):
    b, n_new, _ = bq.shape
    new = lambda c: pl.BlockSpec((1, n_new, c), lambda i: (i, 0, 0))
    old = pl.BlockSpec((1, H_B * HEAD_DIM, past), lambda i: (i, 0, 0))
    kern = functools.partial(_fox_sample_kernel, past=past, n_new=n_new)
    return pl.pallas_call(
        kern,
        grid=(b,),
        in_specs=[new(512), new(512), new(512), old, old,
                  pl.BlockSpec((1, H_B, past), lambda i: (i, 0, 0)),
                  pl.BlockSpec((1, 1, H_B), lambda i: (i, 0, 0)),
                  new(H_B),
                  pl.BlockSpec((1, H_B, n_new), lambda i: (i, 0, 0))],
        out_specs=new(512),
        out_shape=jax.ShapeDtypeStruct((b, n_new, 512), BF16),
        compiler_params=_cparams("parallel"),
        name="fox_sample",
    )(bq, bkn, bvn, ckt, cvt, fkt, tot, lf, lft)


def _prep_weights(w_in, w_o, w_up, w_down):
    offs = np.concatenate([[0], np.cumsum(IN_SPLITS)])
    seg = lambda k: w_in[:, int(offs[k]):int(offs[k + 1])]
    a_q, a_k, a_v, i_q, i_k, i_w, b_q, b_k, b_v, f_z = (seg(k) for k in range(10))
    a_q_pairs = jnp.concatenate([a_q[:, HEAD_DIM * h:HEAD_DIM * (h + 1)] for h in AQ_HEAD_ORDER], axis=1)
    pad = jnp.zeros((w_in.shape[0], N_CAT - C_SM - 16), w_in.dtype)
    w_cat = jnp.concatenate([a_q_pairs, i_q, b_q, a_k, a_v, b_k, b_v, i_k, i_k, i_w, f_z, pad], axis=1)
    wo_a = jnp.concatenate([w_o[HEAD_DIM * h:HEAD_DIM * (h + 1)] for h in AQ_HEAD_ORDER], axis=0)
    wo_b = w_o[H_A * HEAD_DIM:]
    return (w_cat.astype(BF16), wo_a.astype(BF16), wo_b.astype(BF16), w_up.astype(BF16), w_down.astype(BF16))


def kernel(x_prompt, x_sample, cache_a_k, cache_a_v, cache_idx_k, cache_b_k, cache_b_v, cache_b_logf,
           w_in, w_o, b_f, rel_bias, g_attn, w_up, w_down, g_mlp, g_final):
    assert w_in.shape[0] == 1, "single-layer trunk"
    bp, tp, d = x_prompt.shape
    bs, ts, _ = x_sample.shape
    past = cache_a_k.shape[2]
    w_cat, wo_a, wo_b, wup, wdn = _prep_weights(w_in[0], w_o[0], w_up[0], w_down[0])
    g_a = g_attn[0].reshape(1, d)
    g_m = g_mlp[0].reshape(1, d)
    g_f = g_final.reshape(1, d)
    bf = b_f[0].reshape(1, H_B)
    rel_bias = rel_bias.astype(F32)

    (aq, iq, bq, akv, bk, bv, ik2, ak32, av32, ik32, bk32, bv32, logf, iw) = _inproj(x_prompt, g_a, w_cat, bf, 512)
    fkt = _cumsum_lanes(jnp.swapaxes(logf, 1, 2))
    fq = jnp.swapaxes(fkt, 1, 2)
    r = np.arange(TQ)[:, None]
    c = np.arange(TK)[None, :]
    bucket = _t5_bucket_np(np.stack([c - TK - r, c - r]))
    far_bucket = int(_t5_bucket_np(np.array(-TK - 1)))
    assert far_bucket == int(_t5_bucket_np(np.array(-tp)))
    bias_tiles = _bias_tiles(rel_bias, jnp.asarray(bucket))
    out_a = _dsa_prompt(rel_bias, aq, iq, iw, ik2, akv, bias_tiles, far_bucket)
    out_b = _fox_prompt(bq, bk, bv, fq, fkt)
    y_p = _tail(x_prompt.reshape(bp * tp, d), out_a.reshape(bp * tp, 512), out_b.reshape(bp * tp, 512),
                wo_a, wo_b, g_m, wup, wdn, g_f, 512).reshape(bp, tp, d)

    n_s = bs * ts
    outs = _inproj(x_sample.reshape(1, n_s, d), g_a, w_cat, bf, n_s)
    (aq_s, iq_s, bq_s, akv_s, bk_s, bv_s, ik2_s, ak32_s, av32_s, ik32_s, bk32_s, bv32_s, logf_s, iw_s) = (
        o.reshape(bs, ts, o.shape[-1]) for o in outs)
    l_all = past + ts
    l_pad = -(-l_all // LANES) * LANES
    rel_s = np.arange(l_pad)[None, :] - (past + np.arange(ts))[:, None]
    bias_s = _bias_tiles(rel_bias, jnp.asarray(_t5_bucket_np(rel_s))[None])[0].reshape(H_A * ts, l_pad)
    def feature_major(cache):
        c = cache[0]
        c = jnp.transpose(c, (0, 2, 3, 1)) if c.ndim == 4 else jnp.transpose(c, (0, 2, 1))
        return c.reshape(bs, -1, past)

    out_a_s = _dsa_sample(aq_s, iq_s, iw_s, ik2_s, akv_s, feature_major(cache_idx_k),
                          feature_major(cache_a_k), feature_major(cache_a_v), bias_s, past)
    fkt_c = _cumsum_lanes(jnp.swapaxes(cache_b_logf[0].astype(F32), 1, 2))
    tot = fkt_c[:, :, past - 1].reshape(bs, 1, H_B)
    out_b_s = _fox_sample(bq_s, bk_s, bv32_s, feature_major(cache_b_k), feature_major(cache_b_v), fkt_c, tot,
                          logf_s, jnp.swapaxes(logf_s, 1, 2), past)
    y_s = _tail(x_sample.reshape(n_s, d), out_a_s.reshape(n_s, 512), out_b_s.reshape(n_s, 512),
                wo_a, wo_b, g_m, wup, wdn, g_f, n_s).reshape(bs, ts, d)

    def rows(a, heads, b, t):
        return a.reshape(1, b, t, heads, HEAD_DIM)

    return (y_p, y_s,
            rows(ak32, HKV_A, bp, tp), rows(av32, HKV_A, bp, tp), ik32.reshape(1, bp, tp, D_IDX),
            rows(bk32, H_B, bp, tp), rows(bv32, H_B, bp, tp), logf.reshape(1, bp, tp, H_B),
            rows(ak32_s, HKV_A, bs, ts), rows(av32_s, HKV_A, bs, ts), ik32_s.reshape(1, bs, ts, D_IDX),
            rows(bk32_s, H_B, bs, ts), rows(bv32_s, H_B, bs, ts), logf_s.reshape(1, bs, ts, H_B))
```

```python
import functools
import math

import numpy as np
import jax
import jax.numpy as jnp
from jax import lax
from jax.experimental import pallas as pl
from jax.experimental.pallas import tpu as pltpu

F32 = jnp.float32
BF16 = jnp.bfloat16
I32 = jnp.int32

D_MODEL = 1024
CHUNK = 64
HEAD_DIM = 64
H_A = 8
HKV_A = 2
H_B = 8
H_IDX = 8
D_IDX = 64
K_TOP_MAX = 256
NUM_BUCKETS = 32
MAX_DISTANCE = 128
D_FF = 4 * D_MODEL
EPS = 1e-6
IN_SPLITS = (H_A * HEAD_DIM, HKV_A * HEAD_DIM, HKV_A * HEAD_DIM, H_IDX * D_IDX, D_IDX, H_IDX,
             H_B * HEAD_DIM, H_B * HEAD_DIM, H_B * HEAD_DIM, H_B)

LANES = 128
SUBLANES = 8
VMEM_LIMIT = 56 * 1024 * 1024

TQ = 256
TK = 256
NEG = -1e30
LOG2E = 1.4426950408889634
KEY_NEG_INF = -2139095041
INT_MIN = -2147483648

C_AQ, C_IQ, C_BQ, C_AKV, C_BK, C_BV, C_IK2, C_SM, N_CAT = 0, 512, 1024, 1536, 1792, 2304, 2816, 2944, 3072
AQ_HEAD_ORDER = (0, 4, 1, 5, 2, 6, 3, 7)

NT_DIMS = (((1,), (1,)), ((), ()))


def _cparams(*sem):
    return pltpu.CompilerParams(dimension_semantics=sem, vmem_limit_bytes=VMEM_LIMIT)


def _resident(shape):
    nd = len(shape)
    return pl.BlockSpec(shape, lambda *_: (0,) * nd, pipeline_mode=pl.Buffered(1))


def _rms(x, g):
    ms = jnp.mean(x * x, axis=-1, keepdims=True)
    return (x * lax.rsqrt(ms + EPS)) * g


def _for_blocks(n, body):
    def pair(k, c):
        body(2 * k, 0)
        body(2 * k + 1, 1)
        return c

    lax.fori_loop(0, n // 2, pair, 0)

    @pl.when(n % 2 == 1)
    def _():
        body(n - 1, 0)


def _normalize_pair(a_lo, a_hi, low):
    num = jnp.where(low, a_lo, a_hi)
    den = pltpu.roll(jnp.where(low, a_hi, a_lo), HEAD_DIM, axis=1)
    return num / den


def _key_to_float(k):
    return lax.bitcast_convert_type(k ^ (lax.shift_right_arithmetic(k, 31) & 0x7FFFFFFF), F32)


def _t5_bucket_np(rel):
    half = NUM_BUCKETS // 2
    max_exact = half // 2
    ret = np.where(rel > 0, half, 0)
    n = np.abs(rel)
    n_f = np.maximum(n, max_exact).astype(np.float64)
    large = max_exact + (np.log(n_f / max_exact) / math.log(MAX_DISTANCE / max_exact)
                         * (half - max_exact)).astype(np.int32)
    large = np.minimum(large, half - 1)
    return (ret + np.where(n < max_exact, n, large)).astype(np.int32)


def _inproj_kernel(x_ref, g_ref, w_ref, bf_ref,
                   aq_ref, iq_ref, bq_ref, akv_ref, bk_ref, bv_ref, ik2_ref,
                   ak32_ref, av32_ref, bk32_ref, bv32_ref, iw_ref, ikt_ref, logft_ref):
    h = _rms(x_ref[0], g_ref[...])
    p = jnp.dot(h.astype(BF16), w_ref[...], preferred_element_type=F32)
    qscale = HEAD_DIM ** -0.5 * LOG2E
    aq_ref[0] = (p[:, C_AQ:C_AQ + 512] * qscale).astype(BF16)
    iq_ref[0] = (p[:, C_IQ:C_IQ + 512] * (D_IDX ** -0.5)).astype(BF16)
    bq_ref[0] = (p[:, C_BQ:C_BQ + 512] * qscale).astype(BF16)
    low = lax.broadcasted_iota(I32, (p.shape[0], LANES), 1) < HEAD_DIM
    av = p[:, C_AKV + LANES:C_AKV + 2 * LANES]
    akv_ref[0, :, 0:LANES] = p[:, C_AKV:C_AKV + LANES].astype(BF16)
    akv_ref[0, :, LANES:2 * LANES] = jnp.where(low, av, 1.0).astype(BF16)
    akv_ref[0, :, 2 * LANES:3 * LANES] = jnp.where(low, 1.0, av).astype(BF16)
    bk_ref[0] = p[:, C_BK:C_BK + 512].astype(BF16)
    for q in range(4):
        bv = p[:, C_BV + LANES * q:C_BV + LANES * (q + 1)]
        bv_ref[0, :, 2 * q * LANES:(2 * q + 1) * LANES] = jnp.where(low, bv, 1.0).astype(BF16)
        bv_ref[0, :, (2 * q + 1) * LANES:(2 * q + 2) * LANES] = jnp.where(low, 1.0, bv).astype(BF16)
    ik2_ref[0] = p[:, C_IK2:C_IK2 + 128].astype(BF16)
    ak32_ref[0] = p[:, C_AKV:C_AKV + 128]
    av32_ref[0] = p[:, C_AKV + 128:C_AKV + 256]
    bk32_ref[0] = p[:, C_BK:C_BK + 512]
    bv32_ref[0] = p[:, C_BV:C_BV + 512]
    iw_ref[0] = p[:, C_SM:C_SM + 8] * (H_IDX ** -0.5)
    ikt_ref[0] = p[:, C_IK2:C_IK2 + LANES].T[0:D_IDX, :]
    z = p[:, C_SM:C_SM + LANES].T[8:16, :] + bf_ref[...]
    logft_ref[0] = jnp.minimum(z, 0.0) - jnp.log1p(jnp.exp(-jnp.abs(z)))


def _inproj(x, g, w_cat, b_f, tm):
    b, t, d = x.shape
    grid = (b, t // tm)
    row = lambda c: pl.BlockSpec((1, tm, c), lambda i, j: (i, j, 0))
    col = lambda c: pl.BlockSpec((1, c, tm), lambda i, j: (i, 0, j))
    outs = [(512, BF16), (512, BF16), (512, BF16), (384, BF16), (512, BF16), (1024, BF16), (128, BF16),
            (128, F32), (128, F32), (512, F32), (512, F32), (8, F32)]
    outs_t = [D_IDX, H_B]
    return pl.pallas_call(
        _inproj_kernel,
        grid=grid,
        in_specs=[row(d), _resident((1, d)), _resident((d, N_CAT)), _resident((H_B, 1))],
        out_specs=[row(c) for c, _ in outs] + [col(c) for c in outs_t],
        out_shape=([jax.ShapeDtypeStruct((b, t, c), dt) for c, dt in outs]
                   + [jax.ShapeDtypeStruct((b, c, t), F32) for c in outs_t]),
        compiler_params=_cparams("parallel", "parallel"),
        name="inproj",
    )(x, g, w_cat, b_f)


CS_BLK = 512


def _cumsum_kernel(x_ref, out_ref, carry_ref):
    @pl.when(pl.program_id(0) == 0)
    def _():
        carry_ref[...] = jnp.zeros_like(carry_ref)

    x = x_ref[...] * LOG2E
    hi = x.astype(BF16)
    r1 = x - hi.astype(F32)
    mid = r1.astype(BF16)
    lo = (r1 - mid.astype(F32)).astype(BF16)
    src = lax.broadcasted_iota(I32, (CS_BLK, CS_BLK), 0)
    dst = lax.broadcasted_iota(I32, (CS_BLK, CS_BLK), 1)
    tri = jnp.where(src <= dst, 1.0, 0.0).astype(BF16)
    cs = (jnp.dot(hi, tri, preferred_element_type=F32)
          + jnp.dot(mid, tri, preferred_element_type=F32)
          + jnp.dot(lo, tri, preferred_element_type=F32))
    out = cs + carry_ref[...]
    out_ref[...] = out
    carry_ref[...] = jnp.broadcast_to(out[:, CS_BLK - 1:CS_BLK], carry_ref.shape)


def _cumsum_lanes(x_t):
    b, h, l = x_t.shape
    spec = pl.BlockSpec((b * h, CS_BLK), lambda j: (0, j))
    return pl.pallas_call(
        _cumsum_kernel,
        grid=(l // CS_BLK,),
        in_specs=[spec],
        out_specs=spec,
        out_shape=jax.ShapeDtypeStruct((b * h, l), F32),
        scratch_shapes=[pltpu.VMEM((b * h, CS_BLK), F32)],
        compiler_params=_cparams("arbitrary"),
        name="cumsum",
    )(x_t.reshape(b * h, l)).reshape(b, h, l)


def _bias_kernel(rb_ref, bucket_ref, out_ref):
    b = bucket_ref[0]
    for h in range(H_A):
        acc = jnp.zeros(b.shape, F32)
        for k in range(NUM_BUCKETS):
            acc = jnp.where(b == k, rb_ref[k, h] * LOG2E, acc)
        out_ref[0, h] = acc


def _bias_tiles(rel_bias, bucket):
    n, r, c = bucket.shape
    return pl.pallas_call(
        _bias_kernel,
        grid=(n,),
        in_specs=[pl.BlockSpec(memory_space=pltpu.SMEM),
                  pl.BlockSpec((1, r, c), lambda i: (i, 0, 0))],
        out_specs=pl.BlockSpec((1, H_A, r, c), lambda i: (i, 0, 0, 0)),
        out_shape=jax.ShapeDtypeStruct((n, H_A, r, c), F32),
        compiler_params=_cparams("parallel"),
        name="t5_bias",
    )(rel_bias, bucket)


def _dsa_prompt_kernel(rb_ref, aq_ref, iq_ref, iw_ref, ik2_ref, akv_ref, bias_ref, out_ref,
                       qi_ref, qa_ref, iwrep_ref, sc_ref, sct_ref, selb_ref, lg_ref, mx_ref,
                       p_ref, acc_ref, thr_ref, *, k_top, far_bucket, seq_len):
    i = pl.program_id(1)
    nkb = i + 1
    lane = lax.broadcasted_iota(I32, (TQ, LANES), 1)
    low = lane < HEAD_DIM

    for p in range(4):
        s_i = iq_ref[0, :, LANES * p:LANES * (p + 1)].astype(F32)
        qi_ref[(2 * p) * TQ:(2 * p + 1) * TQ, :] = jnp.where(low, s_i, 0.0).astype(BF16)
        qi_ref[(2 * p + 1) * TQ:(2 * p + 2) * TQ, :] = jnp.where(low, 0.0, s_i).astype(BF16)
        s_a = aq_ref[0, :, LANES * p:LANES * (p + 1)].astype(F32)
        qa_ref[p * TQ:(p + 1) * TQ, :] = jnp.where(low, s_a, 0.0).astype(BF16)
        qa_ref[(p + 4) * TQ:(p + 5) * TQ, :] = jnp.where(low, 0.0, s_a).astype(BF16)
    iw = iw_ref[0]
    for h in range(H_IDX):
        iwrep_ref[h] = jnp.broadcast_to(iw[:, h:h + 1], (TQ, TK))

    def blk(j):
        return pl.ds(pl.multiple_of(j * TK, TK), TK)

    r2 = lax.broadcasted_iota(I32, (TQ, TK), 0)
    c2 = lax.broadcasted_iota(I32, (TQ, TK), 1)

    def score_blk(j, slot, diagonal=False):
        kb = ik2_ref[0, blk(j), :]
        acc = jnp.zeros((TQ, TK), F32)
        for h in range(H_IDX):
            d = lax.dot_general(qi_ref[h * TQ:(h + 1) * TQ, :], kb, NT_DIMS, preferred_element_type=F32)
            acc = acc + iwrep_ref[h] * jnp.maximum(d, 0.0)
        if diagonal:
            acc = jnp.where((c2 // CHUNK) <= (r2 // CHUNK), acc, -jnp.inf)
        sc_ref[:, blk(j)] = acc
        sct_ref[blk(j), :] = acc.T

    _for_blocks(i, score_blk)
    score_blk(i, 0, diagonal=True)

    sub = lax.broadcasted_iota(I32, (SUBLANES, TQ), 0)

    def count(pred):
        def body(j, accs):
            base = pl.multiple_of(j * TK, TK)
            accs = list(accs)
            sblk = sct_ref[pl.ds(base, TK), :]
            for g in range(TK // SUBLANES):
                hit = jnp.where(pred(sblk[SUBLANES * g:SUBLANES * (g + 1), :], base + SUBLANES * g + sub), 1.0, 0.0)
                accs[g % len(accs)] = accs[g % len(accs)] + hit
            return tuple(accs)
        zero = jnp.zeros((SUBLANES, TQ), F32)
        a = lax.fori_loop(0, nkb, body, (zero, zero, zero, zero))
        tot = jnp.sum((a[0] + a[1]) + (a[2] + a[3]), axis=0, keepdims=True)
        return jnp.broadcast_to(tot, (SUBLANES, TQ))

    kf = float(k_top)
    c0 = count(lambda s, idx: s >= 0.0)
    ans_key = jnp.where(c0 >= kf, 0, INT_MIN).astype(I32)

    def bit_body(t, ans_key):
        cand_key = ans_key | lax.shift_left(jnp.int32(1), 30 - t)
        cand = _key_to_float(cand_key)
        c = count(lambda s, idx: s >= cand)
        return jnp.where(c >= kf, cand_key, ans_key)

    ans_key = lax.fori_loop(0, 31, bit_body, ans_key)
    ans = jnp.where(ans_key > KEY_NEG_INF, _key_to_float(ans_key), -jnp.inf)

    cgt = count(lambda s, idx: s > ans)
    ceq = count(lambda s, idx: s == ans)
    need = kf - cgt
    tie = (ceq > need) & (ans > -jnp.inf)
    flag = jnp.max(jnp.where(tie[0:1, :], 1.0, 0.0), axis=1, keepdims=True)
    thr_ref[...] = jnp.full((SUBLANES, TQ), seq_len, I32)

    @pl.when(flag[0, 0] > 0.0)
    def _():
        nbits = int(seq_len - 1).bit_length()

        def tie_body(t, m):
            cand = m | lax.shift_left(jnp.int32(1), nbits - 1 - t)
            c = count(lambda s, idx: (s == ans) & (idx < cand))
            return jnp.where(c < need, cand, m)

        m = lax.fori_loop(0, nbits, tie_body, jnp.zeros((SUBLANES, TQ), I32))
        thr_ref[...] = jnp.where(tie, m, seq_len)

    def to_rows(x):
        return jnp.broadcast_to(x[0:1, :], (LANES, TQ)).T

    ans_r = to_rows(ans)
    thr_r = to_rows(thr_ref[...])

    def selb_blk(j, c):
        sb = sc_ref[:, blk(j)]
        col = j * TK + lane
        for half in range(2):
            sh = sb[:, half * LANES:(half + 1) * LANES]
            ch = col + half * LANES
            sel = ((sh > ans_r) | ((sh == ans_r) & (ch <= thr_r))) & (sh > -jnp.inf)
            selb_ref[:, pl.ds(pl.multiple_of(j * TK + half * LANES, LANES), LANES)] = jnp.where(sel, 0.0, NEG)
        return c

    lax.fori_loop(0, nkb, selb_blk, 0)

    for h in range(H_A):
        mx_ref[h] = jnp.full((TQ, LANES), NEG, F32)
    acc_ref[...] = jnp.zeros_like(acc_ref)

    def pass_a(j, bias_of_head):
        kb = akv_ref[0, blk(j), 0:LANES]
        sb = selb_ref[:, blk(j)]
        for h in range(H_A):
            s = lax.dot_general(qa_ref[h * TQ:(h + 1) * TQ, :], kb, NT_DIMS, preferred_element_type=F32)
            lg = s + sb + bias_of_head(h)
            lg_ref[h, :, blk(j)] = lg
            mx_ref[h] = jnp.maximum(mx_ref[h], jnp.maximum(lg[:, :LANES], lg[:, LANES:]))

    _for_blocks(jnp.maximum(i - 1, 0), lambda j, slot: pass_a(j, lambda h: rb_ref[far_bucket, h] * LOG2E))

    @pl.when(i >= 1)
    def _():
        pass_a(i - 1, lambda h: bias_ref[0, h])

    pass_a(i, lambda h: bias_ref[1, h])

    for h in range(H_A):
        mx_ref[h] = jnp.broadcast_to(jnp.max(mx_ref[h], axis=1, keepdims=True), (TQ, LANES))

    half = (H_A // 2) * TQ

    def pass_b(j, slot):
        for h in range(H_A):
            lg = lg_ref[h, :, blk(j)]
            m = mx_ref[h]
            p_ref[slot, h * TQ:(h + 1) * TQ, 0:LANES] = jnp.exp2(lg[:, :LANES] - m).astype(BF16)
            p_ref[slot, h * TQ:(h + 1) * TQ, LANES:TK] = jnp.exp2(lg[:, LANES:] - m).astype(BF16)
        for g in range(HKV_A):
            acc_ref[g * half:(g + 1) * half, :] += jnp.dot(
                p_ref[slot, g * half:(g + 1) * half, :], akv_ref[0, blk(j), (g + 1) * LANES:(g + 2) * LANES],
                preferred_element_type=F32)

    _for_blocks(nkb, pass_b)

    for p in range(4):
        out_ref[0, :, LANES * p:LANES * (p + 1)] = _normalize_pair(
            acc_ref[p * TQ:(p + 1) * TQ, :], acc_ref[(p + 4) * TQ:(p + 5) * TQ, :], low).astype(BF16)


def _dsa_prompt(rel_bias, aq, iq, iw, ik2, akv, bias_tiles, far_bucket):
    b, t, _ = aq.shape
    k_top = min(K_TOP_MAX, t // 4)
    qrow = lambda c: pl.BlockSpec((1, TQ, c), lambda i, j: (i, j, 0))
    full = lambda c: pl.BlockSpec((1, t, c), lambda i, j: (i, 0, 0))
    kern = functools.partial(_dsa_prompt_kernel, k_top=k_top, far_bucket=far_bucket, seq_len=t)
    return pl.pallas_call(
        kern,
        grid=(b, t // TQ),
        in_specs=[pl.BlockSpec(memory_space=pltpu.SMEM),
                  qrow(512), qrow(512), qrow(8), full(128), full(384),
                  _resident((2, H_A, TQ, TK))],
        out_specs=qrow(512),
        out_shape=jax.ShapeDtypeStruct((b, t, 512), BF16),
        scratch_shapes=[
            pltpu.VMEM((H_IDX * TQ, LANES), BF16),
            pltpu.VMEM((H_A * TQ, LANES), BF16),
            pltpu.VMEM((H_IDX, TQ, TK), F32),
            pltpu.VMEM((TQ, t), F32),
            pltpu.VMEM((t, TQ), F32),
            pltpu.VMEM((TQ, t), F32),
            pltpu.VMEM((H_A, TQ, t), F32),
            pltpu.VMEM((H_A, TQ, LANES), F32),
            pltpu.VMEM((2, H_A * TQ, TK), BF16),
            pltpu.VMEM((H_A * TQ, LANES), F32),
            pltpu.VMEM((SUBLANES, TQ), I32),
        ],
        compiler_params=_cparams("parallel", "parallel"),
        name="dsa_prompt",
    )(rel_bias, aq, iq, iw, ik2, akv, bias_tiles)


def _fox_prompt_kernel(bq_ref, bk_ref, bv_ref, fkt_ref, out_ref,
                       qb_ref, fqrep_ref, lg_ref, mx_ref, p_ref, acc_ref):
    i = pl.program_id(1)
    lane = lax.broadcasted_iota(I32, (TQ, LANES), 1)
    low = lane < HEAD_DIM

    for p in range(4):
        s_b = bq_ref[0, :, LANES * p:LANES * (p + 1)].astype(F32)
        qb_ref[(2 * p) * TQ:(2 * p + 1) * TQ, :] = jnp.where(low, s_b, 0.0).astype(BF16)
        qb_ref[(2 * p + 1) * TQ:(2 * p + 2) * TQ, :] = jnp.where(low, 0.0, s_b).astype(BF16)
    fq_t = fkt_ref[0, :, pl.ds(pl.multiple_of(i * TQ, TQ), TQ)]
    for h in range(H_B):
        fqrep_ref[h] = jnp.broadcast_to(fq_t[h:h + 1, :], (LANES, TQ)).T
        mx_ref[h] = jnp.full((TQ, LANES), NEG, F32)
    acc_ref[...] = jnp.zeros_like(acc_ref)

    def blk(j):
        return pl.ds(pl.multiple_of(j * TK, TK), TK)

    r2 = lax.broadcasted_iota(I32, (TQ, LANES), 0)

    def pass_a(j, diagonal):
        for p in range(4):
            s = lax.dot_general(qb_ref[(2 * p) * TQ:(2 * p + 2) * TQ, :],
                                bk_ref[0, blk(j), LANES * p:LANES * (p + 1)],
                                NT_DIMS, preferred_element_type=F32)
            for e in range(2):
                h = 2 * p + e
                fk = fkt_ref[0, h:h + 1, blk(j)]
                halves = []
                for half in range(2):
                    lg = (s[e * TQ:(e + 1) * TQ, half * LANES:(half + 1) * LANES]
                          + (fqrep_ref[h] - fk[:, half * LANES:(half + 1) * LANES]))
                    if diagonal:
                        lg = jnp.where(lane + half * LANES <= r2, lg, NEG)
                    lg_ref[h, :, pl.ds(pl.multiple_of(j * TK + half * LANES, LANES), LANES)] = lg
                    halves.append(lg)
                mx_ref[h] = jnp.maximum(mx_ref[h], jnp.maximum(halves[0], halves[1]))

    _for_blocks(i, lambda j, slot: pass_a(j, False))
    pass_a(i, True)

    for h in range(H_B):
        mx_ref[h] = jnp.broadcast_to(jnp.max(mx_ref[h], axis=1, keepdims=True), (TQ, LANES))

    def pass_b(j, slot):
        for h in range(H_B):
            lg = lg_ref[h, :, blk(j)]
            m = mx_ref[h]
            p_ref[slot, h * TQ:(h + 1) * TQ, 0:LANES] = jnp.exp2(lg[:, :LANES] - m).astype(BF16)
            p_ref[slot, h * TQ:(h + 1) * TQ, LANES:TK] = jnp.exp2(lg[:, LANES:] - m).astype(BF16)
            acc_ref[h * TQ:(h + 1) * TQ, :] += jnp.dot(
                p_ref[slot, h * TQ:(h + 1) * TQ, :], bv_ref[0, blk(j), LANES * h:LANES * (h + 1)],
                preferred_element_type=F32)

    _for_blocks(i + 1, pass_b)

    for p in range(4):
        out_ref[0, :, LANES * p:LANES * (p + 1)] = _normalize_pair(
            acc_ref[(2 * p) * TQ:(2 * p + 1) * TQ, :], acc_ref[(2 * p + 1) * TQ:(2 * p + 2) * TQ, :], low).astype(BF16)


def _fox_prompt(bq, bk, bv, fkt):
    b, t, _ = bq.shape
    qrow = lambda c: pl.BlockSpec((1, TQ, c), lambda i, j: (i, j, 0))
    full = lambda c: pl.BlockSpec((1, t, c), lambda i, j: (i, 0, 0))
    return pl.pallas_call(
        _fox_prompt_kernel,
        grid=(b, t // TQ),
        in_specs=[qrow(512), full(512), full(1024),
                  pl.BlockSpec((1, H_B, t), lambda i, j: (i, 0, 0))],
        out_specs=qrow(512),
        out_shape=jax.ShapeDtypeStruct((b, t, 512), BF16),
        scratch_shapes=[
            pltpu.VMEM((H_B * TQ, LANES), BF16),
            pltpu.VMEM((H_B, TQ, LANES), F32),
            pltpu.VMEM((H_B, TQ, t), F32),
            pltpu.VMEM((H_B, TQ, LANES), F32),
            pltpu.VMEM((2, H_B * TQ, TK), BF16),
            pltpu.VMEM((H_B * TQ, LANES), F32),
        ],
        compiler_params=_cparams("parallel", "parallel"),
        name="fox_prompt",
    )(bq, bk, bv, fkt)


FF_BLK = 1024


def _tail_kernel(x_ref, oa_ref, ob_ref, woa_ref, wob_ref, gm_ref, wup_ref, wdn_ref, gf_ref, y_ref):
    mixed = (jnp.dot(oa_ref[...], woa_ref[...], preferred_element_type=F32)
             + jnp.dot(ob_ref[...], wob_ref[...], preferred_element_type=F32))
    x1 = x_ref[...] + mixed
    h2 = _rms(x1, gm_ref[...]).astype(BF16)
    acc = x1
    for c in range(D_FF // FF_BLK):
        u = jnp.dot(h2, wup_ref[:, c * FF_BLK:(c + 1) * FF_BLK], preferred_element_type=F32)
        u = jnp.square(jnp.maximum(u, 0.0)).astype(BF16)
        acc = acc + jnp.dot(u, wdn_ref[c * FF_BLK:(c + 1) * FF_BLK, :], preferred_element_type=F32)
    y_ref[...] = _rms(acc, gf_ref[...])


def _tail(x, oa, ob, woa, wob, g_mlp, w_up, w_down, g_final, tm):
    n, d = x.shape
    row = lambda c: pl.BlockSpec((tm, c), lambda i: (i, 0))
    return pl.pallas_call(
        _tail_kernel,
        grid=(n // tm,),
        in_specs=[row(d), row(512), row(512), _resident((512, d)), _resident((512, d)), _resident((1, d)),
                  _resident((d, D_FF)), _resident((D_FF, d)), _resident((1, d))],
        out_specs=row(d),
        out_shape=jax.ShapeDtypeStruct((n, d), F32),
        compiler_params=_cparams("parallel"),
        name="tail",
    )(x, oa, ob, woa, wob, g_mlp, w_up, w_down, g_final)


def _dsa_sample_kernel(aq_ref, iq_ref, iw_ref, ik2_ref, akv_ref, cikt_ref, cakt_ref, cavt_ref, bias_ref, out_ref,
                       *, past, n_new, l_pad, k_top, group):
    l_all = past + n_new
    n_tail = l_pad - past
    lane = lax.broadcasted_iota(I32, (n_new, LANES), 1)
    low = lane < HEAD_DIM

    def pad_rows(x):
        return jnp.concatenate([x, jnp.zeros((n_tail - n_new, x.shape[1]), F32)], axis=0).astype(BF16)

    scores = []
    for g in range(group):
        ik_new = pad_rows(ik2_ref[g].astype(F32)[:, 0:D_IDX])
        iq32 = iq_ref[g].astype(F32)
        qi = jnp.concatenate([iq32[:, D_IDX * h:D_IDX * (h + 1)] for h in range(H_IDX)], axis=0).astype(BF16)
        d = jnp.concatenate(
            [jnp.dot(qi, cikt_ref[g].astype(BF16), preferred_element_type=F32),
             lax.dot_general(qi, ik_new, NT_DIMS, preferred_element_type=F32)], axis=1)
        iw = iw_ref[g]
        score = jnp.zeros((n_new, l_pad), F32)
        for h in range(H_IDX):
            score = score + iw[:, h:h + 1] * jnp.maximum(d[h * n_new:(h + 1) * n_new, :], 0.0)
        scores.append(score)
    score = jnp.concatenate(scores, axis=0)

    rows = group * n_new
    row = lax.broadcasted_iota(I32, (rows, l_pad), 0) % n_new
    col = lax.broadcasted_iota(I32, (rows, l_pad), 1)
    adm = (col < l_all) & ((col // CHUNK) <= ((past + row) // CHUNK))
    sc = jnp.where(adm, score, -jnp.inf)

    def count(mask):
        return jnp.sum(jnp.where(mask, 1.0, 0.0), axis=1, keepdims=True)

    kf = float(k_top)
    ans_key = jnp.where(count(sc >= 0.0) >= kf, 0, INT_MIN).astype(I32)

    def bit_body(t, ans_key):
        cand_key = ans_key | lax.shift_left(jnp.int32(1), 30 - t)
        return jnp.where(count(sc >= _key_to_float(cand_key)) >= kf, cand_key, ans_key)

    ans_key = lax.fori_loop(0, 31, bit_body, ans_key)
    ans = jnp.where(ans_key > KEY_NEG_INF, _key_to_float(ans_key), -jnp.inf)
    eq = sc == ans
    need = kf - count(sc > ans)
    tie = (count(eq) > need) & (ans > -jnp.inf)
    flag = jnp.max(jnp.where(tie, 1.0, 0.0), axis=0, keepdims=True)
    nbits = int(l_pad - 1).bit_length()

    def tie_search(_):
        def tie_body(t, m):
            cand = m | lax.shift_left(jnp.int32(1), nbits - 1 - t)
            return jnp.where(count(eq & (col < cand)) < need, cand, m)
        m = lax.fori_loop(0, nbits, tie_body, jnp.zeros((rows, 1), I32))
        return jnp.where(tie, m, l_pad)

    thr = lax.cond(flag[0, 0] > 0.0, tie_search, lambda _: jnp.full((rows, 1), l_pad, I32), 0)
    sel = ((sc > ans) | (eq & (col <= thr))) & (sc > -jnp.inf)
    selb_all = jnp.where(sel, 0.0, NEG)

    for g in range(group):
        akv = akv_ref[g].astype(F32)
        ak_new = pad_rows(akv[:, 0:LANES])
        av_new = pad_rows(jnp.where(low, akv[:, LANES:2 * LANES], akv[:, 2 * LANES:3 * LANES]))
        selb = selb_all[g * n_new:(g + 1) * n_new, :]
        aq32 = aq_ref[g].astype(F32)
        slabs = [aq32[:, LANES * p:LANES * (p + 1)] for p in range(4)]
        qa = jnp.concatenate([jnp.where(low, s, 0.0) for s in slabs] + [jnp.where(low, 0.0, s) for s in slabs],
                             axis=0).astype(BF16)
        s = jnp.concatenate(
            [jnp.dot(qa, cakt_ref[g].astype(BF16), preferred_element_type=F32),
             lax.dot_general(qa, ak_new, NT_DIMS, preferred_element_type=F32)], axis=1)
        lg = s + bias_ref[...] + jnp.concatenate([selb] * H_A, axis=0)
        m = jnp.max(lg, axis=1, keepdims=True)
        p = jnp.exp2(lg - m)
        l = jnp.sum(p, axis=1, keepdims=True)
        pb = p.astype(BF16)
        o = (lax.dot_general(pb[:, 0:past], cavt_ref[g].astype(BF16), NT_DIMS, preferred_element_type=F32)
             + jnp.dot(pb[:, past:l_pad], av_new, preferred_element_type=F32)) / l
        for q in range(4):
            out_ref[g, :, LANES * q:LANES * (q + 1)] = jnp.where(
                low, o[q * n_new:(q + 1) * n_new, :], o[(q + 4) * n_new:(q + 5) * n_new, :]).astype(BF16)


SAMPLE_GROUP = 4


def _dsa_sample(aq, iq, iw, ik2, akv, cikt, cakt, cavt, bias_rows, past):
    b, n_new, _ = aq.shape
    l_pad = bias_rows.shape[1]
    assert past % LANES == 0 and l_pad == past + LANES and n_new <= LANES and b % SAMPLE_GROUP == 0
    k_top = min(K_TOP_MAX, (past + n_new) // 4)
    new = lambda c: pl.BlockSpec((SAMPLE_GROUP, n_new, c), lambda i: (i, 0, 0))
    old = lambda c: pl.BlockSpec((SAMPLE_GROUP, c, past), lambda i: (i, 0, 0))
    kern = functools.partial(_dsa_sample_kernel, past=past, n_new=n_new, l_pad=l_pad, k_top=k_top,
                             group=SAMPLE_GROUP)
    return pl.pallas_call(
        kern,
        grid=(b // SAMPLE_GROUP,),
        in_specs=[new(512), new(512), new(8), new(128), new(384), old(D_IDX), old(LANES), old(LANES),
                  _resident((H_A * n_new, l_pad))],
        out_specs=new(512),
        out_shape=jax.ShapeDtypeStruct((b, n_new, 512), BF16),
        compiler_params=_cparams("parallel"),
        name="dsa_sample",
    )(aq, iq, iw, ik2, akv, cikt, cakt, cavt, bias_rows)


def _fox_sample_kernel(bq_ref, bkn_ref, bvn_ref, ckt_ref, cvt_ref, fkt_ref, tot_ref, lf_ref, lft_ref, out_ref,
                       *, past, n_new):
    lf = lf_ref[0] * LOG2E
    lft = lft_ref[0] * LOG2E
    rown = lax.broadcasted_iota(I32, (n_new, H_B), 0)
    lanen = lax.broadcasted_iota(I32, (H_B, n_new), 1)
    fq = jnp.broadcast_to(tot_ref[0], (n_new, H_B))
    fqt = jnp.broadcast_to(fkt_ref[0][:, past - 1:past], (H_B, n_new))
    for s in range(n_new):
        fq = fq + jnp.where(rown >= s, lf[s:s + 1, :], 0.0)
        fqt = fqt + jnp.where(lanen >= s, lft[:, s:s + 1], 0.0)

    kct = ckt_ref[0].astype(BF16)
    vct = cvt_ref[0].astype(BF16)
    bq32 = bq_ref[0].astype(F32)
    head_of_lane = lax.broadcasted_iota(I32, (n_new, H_B * HEAD_DIM), 1) // HEAD_DIM
    qb = jnp.concatenate([jnp.where(head_of_lane == h, bq32, 0.0) for h in range(H_B)], axis=0).astype(BF16)
    s_past = jnp.dot(qb, kct, preferred_element_type=F32)
    s_new = lax.dot_general(qb, bkn_ref[0], NT_DIMS, preferred_element_type=F32)
    fkt = fkt_ref[0]
    fq_col = jnp.concatenate([fq[:, h:h + 1] for h in range(H_B)], axis=0)
    fk_past = jnp.concatenate([jnp.broadcast_to(fkt[h:h + 1, :], (n_new, past)) for h in range(H_B)], axis=0)
    fk_new = jnp.concatenate([jnp.broadcast_to(fqt[h:h + 1, :], (n_new, n_new)) for h in range(H_B)], axis=0)
    lg_past = s_past + (fq_col - fk_past)
    trow = lax.broadcasted_iota(I32, (H_B * n_new, n_new), 0) % n_new
    tcol = lax.broadcasted_iota(I32, (H_B * n_new, n_new), 1)
    lg_new = jnp.where(tcol <= trow, s_new + (fq_col - fk_new), NEG)
    m = jnp.maximum(jnp.max(lg_past, axis=1, keepdims=True), jnp.max(lg_new, axis=1, keepdims=True))
    p_past = jnp.exp2(lg_past - m)
    p_new = jnp.exp2(lg_new - m)
    l = jnp.sum(p_past, axis=1, keepdims=True) + jnp.sum(p_new, axis=1, keepdims=True)
    o = (lax.dot_general(p_past.astype(BF16), vct, NT_DIMS, preferred_element_type=F32)
         + jnp.dot(p_new.astype(BF16), bvn_ref[0].astype(BF16), preferred_element_type=F32)) / l
    out = jnp.zeros((n_new, H_B * HEAD_DIM), F32)
    for h in range(H_B):
        out = out + jnp.where(head_of_lane == h, o[h * n_new:(h + 1) * n_new, :], 0.0)
    out_ref[0] = out.astype(BF16)


def _fox_sample(bq, bkn, bvn, ckt, cvt, fkt, tot, lf, lft, past):
    b, n_new, _ = bq.shape
    new = lambda c: pl.BlockSpec((1, n_new, c), lambda i: (i, 0, 0))
    old = pl.BlockSpec((1, H_B * HEAD_DIM, past), lambda i: (i, 0, 0))
    kern = functools.partial(_fox_sample_kernel, past=past, n_new=n_new)
    return pl.pallas_call(
        kern,
        grid=(b,),
        in_specs=[new(512), new(512), new(512), old, old,
                  pl.BlockSpec((1, H_B, past), lambda i: (i, 0, 0)),
                  pl.BlockSpec((1, 1, H_B), lambda i: (i, 0, 0)),
                  new(H_B),
                  pl.BlockSpec((1, H_B, n_new), lambda i: (i, 0, 0))],
        out_specs=new(512),
        out_shape=jax.ShapeDtypeStruct((b, n_new, 512), BF16),
        compiler_params=_cparams("parallel"),
        name="fox_sample",
    )(bq, bkn, bvn, ckt, cvt, fkt, tot, lf, lft)


def _prep_weights(w_in, w_o, w_up, w_down):
    offs = np.concatenate([[0], np.cumsum(IN_SPLITS)])
    seg = lambda k: w_in[:, int(offs[k]):int(offs[k + 1])]
    a_q, a_k, a_v, i_q, i_k, i_w, b_q, b_k, b_v, f_z = (seg(k) for k in range(10))
    a_q_pairs = jnp.concatenate([a_q[:, HEAD_DIM * h:HEAD_DIM * (h + 1)] for h in AQ_HEAD_ORDER], axis=1)
    pad = jnp.zeros((w_in.shape[0], N_CAT - C_SM - 16), w_in.dtype)
    w_cat = jnp.concatenate([a_q_pairs, i_q, b_q, a_k, a_v, b_k, b_v, i_k, i_k, i_w, f_z, pad], axis=1)
    wo_a = jnp.concatenate([w_o[HEAD_DIM * h:HEAD_DIM * (h + 1)] for h in AQ_HEAD_ORDER], axis=0)
    wo_b = w_o[H_A * HEAD_DIM:]
    return (w_cat.astype(BF16), wo_a.astype(BF16), wo_b.astype(BF16), w_up.astype(BF16), w_down.astype(BF16))


def kernel(x_prompt, x_sample, cache_a_k, cache_a_v, cache_idx_k, cache_b_k, cache_b_v, cache_b_logf,
           w_in, w_o, b_f, rel_bias, g_attn, w_up, w_down, g_mlp, g_final):
    assert w_in.shape[0] == 1, "single-layer trunk"
    bp, tp, d = x_prompt.shape
    bs, ts, _ = x_sample.shape
    past = cache_a_k.shape[2]
    w_cat, wo_a, wo_b, wup, wdn = _prep_weights(w_in[0], w_o[0], w_up[0], w_down[0])
    g_a = g_attn[0].reshape(1, d)
    g_m = g_mlp[0].reshape(1, d)
    g_f = g_final.reshape(1, d)
    bf = b_f[0].reshape(H_B, 1)
    rel_bias = rel_bias.astype(F32)

    (aq, iq, bq, akv, bk, bv, ik2, ak32, av32, bk32, bv32, iw, ikt32, logft) = _inproj(x_prompt, g_a, w_cat, bf, 512)
    fkt = _cumsum_lanes(logft)
    r = np.arange(TQ)[:, None]
    c = np.arange(TK)[None, :]
    bucket = _t5_bucket_np(np.stack([c - TK - r, c - r]))
    far_bucket = int(_t5_bucket_np(np.array(-TK - 1)))
    assert far_bucket == int(_t5_bucket_np(np.array(-tp)))
    bias_tiles = _bias_tiles(rel_bias, jnp.asarray(bucket))
    out_a = _dsa_prompt(rel_bias, aq, iq, iw, ik2, akv, bias_tiles, far_bucket)
    out_b = _fox_prompt(bq, bk, bv, fkt)
    y_p = _tail(x_prompt.reshape(bp * tp, d), out_a.reshape(bp * tp, 512), out_b.reshape(bp * tp, 512),
                wo_a, wo_b, g_m, wup, wdn, g_f, 512).reshape(bp, tp, d)

    n_s = bs * ts
    outs = _inproj(x_sample.reshape(1, n_s, d), g_a, w_cat, bf, n_s)
    (aq_s, iq_s, bq_s, akv_s, bk_s, bv_s, ik2_s, ak32_s, av32_s, bk32_s, bv32_s, iw_s) = (
        o.reshape(bs, ts, o.shape[-1]) for o in outs[:12])
    ik32_s = jnp.swapaxes(outs[12][0], 0, 1).reshape(bs, ts, D_IDX)
    logf_s = jnp.swapaxes(outs[13][0], 0, 1).reshape(bs, ts, H_B)
    l_all = past + ts
    l_pad = -(-l_all // LANES) * LANES
    rel_s = np.arange(l_pad)[None, :] - (past + np.arange(ts))[:, None]
    bias_s = _bias_tiles(rel_bias, jnp.asarray(_t5_bucket_np(rel_s))[None])[0].reshape(H_A * ts, l_pad)
    def feature_major(cache):
        c = cache[0]
        c = jnp.transpose(c, (0, 2, 3, 1)) if c.ndim == 4 else jnp.transpose(c, (0, 2, 1))
        return c.reshape(bs, -1, past)

    out_a_s = _dsa_sample(aq_s, iq_s, iw_s, ik2_s, akv_s, feature_major(cache_idx_k),
                          feature_major(cache_a_k), feature_major(cache_a_v), bias_s, past)
    fkt_c = _cumsum_lanes(jnp.swapaxes(cache_b_logf[0].astype(F32), 1, 2))
    tot = fkt_c[:, :, past - 1].reshape(bs, 1, H_B)
    out_b_s = _fox_sample(bq_s, bk_s, bv32_s, feature_major(cache_b_k), feature_major(cache_b_v), fkt_c, tot,
                          logf_s, jnp.swapaxes(logf_s, 1, 2), past)
    y_s = _tail(x_sample.reshape(n_s, d), out_a_s.reshape(n_s, 512), out_b_s.reshape(n_s, 512),
                wo_a, wo_b, g_m, wup, wdn, g_f, n_s).reshape(bs, ts, d)

    def rows(a, heads, b, t):
        return a.reshape(1, b, t, heads, HEAD_DIM)

    return (y_p, y_s,
            rows(ak32, HKV_A, bp, tp), rows(av32, HKV_A, bp, tp), jnp.swapaxes(ikt32, 1, 2)[None],
            rows(bk32, H_B, bp, tp), rows(bv32, H_B, bp, tp), jnp.swapaxes(logft, 1, 2)[None],
            rows(ak32_s, HKV_A, bs, ts), rows(av32_s, HKV_A, bs, ts), ik32_s.reshape(1, bs, ts, D_IDX),
            rows(bk32_s, H_B, bs, ts), rows(bv32_s, H_B, bs, ts), logf_s.reshape(1, bs, ts, H_B))
```

```python
import functools
import math

import numpy as np
import jax
import jax.numpy as jnp
from jax import lax
from jax.experimental import pallas as pl
from jax.experimental.pallas import tpu as pltpu

F32 = jnp.float32
BF16 = jnp.bfloat16
I32 = jnp.int32

D_MODEL = 1024
CHUNK = 64
HEAD_DIM = 64
H_A = 8
HKV_A = 2
H_B = 8
H_IDX = 8
D_IDX = 64
K_TOP_MAX = 256
NUM_BUCKETS = 32
MAX_DISTANCE = 128
D_FF = 4 * D_MODEL
EPS = 1e-6
IN_SPLITS = (H_A * HEAD_DIM, HKV_A * HEAD_DIM, HKV_A * HEAD_DIM, H_IDX * D_IDX, D_IDX, H_IDX,
             H_B * HEAD_DIM, H_B * HEAD_DIM, H_B * HEAD_DIM, H_B)

LANES = 128
SUBLANES = 8
VMEM_LIMIT = 56 * 1024 * 1024

TQ = 256
TK = 256
NEG = -1e30
LOG2E = 1.4426950408889634
KEY_NEG_INF = -2139095041
INT_MIN = -2147483648

C_AQ, C_IQ, C_BQ, C_AKV, C_BK, C_BV, C_IK2, C_SM, N_CAT = 0, 512, 1024, 1536, 1792, 2304, 2816, 2944, 3072
AQ_HEAD_ORDER = (0, 4, 1, 5, 2, 6, 3, 7)

NT_DIMS = (((1,), (1,)), ((), ()))


def _cparams(*sem):
    return pltpu.CompilerParams(dimension_semantics=sem, vmem_limit_bytes=VMEM_LIMIT)


def _resident(shape):
    nd = len(shape)
    return pl.BlockSpec(shape, lambda *_: (0,) * nd, pipeline_mode=pl.Buffered(1))


def _rms(x, g):
    ms = jnp.mean(x * x, axis=-1, keepdims=True)
    return (x * lax.rsqrt(ms + EPS)) * g


def _for_blocks(n, body):
    def pair(k, c):
        body(2 * k, 0)
        body(2 * k + 1, 1)
        return c

    lax.fori_loop(0, n // 2, pair, 0)

    @pl.when(n % 2 == 1)
    def _():
        body(n - 1, 0)


def _normalize_pair(a_lo, a_hi, low):
    num = jnp.where(low, a_lo, a_hi)
    den = pltpu.roll(jnp.where(low, a_hi, a_lo), HEAD_DIM, axis=1)
    return num / den


def _key_to_float(k):
    return lax.bitcast_convert_type(k ^ (lax.shift_right_arithmetic(k, 31) & 0x7FFFFFFF), F32)


def _t5_bucket_np(rel):
    half = NUM_BUCKETS // 2
    max_exact = half // 2
    ret = np.where(rel > 0, half, 0)
    n = np.abs(rel)
    n_f = np.maximum(n, max_exact).astype(np.float64)
    large = max_exact + (np.log(n_f / max_exact) / math.log(MAX_DISTANCE / max_exact)
                         * (half - max_exact)).astype(np.int32)
    large = np.minimum(large, half - 1)
    return (ret + np.where(n < max_exact, n, large)).astype(np.int32)


def _inproj_kernel(x_ref, g_ref, w_ref, bf_ref,
                   aq_ref, iq_ref, bq_ref, akv_ref, bk_ref, bv_ref, ik2_ref,
                   ak32_ref, av32_ref, bk32_ref, bv32_ref, iw_ref, ikt_ref, logft_ref):
    h = _rms(x_ref[0], g_ref[...])
    p = jnp.dot(h.astype(BF16), w_ref[...], preferred_element_type=F32)
    qscale = HEAD_DIM ** -0.5 * LOG2E
    aq_ref[0] = (p[:, C_AQ:C_AQ + 512] * qscale).astype(BF16)
    iq_ref[0] = (p[:, C_IQ:C_IQ + 512] * (D_IDX ** -0.5)).astype(BF16)
    bq_ref[0] = (p[:, C_BQ:C_BQ + 512] * qscale).astype(BF16)
    low = lax.broadcasted_iota(I32, (p.shape[0], LANES), 1) < HEAD_DIM
    av = p[:, C_AKV + LANES:C_AKV + 2 * LANES]
    akv_ref[0, :, 0:LANES] = p[:, C_AKV:C_AKV + LANES].astype(BF16)
    akv_ref[0, :, LANES:2 * LANES] = jnp.where(low, av, 1.0).astype(BF16)
    akv_ref[0, :, 2 * LANES:3 * LANES] = jnp.where(low, 1.0, av).astype(BF16)
    bk_ref[0] = p[:, C_BK:C_BK + 512].astype(BF16)
    for q in range(4):
        bv = p[:, C_BV + LANES * q:C_BV + LANES * (q + 1)]
        bv_ref[0, :, 2 * q * LANES:(2 * q + 1) * LANES] = jnp.where(low, bv, 1.0).astype(BF16)
        bv_ref[0, :, (2 * q + 1) * LANES:(2 * q + 2) * LANES] = jnp.where(low, 1.0, bv).astype(BF16)
    ik2_ref[0] = p[:, C_IK2:C_IK2 + 128].astype(BF16)
    ak32_ref[0] = p[:, C_AKV:C_AKV + 128]
    av32_ref[0] = p[:, C_AKV + 128:C_AKV + 256]
    bk32_ref[0] = p[:, C_BK:C_BK + 512]
    bv32_ref[0] = p[:, C_BV:C_BV + 512]
    iw_ref[0] = p[:, C_SM:C_SM + 8] * (H_IDX ** -0.5)
    ikt_ref[0] = p[:, C_IK2:C_IK2 + LANES].T[0:D_IDX, :]
    z = p[:, C_SM:C_SM + LANES].T[8:16, :] + bf_ref[...]
    logft_ref[0] = jnp.minimum(z, 0.0) - jnp.log1p(jnp.exp(-jnp.abs(z)))


def _inproj(x, g, w_cat, b_f, tm):
    b, t, d = x.shape
    grid = (b, t // tm)
    row = lambda c: pl.BlockSpec((1, tm, c), lambda i, j: (i, j, 0))
    col = lambda c: pl.BlockSpec((1, c, tm), lambda i, j: (i, 0, j))
    outs = [(512, BF16), (512, BF16), (512, BF16), (384, BF16), (512, BF16), (1024, BF16), (128, BF16),
            (128, F32), (128, F32), (512, F32), (512, F32), (8, F32)]
    outs_t = [D_IDX, H_B]
    return pl.pallas_call(
        _inproj_kernel,
        grid=grid,
        in_specs=[row(d), _resident((1, d)), _resident((d, N_CAT)), _resident((H_B, 1))],
        out_specs=[row(c) for c, _ in outs] + [col(c) for c in outs_t],
        out_shape=([jax.ShapeDtypeStruct((b, t, c), dt) for c, dt in outs]
                   + [jax.ShapeDtypeStruct((b, c, t), F32) for c in outs_t]),
        compiler_params=_cparams("parallel", "parallel"),
        name="inproj",
    )(x, g, w_cat, b_f)


CS_BLK = 512


def _cumsum_kernel(x_ref, out_ref, carry_ref):
    @pl.when(pl.program_id(0) == 0)
    def _():
        carry_ref[...] = jnp.zeros_like(carry_ref)

    x = x_ref[...] * LOG2E
    hi = x.astype(BF16)
    r1 = x - hi.astype(F32)
    mid = r1.astype(BF16)
    lo = (r1 - mid.astype(F32)).astype(BF16)
    src = lax.broadcasted_iota(I32, (CS_BLK, CS_BLK), 0)
    dst = lax.broadcasted_iota(I32, (CS_BLK, CS_BLK), 1)
    tri = jnp.where(src <= dst, 1.0, 0.0).astype(BF16)
    cs = (jnp.dot(hi, tri, preferred_element_type=F32)
          + jnp.dot(mid, tri, preferred_element_type=F32)
          + jnp.dot(lo, tri, preferred_element_type=F32))
    out = cs + carry_ref[...]
    out_ref[...] = out
    carry_ref[...] = jnp.broadcast_to(out[:, CS_BLK - 1:CS_BLK], carry_ref.shape)


def _cumsum_lanes(x_t):
    b, h, l = x_t.shape
    spec = pl.BlockSpec((b * h, CS_BLK), lambda j: (0, j))
    return pl.pallas_call(
        _cumsum_kernel,
        grid=(l // CS_BLK,),
        in_specs=[spec],
        out_specs=spec,
        out_shape=jax.ShapeDtypeStruct((b * h, l), F32),
        scratch_shapes=[pltpu.VMEM((b * h, CS_BLK), F32)],
        compiler_params=_cparams("arbitrary"),
        name="cumsum",
    )(x_t.reshape(b * h, l)).reshape(b, h, l)


def _bias_kernel(rb_ref, bucket_ref, out_ref):
    b = bucket_ref[0]
    for h in range(H_A):
        acc = jnp.zeros(b.shape, F32)
        for k in range(NUM_BUCKETS):
            acc = jnp.where(b == k, rb_ref[k, h] * LOG2E, acc)
        out_ref[0, h] = acc


def _bias_tiles(rel_bias, bucket):
    n, r, c = bucket.shape
    return pl.pallas_call(
        _bias_kernel,
        grid=(n,),
        in_specs=[pl.BlockSpec(memory_space=pltpu.SMEM),
                  pl.BlockSpec((1, r, c), lambda i: (i, 0, 0))],
        out_specs=pl.BlockSpec((1, H_A, r, c), lambda i: (i, 0, 0, 0)),
        out_shape=jax.ShapeDtypeStruct((n, H_A, r, c), F32),
        compiler_params=_cparams("parallel"),
        name="t5_bias",
    )(rel_bias, bucket)


def _dsa_prompt_kernel(rb_ref, aq_ref, iq_ref, iw_ref, ik2_ref, akv_ref, bias_ref, out_ref,
                       qi_ref, qa_ref, iwrep_ref, sc_ref, sct_ref, selb_ref, lg_ref, mx_ref,
                       p_ref, acc_ref, thr_ref, ans_ref, *, k_top, far_bucket, seq_len):
    i = pl.program_id(1)
    nkb = i + 1
    lane = lax.broadcasted_iota(I32, (TQ, LANES), 1)
    low = lane < HEAD_DIM

    for p in range(4):
        s_i = iq_ref[0, :, LANES * p:LANES * (p + 1)].astype(F32)
        qi_ref[(2 * p) * TQ:(2 * p + 1) * TQ, :] = jnp.where(low, s_i, 0.0).astype(BF16)
        qi_ref[(2 * p + 1) * TQ:(2 * p + 2) * TQ, :] = jnp.where(low, 0.0, s_i).astype(BF16)
        s_a = aq_ref[0, :, LANES * p:LANES * (p + 1)].astype(F32)
        qa_ref[p * TQ:(p + 1) * TQ, :] = jnp.where(low, s_a, 0.0).astype(BF16)
        qa_ref[(p + 4) * TQ:(p + 5) * TQ, :] = jnp.where(low, 0.0, s_a).astype(BF16)
    iw = iw_ref[0]
    for h in range(H_IDX):
        iwrep_ref[h] = jnp.broadcast_to(iw[:, h:h + 1], (TQ, TK))

    def blk(j):
        return pl.ds(pl.multiple_of(j * TK, TK), TK)

    r2 = lax.broadcasted_iota(I32, (TQ, TK), 0)
    c2 = lax.broadcasted_iota(I32, (TQ, TK), 1)

    def score_blk(j, slot, diagonal=False):
        kb = ik2_ref[0, blk(j), :]
        acc = jnp.zeros((TQ, TK), F32)
        for h in range(H_IDX):
            d = lax.dot_general(qi_ref[h * TQ:(h + 1) * TQ, :], kb, NT_DIMS, preferred_element_type=F32)
            acc = acc + iwrep_ref[h] * jnp.maximum(d, 0.0)
        if diagonal:
            acc = jnp.where((c2 // CHUNK) <= (r2 // CHUNK), acc, -jnp.inf)
        sc_ref[:, blk(j)] = acc
        sct_ref[blk(j), :] = acc.T

    _for_blocks(i, score_blk)
    score_blk(i, 0, diagonal=True)

    sub = lax.broadcasted_iota(I32, (SUBLANES, TQ), 0)
    kf = float(k_top)

    def search(n_blocks):
        def count(pred):
            accs = [jnp.zeros((SUBLANES, TQ), F32)] * 4
            for j in range(n_blocks):
                for g in range(TK // SUBLANES):
                    r0 = j * TK + SUBLANES * g
                    hit = jnp.where(pred(sct_ref[r0:r0 + SUBLANES, :], r0 + sub), 1.0, 0.0)
                    accs[g % 4] = accs[g % 4] + hit
            tot = jnp.sum((accs[0] + accs[1]) + (accs[2] + accs[3]), axis=0, keepdims=True)
            return jnp.broadcast_to(tot, (SUBLANES, TQ))

        c0 = count(lambda s, idx: s >= 0.0)
        ans_key = jnp.where(c0 >= kf, 0, INT_MIN).astype(I32)

        def bit_body(t, ans_key):
            cand_key = ans_key | lax.shift_left(jnp.int32(1), 30 - t)
            cand = _key_to_float(cand_key)
            c = count(lambda s, idx: s >= cand)
            return jnp.where(c >= kf, cand_key, ans_key)

        ans_key = lax.fori_loop(0, 31, bit_body, ans_key)
        ans = jnp.where(ans_key > KEY_NEG_INF, _key_to_float(ans_key), -jnp.inf)
        ans_ref[...] = ans

        cgt = count(lambda s, idx: s > ans)
        ceq = count(lambda s, idx: s == ans)
        need = kf - cgt
        tie = (ceq > need) & (ans > -jnp.inf)
        flag = jnp.max(jnp.where(tie[0:1, :], 1.0, 0.0), axis=1, keepdims=True)
        thr_ref[...] = jnp.full((SUBLANES, TQ), seq_len, I32)

        @pl.when(flag[0, 0] > 0.0)
        def _():
            nbits = int(seq_len - 1).bit_length()

            def tie_body(t, m):
                cand = m | lax.shift_left(jnp.int32(1), nbits - 1 - t)
                c = count(lambda s, idx: (s == ans) & (idx < cand))
                return jnp.where(c < need, cand, m)

            m = lax.fori_loop(0, nbits, tie_body, jnp.zeros((SUBLANES, TQ), I32))
            thr_ref[...] = jnp.where(tie, m, seq_len)

    for n in range(seq_len // TQ):
        pl.when(i == n)(functools.partial(search, n + 1))

    def to_rows(x):
        return jnp.broadcast_to(x[0:1, :], (LANES, TQ)).T

    ans_r = to_rows(ans_ref[...])
    thr_r = to_rows(thr_ref[...])

    def selb_blk(j, c):
        sb = sc_ref[:, blk(j)]
        col = j * TK + lane
        for half in range(2):
            sh = sb[:, half * LANES:(half + 1) * LANES]
            ch = col + half * LANES
            sel = ((sh > ans_r) | ((sh == ans_r) & (ch <= thr_r))) & (sh > -jnp.inf)
            selb_ref[:, pl.ds(pl.multiple_of(j * TK + half * LANES, LANES), LANES)] = jnp.where(sel, 0.0, NEG)
        return c

    lax.fori_loop(0, nkb, selb_blk, 0)

    for h in range(H_A):
        mx_ref[h] = jnp.full((TQ, LANES), NEG, F32)
    acc_ref[...] = jnp.zeros_like(acc_ref)

    def pass_a(j, bias_of_head):
        kb = akv_ref[0, blk(j), 0:LANES]
        sb = selb_ref[:, blk(j)]
        for h in range(H_A):
            s = lax.dot_general(qa_ref[h * TQ:(h + 1) * TQ, :], kb, NT_DIMS, preferred_element_type=F32)
            lg = s + sb + bias_of_head(h)
            lg_ref[h, :, blk(j)] = lg
            mx_ref[h] = jnp.maximum(mx_ref[h], jnp.maximum(lg[:, :LANES], lg[:, LANES:]))

    _for_blocks(jnp.maximum(i - 1, 0), lambda j, slot: pass_a(j, lambda h: rb_ref[far_bucket, h] * LOG2E))

    @pl.when(i >= 1)
    def _():
        pass_a(i - 1, lambda h: bias_ref[0, h])

    pass_a(i, lambda h: bias_ref[1, h])

    for h in range(H_A):
        mx_ref[h] = jnp.broadcast_to(jnp.max(mx_ref[h], axis=1, keepdims=True), (TQ, LANES))

    half = (H_A // 2) * TQ

    def pass_b(j, slot):
        for h in range(H_A):
            lg = lg_ref[h, :, blk(j)]
            m = mx_ref[h]
            p_ref[slot, h * TQ:(h + 1) * TQ, 0:LANES] = jnp.exp2(lg[:, :LANES] - m).astype(BF16)
            p_ref[slot, h * TQ:(h + 1) * TQ, LANES:TK] = jnp.exp2(lg[:, LANES:] - m).astype(BF16)
        for g in range(HKV_A):
            acc_ref[g * half:(g + 1) * half, :] += jnp.dot(
                p_ref[slot, g * half:(g + 1) * half, :], akv_ref[0, blk(j), (g + 1) * LANES:(g + 2) * LANES],
                preferred_element_type=F32)

    _for_blocks(nkb, pass_b)

    for p in range(4):
        out_ref[0, :, LANES * p:LANES * (p + 1)] = _normalize_pair(
            acc_ref[p * TQ:(p + 1) * TQ, :], acc_ref[(p + 4) * TQ:(p + 5) * TQ, :], low).astype(BF16)


def _dsa_prompt(rel_bias, aq, iq, iw, ik2, akv, bias_tiles, far_bucket):
    b, t, _ = aq.shape
    k_top = min(K_TOP_MAX, t // 4)
    qrow = lambda c: pl.BlockSpec((1, TQ, c), lambda i, j: (i, j, 0))
    full = lambda c: pl.BlockSpec((1, t, c), lambda i, j: (i, 0, 0))
    kern = functools.partial(_dsa_prompt_kernel, k_top=k_top, far_bucket=far_bucket, seq_len=t)
    return pl.pallas_call(
        kern,
        grid=(b, t // TQ),
        in_specs=[pl.BlockSpec(memory_space=pltpu.SMEM),
                  qrow(512), qrow(512), qrow(8), full(128), full(384),
                  _resident((2, H_A, TQ, TK))],
        out_specs=qrow(512),
        out_shape=jax.ShapeDtypeStruct((b, t, 512), BF16),
        scratch_shapes=[
            pltpu.VMEM((H_IDX * TQ, LANES), BF16),
            pltpu.VMEM((H_A * TQ, LANES), BF16),
            pltpu.VMEM((H_IDX, TQ, TK), F32),
            pltpu.VMEM((TQ, t), F32),
            pltpu.VMEM((t, TQ), F32),
            pltpu.VMEM((TQ, t), F32),
            pltpu.VMEM((H_A, TQ, t), F32),
            pltpu.VMEM((H_A, TQ, LANES), F32),
            pltpu.VMEM((2, H_A * TQ, TK), BF16),
            pltpu.VMEM((H_A * TQ, LANES), F32),
            pltpu.VMEM((SUBLANES, TQ), I32),
            pltpu.VMEM((SUBLANES, TQ), F32),
        ],
        compiler_params=_cparams("parallel", "parallel"),
        name="dsa_prompt",
    )(rel_bias, aq, iq, iw, ik2, akv, bias_tiles)


def _fox_prompt_kernel(bq_ref, bk_ref, bv_ref, fkt_ref, out_ref,
                       qb_ref, fqrep_ref, lg_ref, mx_ref, p_ref, acc_ref):
    i = pl.program_id(1)
    lane = lax.broadcasted_iota(I32, (TQ, LANES), 1)
    low = lane < HEAD_DIM

    for p in range(4):
        s_b = bq_ref[0, :, LANES * p:LANES * (p + 1)].astype(F32)
        qb_ref[(2 * p) * TQ:(2 * p + 1) * TQ, :] = jnp.where(low, s_b, 0.0).astype(BF16)
        qb_ref[(2 * p + 1) * TQ:(2 * p + 2) * TQ, :] = jnp.where(low, 0.0, s_b).astype(BF16)
    fq_t = fkt_ref[0, :, pl.ds(pl.multiple_of(i * TQ, TQ), TQ)]
    for h in range(H_B):
        fqrep_ref[h] = jnp.broadcast_to(fq_t[h:h + 1, :], (LANES, TQ)).T
        mx_ref[h] = jnp.full((TQ, LANES), NEG, F32)
    acc_ref[...] = jnp.zeros_like(acc_ref)

    def blk(j):
        return pl.ds(pl.multiple_of(j * TK, TK), TK)

    r2 = lax.broadcasted_iota(I32, (TQ, LANES), 0)

    def pass_a(j, diagonal):
        for p in range(4):
            s = lax.dot_general(qb_ref[(2 * p) * TQ:(2 * p + 2) * TQ, :],
                                bk_ref[0, blk(j), LANES * p:LANES * (p + 1)],
                                NT_DIMS, preferred_element_type=F32)
            for e in range(2):
                h = 2 * p + e
                fk = fkt_ref[0, h:h + 1, blk(j)]
                halves = []
                for half in range(2):
                    lg = (s[e * TQ:(e + 1) * TQ, half * LANES:(half + 1) * LANES]
                          + (fqrep_ref[h] - fk[:, half * LANES:(half + 1) * LANES]))
                    if diagonal:
                        lg = jnp.where(lane + half * LANES <= r2, lg, NEG)
                    lg_ref[h, :, pl.ds(pl.multiple_of(j * TK + half * LANES, LANES), LANES)] = lg
                    halves.append(lg)
                mx_ref[h] = jnp.maximum(mx_ref[h], jnp.maximum(halves[0], halves[1]))

    _for_blocks(i, lambda j, slot: pass_a(j, False))
    pass_a(i, True)

    for h in range(H_B):
        mx_ref[h] = jnp.broadcast_to(jnp.max(mx_ref[h], axis=1, keepdims=True), (TQ, LANES))

    def pass_b(j, slot):
        for h in range(H_B):
            lg = lg_ref[h, :, blk(j)]
            m = mx_ref[h]
            p_ref[slot, h * TQ:(h + 1) * TQ, 0:LANES] = jnp.exp2(lg[:, :LANES] - m).astype(BF16)
            p_ref[slot, h * TQ:(h + 1) * TQ, LANES:TK] = jnp.exp2(lg[:, LANES:] - m).astype(BF16)
            acc_ref[h * TQ:(h + 1) * TQ, :] += jnp.dot(
                p_ref[slot, h * TQ:(h + 1) * TQ, :], bv_ref[0, blk(j), LANES * h:LANES * (h + 1)],
                preferred_element_type=F32)

    _for_blocks(i + 1, pass_b)

    for p in range(4):
        out_ref[0, :, LANES * p:LANES * (p + 1)] = _normalize_pair(
            acc_ref[(2 * p) * TQ:(2 * p + 1) * TQ, :], acc_ref[(2 * p + 1) * TQ:(2 * p + 2) * TQ, :], low).astype(BF16)


def _fox_prompt(bq, bk, bv, fkt):
    b, t, _ = bq.shape
    qrow = lambda c: pl.BlockSpec((1, TQ, c), lambda i, j: (i, j, 0))
    full = lambda c: pl.BlockSpec((1, t, c), lambda i, j: (i, 0, 0))
    return pl.pallas_call(
        _fox_prompt_kernel,
        grid=(b, t // TQ),
        in_specs=[qrow(512), full(512), full(1024),
                  pl.BlockSpec((1, H_B, t), lambda i, j: (i, 0, 0))],
        out_specs=qrow(512),
        out_shape=jax.ShapeDtypeStruct((b, t, 512), BF16),
        scratch_shapes=[
            pltpu.VMEM((H_B * TQ, LANES), BF16),
            pltpu.VMEM((H_B, TQ, LANES), F32),
            pltpu.VMEM((H_B, TQ, t), F32),
            pltpu.VMEM((H_B, TQ, LANES), F32),
            pltpu.VMEM((2, H_B * TQ, TK), BF16),
            pltpu.VMEM((H_B * TQ, LANES), F32),
        ],
        compiler_params=_cparams("parallel", "parallel"),
        name="fox_prompt",
    )(bq, bk, bv, fkt)


FF_BLK = 1024


def _tail_kernel(x_ref, oa_ref, ob_ref, woa_ref, wob_ref, gm_ref, wup_ref, wdn_ref, gf_ref, y_ref):
    mixed = (jnp.dot(oa_ref[...], woa_ref[...], preferred_element_type=F32)
             + jnp.dot(ob_ref[...], wob_ref[...], preferred_element_type=F32))
    x1 = x_ref[...] + mixed
    h2 = _rms(x1, gm_ref[...]).astype(BF16)
    acc = x1
    for c in range(D_FF // FF_BLK):
        u = jnp.dot(h2, wup_ref[:, c * FF_BLK:(c + 1) * FF_BLK], preferred_element_type=F32)
        u = jnp.square(jnp.maximum(u, 0.0)).astype(BF16)
        acc = acc + jnp.dot(u, wdn_ref[c * FF_BLK:(c + 1) * FF_BLK, :], preferred_element_type=F32)
    y_ref[...] = _rms(acc, gf_ref[...])


def _tail(x, oa, ob, woa, wob, g_mlp, w_up, w_down, g_final, tm):
    n, d = x.shape
    row = lambda c: pl.BlockSpec((tm, c), lambda i: (i, 0))
    return pl.pallas_call(
        _tail_kernel,
        grid=(n // tm,),
        in_specs=[row(d), row(512), row(512), _resident((512, d)), _resident((512, d)), _resident((1, d)),
                  _resident((d, D_FF)), _resident((D_FF, d)), _resident((1, d))],
        out_specs=row(d),
        out_shape=jax.ShapeDtypeStruct((n, d), F32),
        compiler_params=_cparams("parallel"),
        name="tail",
    )(x, oa, ob, woa, wob, g_mlp, w_up, w_down, g_final)


def _dsa_sample_kernel(aq_ref, iq_ref, iw_ref, ik2_ref, akv_ref, cikt_ref, cakt_ref, cavt_ref, bias_ref, out_ref,
                       *, past, n_new, l_pad, k_top, group):
    l_all = past + n_new
    n_tail = l_pad - past
    lane = lax.broadcasted_iota(I32, (n_new, LANES), 1)
    low = lane < HEAD_DIM

    def pad_rows(x):
        return jnp.concatenate([x, jnp.zeros((n_tail - n_new, x.shape[1]), F32)], axis=0).astype(BF16)

    scores = []
    for g in range(group):
        ik_new = pad_rows(ik2_ref[g].astype(F32)[:, 0:D_IDX])
        iq32 = iq_ref[g].astype(F32)
        qi = jnp.concatenate([iq32[:, D_IDX * h:D_IDX * (h + 1)] for h in range(H_IDX)], axis=0).astype(BF16)
        d = jnp.concatenate(
            [jnp.dot(qi, cikt_ref[g].astype(BF16), preferred_element_type=F32),
             lax.dot_general(qi, ik_new, NT_DIMS, preferred_element_type=F32)], axis=1)
        iw = iw_ref[g]
        score = jnp.zeros((n_new, l_pad), F32)
        for h in range(H_IDX):
            score = score + iw[:, h:h + 1] * jnp.maximum(d[h * n_new:(h + 1) * n_new, :], 0.0)
        scores.append(score)
    score = jnp.concatenate(scores, axis=0)

    rows = group * n_new
    row = lax.broadcasted_iota(I32, (rows, l_pad), 0) % n_new
    col = lax.broadcasted_iota(I32, (rows, l_pad), 1)
    adm = (col < l_all) & ((col // CHUNK) <= ((past + row) // CHUNK))
    sc = jnp.where(adm, score, -jnp.inf)

    def count(mask):
        return jnp.sum(jnp.where(mask, 1.0, 0.0), axis=1, keepdims=True)

    kf = float(k_top)
    ans_key = jnp.where(count(sc >= 0.0) >= kf, 0, INT_MIN).astype(I32)

    def bit_body(t, ans_key):
        cand_key = ans_key | lax.shift_left(jnp.int32(1), 30 - t)
        return jnp.where(count(sc >= _key_to_float(cand_key)) >= kf, cand_key, ans_key)

    ans_key = lax.fori_loop(0, 31, bit_body, ans_key)
    ans = jnp.where(ans_key > KEY_NEG_INF, _key_to_float(ans_key), -jnp.inf)
    eq = sc == ans
    need = kf - count(sc > ans)
    tie = (count(eq) > need) & (ans > -jnp.inf)
    flag = jnp.max(jnp.where(tie, 1.0, 0.0), axis=0, keepdims=True)
    nbits = int(l_pad - 1).bit_length()

    def tie_search(_):
        def tie_body(t, m):
            cand = m | lax.shift_left(jnp.int32(1), nbits - 1 - t)
            return jnp.where(count(eq & (col < cand)) < need, cand, m)
        m = lax.fori_loop(0, nbits, tie_body, jnp.zeros((rows, 1), I32))
        return jnp.where(tie, m, l_pad)

    thr = lax.cond(flag[0, 0] > 0.0, tie_search, lambda _: jnp.full((rows, 1), l_pad, I32), 0)
    sel = ((sc > ans) | (eq & (col <= thr))) & (sc > -jnp.inf)
    selb_all = jnp.where(sel, 0.0, NEG)

    for g in range(group):
        akv = akv_ref[g].astype(F32)
        ak_new = pad_rows(akv[:, 0:LANES])
        av_new = pad_rows(jnp.where(low, akv[:, LANES:2 * LANES], akv[:, 2 * LANES:3 * LANES]))
        selb = selb_all[g * n_new:(g + 1) * n_new, :]
        aq32 = aq_ref[g].astype(F32)
        slabs = [aq32[:, LANES * p:LANES * (p + 1)] for p in range(4)]
        qa = jnp.concatenate([jnp.where(low, s, 0.0) for s in slabs] + [jnp.where(low, 0.0, s) for s in slabs],
                             axis=0).astype(BF16)
        s = jnp.concatenate(
            [jnp.dot(qa, cakt_ref[g].astype(BF16), preferred_element_type=F32),
             lax.dot_general(qa, ak_new, NT_DIMS, preferred_element_type=F32)], axis=1)
        lg = s + bias_ref[...] + jnp.concatenate([selb] * H_A, axis=0)
        m = jnp.max(lg, axis=1, keepdims=True)
        p = jnp.exp2(lg - m)
        l = jnp.sum(p, axis=1, keepdims=True)
        pb = p.astype(BF16)
        o = (lax.dot_general(pb[:, 0:past], cavt_ref[g].astype(BF16), NT_DIMS, preferred_element_type=F32)
             + jnp.dot(pb[:, past:l_pad], av_new, preferred_element_type=F32)) / l
        for q in range(4):
            out_ref[g, :, LANES * q:LANES * (q + 1)] = jnp.where(
                low, o[q * n_new:(q + 1) * n_new, :], o[(q + 4) * n_new:(q + 5) * n_new, :]).astype(BF16)


SAMPLE_GROUP = 4


def _dsa_sample(aq, iq, iw, ik2, akv, cikt, cakt, cavt, bias_rows, past):
    b, n_new, _ = aq.shape
    l_pad = bias_rows.shape[1]
    assert past % LANES == 0 and l_pad == past + LANES and n_new <= LANES and b % SAMPLE_GROUP == 0
    k_top = min(K_TOP_MAX, (past + n_new) // 4)
    new = lambda c: pl.BlockSpec((SAMPLE_GROUP, n_new, c), lambda i: (i, 0, 0))
    old = lambda c: pl.BlockSpec((SAMPLE_GROUP, c, past), lambda i: (i, 0, 0))
    kern = functools.partial(_dsa_sample_kernel, past=past, n_new=n_new, l_pad=l_pad, k_top=k_top,
                             group=SAMPLE_GROUP)
    return pl.pallas_call(
        kern,
        grid=(b // SAMPLE_GROUP,),
        in_specs=[new(512), new(512), new(8), new(128), new(384), old(D_IDX), old(LANES), old(LANES),
                  _resident((H_A * n_new, l_pad))],
        out_specs=new(512),
        out_shape=jax.ShapeDtypeStruct((b, n_new, 512), BF16),
        compiler_params=_cparams("parallel"),
        name="dsa_sample",
    )(aq, iq, iw, ik2, akv, cikt, cakt, cavt, bias_rows)


def _fox_sample_kernel(bq_ref, bkn_ref, bvn_ref, ckt_ref, cvt_ref, fkt_ref, tot_ref, lf_ref, lft_ref, out_ref,
                       *, past, n_new):
    lf = lf_ref[0] * LOG2E
    lft = lft_ref[0] * LOG2E
    rown = lax.broadcasted_iota(I32, (n_new, H_B), 0)
    lanen = lax.broadcasted_iota(I32, (H_B, n_new), 1)
    fq = jnp.broadcast_to(tot_ref[0], (n_new, H_B))
    fqt = jnp.broadcast_to(fkt_ref[0][:, past - 1:past], (H_B, n_new))
    for s in range(n_new):
        fq = fq + jnp.where(rown >= s, lf[s:s + 1, :], 0.0)
        fqt = fqt + jnp.where(lanen >= s, lft[:, s:s + 1], 0.0)

    kct = ckt_ref[0].astype(BF16)
    vct = cvt_ref[0].astype(BF16)
    bq32 = bq_ref[0].astype(F32)
    head_of_lane = lax.broadcasted_iota(I32, (n_new, H_B * HEAD_DIM), 1) // HEAD_DIM
    qb = jnp.concatenate([jnp.where(head_of_lane == h, bq32, 0.0) for h in range(H_B)], axis=0).astype(BF16)
    s_past = jnp.dot(qb, kct, preferred_element_type=F32)
    s_new = lax.dot_general(qb, bkn_ref[0], NT_DIMS, preferred_element_type=F32)
    fkt = fkt_ref[0]
    fq_col = jnp.concatenate([fq[:, h:h + 1] for h in range(H_B)], axis=0)
    fk_past = jnp.concatenate([jnp.broadcast_to(fkt[h:h + 1, :], (n_new, past)) for h in range(H_B)], axis=0)
    fk_new = jnp.concatenate([jnp.broadcast_to(fqt[h:h + 1, :], (n_new, n_new)) for h in range(H_B)], axis=0)
    lg_past = s_past + (fq_col - fk_past)
    trow = lax.broadcasted_iota(I32, (H_B * n_new, n_new), 0) % n_new
    tcol = lax.broadcasted_iota(I32, (H_B * n_new, n_new), 1)
    lg_new = jnp.where(tcol <= trow, s_new + (fq_col - fk_new), NEG)
    m = jnp.maximum(jnp.max(lg_past, axis=1, keepdims=True), jnp.max(lg_new, axis=1, keepdims=True))
    p_past = jnp.exp2(lg_past - m)
    p_new = jnp.exp2(lg_new - m)
    l = jnp.sum(p_past, axis=1, keepdims=True) + jnp.sum(p_new, axis=1, keepdims=True)
    o = (lax.dot_general(p_past.astype(BF16), vct, NT_DIMS, preferred_element_type=F32)
         + jnp.dot(p_new.astype(BF16), bvn_ref[0].astype(BF16), preferred_element_type=F32)) / l
    out = jnp.zeros((n_new, H_B * HEAD_DIM), F32)
    for h in range(H_B):
        out = out + jnp.where(head_of_lane == h, o[h * n_new:(h + 1) * n_new, :], 0.0)
    out_ref[0] = out.astype(BF16)


def _fox_sample(bq, bkn, bvn, ckt, cvt, fkt, tot, lf, lft, past):
    b, n_new, _ = bq.shape
    new = lambda c: pl.BlockSpec((1, n_new, c), lambda i: (i, 0, 0))
    old = pl.BlockSpec((1, H_B * HEAD_DIM, past), lambda i: (i, 0, 0))
    kern = functools.partial(_fox_sample_kernel, past=past, n_new=n_new)
    return pl.pallas_call(
        kern,
        grid=(b,),
        in_specs=[new(512), new(512), new(512), old, old,
                  pl.BlockSpec((1, H_B, past), lambda i: (i, 0, 0)),
                  pl.BlockSpec((1, 1, H_B), lambda i: (i, 0, 0)),
                  new(H_B),
                  pl.BlockSpec((1, H_B, n_new), lambda i: (i, 0, 0))],
        out_specs=new(512),
        out_shape=jax.ShapeDtypeStruct((b, n_new, 512), BF16),
        compiler_params=_cparams("parallel"),
        name="fox_sample",
    )(bq, bkn, bvn, ckt, cvt, fkt, tot, lf, lft)


def _prep_weights(w_in, w_o, w_up, w_down):
    offs = np.concatenate([[0], np.cumsum(IN_SPLITS)])
    seg = lambda k: w_in[:, int(offs[k]):int(offs[k + 1])]
    a_q, a_k, a_v, i_q, i_k, i_w, b_q, b_k, b_v, f_z = (seg(k) for k in range(10))
    a_q_pairs = jnp.concatenate([a_q[:, HEAD_DIM * h:HEAD_DIM * (h + 1)] for h in AQ_HEAD_ORDER], axis=1)
    pad = jnp.zeros((w_in.shape[0], N_CAT - C_SM - 16), w_in.dtype)
    w_cat = jnp.concatenate([a_q_pairs, i_q, b_q, a_k, a_v, b_k, b_v, i_k, i_k, i_w, f_z, pad], axis=1)
    wo_a = jnp.concatenate([w_o[HEAD_DIM * h:HEAD_DIM * (h + 1)] for h in AQ_HEAD_ORDER], axis=0)
    wo_b = w_o[H_A * HEAD_DIM:]
    return (w_cat.astype(BF16), wo_a.astype(BF16), wo_b.astype(BF16), w_up.astype(BF16), w_down.astype(BF16))


def kernel(x_prompt, x_sample, cache_a_k, cache_a_v, cache_idx_k, cache_b_k, cache_b_v, cache_b_logf,
           w_in, w_o, b_f, rel_bias, g_attn, w_up, w_down, g_mlp, g_final):
    assert w_in.shape[0] == 1, "single-layer trunk"
    bp, tp, d = x_prompt.shape
    bs, ts, _ = x_sample.shape
    past = cache_a_k.shape[2]
    w_cat, wo_a, wo_b, wup, wdn = _prep_weights(w_in[0], w_o[0], w_up[0], w_down[0])
    g_a = g_attn[0].reshape(1, d)
    g_m = g_mlp[0].reshape(1, d)
    g_f = g_final.reshape(1, d)
    bf = b_f[0].reshape(H_B, 1)
    rel_bias = rel_bias.astype(F32)

    (aq, iq, bq, akv, bk, bv, ik2, ak32, av32, bk32, bv32, iw, ikt32, logft) = _inproj(x_prompt, g_a, w_cat, bf, 512)
    fkt = _cumsum_lanes(logft)
    r = np.arange(TQ)[:, None]
    c = np.arange(TK)[None, :]
    bucket = _t5_bucket_np(np.stack([c - TK - r, c - r]))
    far_bucket = int(_t5_bucket_np(np.array(-TK - 1)))
    assert far_bucket == int(_t5_bucket_np(np.array(-tp)))
    bias_tiles = _bias_tiles(rel_bias, jnp.asarray(bucket))
    out_a = _dsa_prompt(rel_bias, aq, iq, iw, ik2, akv, bias_tiles, far_bucket)
    out_b = _fox_prompt(bq, bk, bv, fkt)
    y_p = _tail(x_prompt.reshape(bp * tp, d), out_a.reshape(bp * tp, 512), out_b.reshape(bp * tp, 512),
                wo_a, wo_b, g_m, wup, wdn, g_f, 512).reshape(bp, tp, d)

    n_s = bs * ts
    outs = _inproj(x_sample.reshape(1, n_s, d), g_a, w_cat, bf, n_s)
    (aq_s, iq_s, bq_s, akv_s, bk_s, bv_s, ik2_s, ak32_s, av32_s, bk32_s, bv32_s, iw_s) = (
        o.reshape(bs, ts, o.shape[-1]) for o in outs[:12])
    ik32_s = jnp.swapaxes(outs[12][0], 0, 1).reshape(bs, ts, D_IDX)
    logf_s = jnp.swapaxes(outs[13][0], 0, 1).reshape(bs, ts, H_B)
    l_all = past + ts
    l_pad = -(-l_all // LANES) * LANES
    rel_s = np.arange(l_pad)[None, :] - (past + np.arange(ts))[:, None]
    bias_s = _bias_tiles(rel_bias, jnp.asarray(_t5_bucket_np(rel_s))[None])[0].reshape(H_A * ts, l_pad)
    def feature_major(cache):
        c = cache[0]
        c = jnp.transpose(c, (0, 2, 3, 1)) if c.ndim == 4 else jnp.transpose(c, (0, 2, 1))
        return c.reshape(bs, -1, past)

    out_a_s = _dsa_sample(aq_s, iq_s, iw_s, ik2_s, akv_s, feature_major(cache_idx_k),
                          feature_major(cache_a_k), feature_major(cache_a_v), bias_s, past)
    fkt_c = _cumsum_lanes(jnp.swapaxes(cache_b_logf[0].astype(F32), 1, 2))
    tot = fkt_c[:, :, past - 1].reshape(bs, 1, H_B)
    out_b_s = _fox_sample(bq_s, bk_s, bv32_s, feature_major(cache_b_k), feature_major(cache_b_v), fkt_c, tot,
                          logf_s, jnp.swapaxes(logf_s, 1, 2), past)
    y_s = _tail(x_sample.reshape(n_s, d), out_a_s.reshape(n_s, 512), out_b_s.reshape(n_s, 512),
                wo_a, wo_b, g_m, wup, wdn, g_f, n_s).reshape(bs, ts, d)

    def rows(a, heads, b, t):
        return a.reshape(1, b, t, heads, HEAD_DIM)

    return (y_p, y_s,
            rows(ak32, HKV_A, bp, tp), rows(av32, HKV_A, bp, tp), jnp.swapaxes(ikt32, 1, 2)[None],
            rows(bk32, H_B, bp, tp), rows(bv32, H_B, bp, tp), jnp.swapaxes(logft, 1, 2)[None],
            rows(ak32_s, HKV_A, bs, ts), rows(av32_s, HKV_A, bs, ts), ik32_s.reshape(1, bs, ts, D_IDX),
            rows(bk32_s, H_B, bs, ts), rows(bv32_s, H_B, bs, ts), logf_s.reshape(1, bs, ts, H_B))
```

```python
import functools
import math

import numpy as np
import jax
import jax.numpy as jnp
from jax import lax
from jax.experimental import pallas as pl
from jax.experimental.pallas import tpu as pltpu

F32 = jnp.float32
BF16 = jnp.bfloat16
I32 = jnp.int32

D_MODEL = 1024
CHUNK = 64
HEAD_DIM = 64
H_A = 8
HKV_A = 2
H_B = 8
H_IDX = 8
D_IDX = 64
K_TOP_MAX = 256
NUM_BUCKETS = 32
MAX_DISTANCE = 128
D_FF = 4 * D_MODEL
EPS = 1e-6
IN_SPLITS = (H_A * HEAD_DIM, HKV_A * HEAD_DIM, HKV_A * HEAD_DIM, H_IDX * D_IDX, D_IDX, H_IDX,
             H_B * HEAD_DIM, H_B * HEAD_DIM, H_B * HEAD_DIM, H_B)

LANES = 128
SUBLANES = 8
VMEM_LIMIT = 56 * 1024 * 1024

TQ = 256
TK = 256
NEG = -1e30
LOG2E = 1.4426950408889634
KEY_NEG_INF = -2139095041
INT_MIN = -2147483648

C_AQ, C_IQ, C_BQ, C_AKV, C_BK, C_BV, C_IK2, C_SM, N_CAT = 0, 512, 1024, 1536, 1792, 2304, 2816, 2944, 3072
AQ_HEAD_ORDER = (0, 4, 1, 5, 2, 6, 3, 7)

NT_DIMS = (((1,), (1,)), ((), ()))


def _cparams(*sem):
    return pltpu.CompilerParams(dimension_semantics=sem, vmem_limit_bytes=VMEM_LIMIT)


def _resident(shape):
    nd = len(shape)
    return pl.BlockSpec(shape, lambda *_: (0,) * nd, pipeline_mode=pl.Buffered(1))


def _rms(x, g):
    ms = jnp.mean(x * x, axis=-1, keepdims=True)
    return (x * lax.rsqrt(ms + EPS)) * g


def _for_blocks(n, body):
    def pair(k, c):
        body(2 * k, 0)
        body(2 * k + 1, 1)
        return c

    lax.fori_loop(0, n // 2, pair, 0)

    @pl.when(n % 2 == 1)
    def _():
        body(n - 1, 0)


def _normalize_pair(a_lo, a_hi, low):
    num = jnp.where(low, a_lo, a_hi)
    den = pltpu.roll(jnp.where(low, a_hi, a_lo), HEAD_DIM, axis=1)
    return num / den


def _key_to_float(k):
    return lax.bitcast_convert_type(k ^ (lax.shift_right_arithmetic(k, 31) & 0x7FFFFFFF), F32)


def _t5_bucket_np(rel):
    half = NUM_BUCKETS // 2
    max_exact = half // 2
    ret = np.where(rel > 0, half, 0)
    n = np.abs(rel)
    n_f = np.maximum(n, max_exact).astype(np.float64)
    large = max_exact + (np.log(n_f / max_exact) / math.log(MAX_DISTANCE / max_exact)
                         * (half - max_exact)).astype(np.int32)
    large = np.minimum(large, half - 1)
    return (ret + np.where(n < max_exact, n, large)).astype(np.int32)


def _inproj_kernel(x_ref, g_ref, w_ref, bf_ref,
                   aq_ref, iq_ref, bq_ref, akv_ref, bk_ref, bv_ref, ik2_ref,
                   ak32_ref, av32_ref, bk32_ref, bv32_ref, iw_ref, ikt_ref, logft_ref):
    h = _rms(x_ref[0], g_ref[...])
    p = jnp.dot(h.astype(BF16), w_ref[...], preferred_element_type=F32)
    qscale = HEAD_DIM ** -0.5 * LOG2E
    aq_ref[0] = (p[:, C_AQ:C_AQ + 512] * qscale).astype(BF16)
    iq_ref[0] = (p[:, C_IQ:C_IQ + 512] * (D_IDX ** -0.5)).astype(BF16)
    bq_ref[0] = (p[:, C_BQ:C_BQ + 512] * qscale).astype(BF16)
    low = lax.broadcasted_iota(I32, (p.shape[0], LANES), 1) < HEAD_DIM
    av = p[:, C_AKV + LANES:C_AKV + 2 * LANES]
    akv_ref[0, :, 0:LANES] = p[:, C_AKV:C_AKV + LANES].astype(BF16)
    akv_ref[0, :, LANES:2 * LANES] = jnp.where(low, av, 1.0).astype(BF16)
    akv_ref[0, :, 2 * LANES:3 * LANES] = jnp.where(low, 1.0, av).astype(BF16)
    bk_ref[0] = p[:, C_BK:C_BK + 512].astype(BF16)
    for q in range(4):
        bv = p[:, C_BV + LANES * q:C_BV + LANES * (q + 1)]
        bv_ref[0, :, 2 * q * LANES:(2 * q + 1) * LANES] = jnp.where(low, bv, 1.0).astype(BF16)
        bv_ref[0, :, (2 * q + 1) * LANES:(2 * q + 2) * LANES] = jnp.where(low, 1.0, bv).astype(BF16)
    ik2_ref[0] = p[:, C_IK2:C_IK2 + 128].astype(BF16)
    ak32_ref[0] = p[:, C_AKV:C_AKV + 128]
    av32_ref[0] = p[:, C_AKV + 128:C_AKV + 256]
    bk32_ref[0] = p[:, C_BK:C_BK + 512]
    bv32_ref[0] = p[:, C_BV:C_BV + 512]
    iw_ref[0] = p[:, C_SM:C_SM + 8] * (H_IDX ** -0.5)
    ikt_ref[0] = p[:, C_IK2:C_IK2 + LANES].T[0:D_IDX, :]
    z = p[:, C_SM:C_SM + LANES].T[8:16, :] + bf_ref[...]
    logft_ref[0] = jnp.minimum(z, 0.0) - jnp.log1p(jnp.exp(-jnp.abs(z)))


def _inproj(x, g, w_cat, b_f, tm):
    b, t, d = x.shape
    grid = (b, t // tm)
    row = lambda c: pl.BlockSpec((1, tm, c), lambda i, j: (i, j, 0))
    col = lambda c: pl.BlockSpec((1, c, tm), lambda i, j: (i, 0, j))
    outs = [(512, BF16), (512, BF16), (512, BF16), (384, BF16), (512, BF16), (1024, BF16), (128, BF16),
            (128, F32), (128, F32), (512, F32), (512, F32), (8, F32)]
    outs_t = [D_IDX, H_B]
    return pl.pallas_call(
        _inproj_kernel,
        grid=grid,
        in_specs=[row(d), _resident((1, d)), _resident((d, N_CAT)), _resident((H_B, 1))],
        out_specs=[row(c) for c, _ in outs] + [col(c) for c in outs_t],
        out_shape=([jax.ShapeDtypeStruct((b, t, c), dt) for c, dt in outs]
                   + [jax.ShapeDtypeStruct((b, c, t), F32) for c in outs_t]),
        compiler_params=_cparams("parallel", "parallel"),
        name="inproj",
    )(x, g, w_cat, b_f)


CS_BLK = 512


def _cumsum_kernel(x_ref, out_ref, carry_ref):
    @pl.when(pl.program_id(0) == 0)
    def _():
        carry_ref[...] = jnp.zeros_like(carry_ref)

    x = x_ref[...] * LOG2E
    hi = x.astype(BF16)
    r1 = x - hi.astype(F32)
    mid = r1.astype(BF16)
    lo = (r1 - mid.astype(F32)).astype(BF16)
    src = lax.broadcasted_iota(I32, (CS_BLK, CS_BLK), 0)
    dst = lax.broadcasted_iota(I32, (CS_BLK, CS_BLK), 1)
    tri = jnp.where(src <= dst, 1.0, 0.0).astype(BF16)
    cs = (jnp.dot(hi, tri, preferred_element_type=F32)
          + jnp.dot(mid, tri, preferred_element_type=F32)
          + jnp.dot(lo, tri, preferred_element_type=F32))
    out = cs + carry_ref[...]
    out_ref[...] = out
    carry_ref[...] = jnp.broadcast_to(out[:, CS_BLK - 1:CS_BLK], carry_ref.shape)


def _cumsum_lanes(x_t):
    b, h, l = x_t.shape
    spec = pl.BlockSpec((b * h, CS_BLK), lambda j: (0, j))
    return pl.pallas_call(
        _cumsum_kernel,
        grid=(l // CS_BLK,),
        in_specs=[spec],
        out_specs=spec,
        out_shape=jax.ShapeDtypeStruct((b * h, l), F32),
        scratch_shapes=[pltpu.VMEM((b * h, CS_BLK), F32)],
        compiler_params=_cparams("arbitrary"),
        name="cumsum",
    )(x_t.reshape(b * h, l)).reshape(b, h, l)


def _bias_kernel(rb_ref, bucket_ref, out_ref):
    b = bucket_ref[0]
    for h in range(H_A):
        acc = jnp.zeros(b.shape, F32)
        for k in range(NUM_BUCKETS):
            acc = jnp.where(b == k, rb_ref[k, h] * LOG2E, acc)
        out_ref[0, h] = acc


def _bias_tiles(rel_bias, bucket):
    n, r, c = bucket.shape
    return pl.pallas_call(
        _bias_kernel,
        grid=(n,),
        in_specs=[pl.BlockSpec(memory_space=pltpu.SMEM),
                  pl.BlockSpec((1, r, c), lambda i: (i, 0, 0))],
        out_specs=pl.BlockSpec((1, H_A, r, c), lambda i: (i, 0, 0, 0)),
        out_shape=jax.ShapeDtypeStruct((n, H_A, r, c), F32),
        compiler_params=_cparams("parallel"),
        name="t5_bias",
    )(rel_bias, bucket)


def _dsa_prompt_kernel(rb_ref, aq_ref, iq_ref, iw_ref, ik2_ref, akv_ref, bias_ref, out_ref,
                       qi_ref, qa_ref, iwrep_ref, sc_ref, sct_ref, selb_ref, lg_ref, mx_ref,
                       acc_ref, thr_ref, ans_ref, *, k_top, far_bucket, seq_len):
    i = pl.program_id(1)
    nkb = i + 1
    lane = lax.broadcasted_iota(I32, (TQ, LANES), 1)
    low = lane < HEAD_DIM

    for p in range(4):
        s_i = iq_ref[0, :, LANES * p:LANES * (p + 1)].astype(F32)
        qi_ref[(2 * p) * TQ:(2 * p + 1) * TQ, :] = jnp.where(low, s_i, 0.0).astype(BF16)
        qi_ref[(2 * p + 1) * TQ:(2 * p + 2) * TQ, :] = jnp.where(low, 0.0, s_i).astype(BF16)
        s_a = aq_ref[0, :, LANES * p:LANES * (p + 1)].astype(F32)
        qa_ref[p * TQ:(p + 1) * TQ, :] = jnp.where(low, s_a, 0.0).astype(BF16)
        qa_ref[(p + 4) * TQ:(p + 5) * TQ, :] = jnp.where(low, 0.0, s_a).astype(BF16)
    iw = iw_ref[0]
    for h in range(H_IDX):
        iwrep_ref[h] = jnp.broadcast_to(iw[:, h:h + 1], (TQ, TK))

    def blk(j):
        return pl.ds(pl.multiple_of(j * TK, TK), TK)

    r2 = lax.broadcasted_iota(I32, (TQ, TK), 0)
    c2 = lax.broadcasted_iota(I32, (TQ, TK), 1)

    def score_blk(j, slot, diagonal=False):
        kb = ik2_ref[0, blk(j), :]
        acc = jnp.zeros((TQ, TK), F32)
        for h in range(H_IDX):
            d = lax.dot_general(qi_ref[h * TQ:(h + 1) * TQ, :], kb, NT_DIMS, preferred_element_type=F32)
            acc = acc + iwrep_ref[h] * jnp.maximum(d, 0.0)
        if diagonal:
            acc = jnp.where((c2 // CHUNK) <= (r2 // CHUNK), acc, -jnp.inf)
        sc_ref[:, blk(j)] = acc
        sct_ref[blk(j), :] = acc.T

    _for_blocks(i, score_blk)
    score_blk(i, 0, diagonal=True)

    sub = lax.broadcasted_iota(I32, (SUBLANES, TQ), 0)
    kf = float(k_top)

    def search(n_blocks):
        def count(pred):
            accs = [jnp.zeros((SUBLANES, TQ), F32)] * 4
            for j in range(n_blocks):
                for g in range(TK // SUBLANES):
                    r0 = j * TK + SUBLANES * g
                    hit = jnp.where(pred(sct_ref[r0:r0 + SUBLANES, :], r0 + sub), 1.0, 0.0)
                    accs[g % 4] = accs[g % 4] + hit
            tot = jnp.sum((accs[0] + accs[1]) + (accs[2] + accs[3]), axis=0, keepdims=True)
            return jnp.broadcast_to(tot, (SUBLANES, TQ))

        c0 = count(lambda s, idx: s >= 0.0)
        ans_key = jnp.where(c0 >= kf, 0, INT_MIN).astype(I32)

        def bit_body(t, ans_key):
            cand_key = ans_key | lax.shift_left(jnp.int32(1), 30 - t)
            cand = _key_to_float(cand_key)
            c = count(lambda s, idx: s >= cand)
            return jnp.where(c >= kf, cand_key, ans_key)

        ans_key = lax.fori_loop(0, 31, bit_body, ans_key)
        ans = jnp.where(ans_key > KEY_NEG_INF, _key_to_float(ans_key), -jnp.inf)
        ans_ref[...] = ans

        cgt = count(lambda s, idx: s > ans)
        ceq = count(lambda s, idx: s == ans)
        need = kf - cgt
        tie = (ceq > need) & (ans > -jnp.inf)
        flag = jnp.max(jnp.where(tie[0:1, :], 1.0, 0.0), axis=1, keepdims=True)
        thr_ref[...] = jnp.full((SUBLANES, TQ), seq_len, I32)

        @pl.when(flag[0, 0] > 0.0)
        def _():
            nbits = int(seq_len - 1).bit_length()

            def tie_body(t, m):
                cand = m | lax.shift_left(jnp.int32(1), nbits - 1 - t)
                c = count(lambda s, idx: (s == ans) & (idx < cand))
                return jnp.where(c < need, cand, m)

            m = lax.fori_loop(0, nbits, tie_body, jnp.zeros((SUBLANES, TQ), I32))
            thr_ref[...] = jnp.where(tie, m, seq_len)

    for n in range(seq_len // TQ):
        pl.when(i == n)(functools.partial(search, n + 1))

    def to_rows(x):
        return jnp.broadcast_to(x[0:1, :], (LANES, TQ)).T

    ans_r = to_rows(ans_ref[...])
    thr_r = to_rows(thr_ref[...])

    def selb_blk(j, c):
        sb = sc_ref[:, blk(j)]
        col = j * TK + lane
        for half in range(2):
            sh = sb[:, half * LANES:(half + 1) * LANES]
            ch = col + half * LANES
            sel = ((sh > ans_r) | ((sh == ans_r) & (ch <= thr_r))) & (sh > -jnp.inf)
            selb_ref[:, pl.ds(pl.multiple_of(j * TK + half * LANES, LANES), LANES)] = jnp.where(sel, 0.0, NEG)
        return c

    lax.fori_loop(0, nkb, selb_blk, 0)

    for h in range(H_A):
        mx_ref[h] = jnp.full((TQ, LANES), NEG, F32)
    acc_ref[...] = jnp.zeros_like(acc_ref)

    def pass_a(j, bias_of_head):
        kb = akv_ref[0, blk(j), 0:LANES]
        sb = selb_ref[:, blk(j)]
        for h in range(H_A):
            s = lax.dot_general(qa_ref[h * TQ:(h + 1) * TQ, :], kb, NT_DIMS, preferred_element_type=F32)
            lg = s + sb + bias_of_head(h)
            lg_ref[h, :, blk(j)] = lg
            mx_ref[h] = jnp.maximum(mx_ref[h], jnp.maximum(lg[:, :LANES], lg[:, LANES:]))

    _for_blocks(jnp.maximum(i - 1, 0), lambda j, slot: pass_a(j, lambda h: rb_ref[far_bucket, h] * LOG2E))

    @pl.when(i >= 1)
    def _():
        pass_a(i - 1, lambda h: bias_ref[0, h])

    pass_a(i, lambda h: bias_ref[1, h])

    for h in range(H_A):
        mx_ref[h] = jnp.broadcast_to(jnp.max(mx_ref[h], axis=1, keepdims=True), (TQ, LANES))

    half = (H_A // 2) * TQ

    def pass_b(j, slot):
        for h in range(H_A):
            lg = lg_ref[h, :, blk(j)]
            m = mx_ref[h]
            p = jnp.exp2(lg - jnp.concatenate([m, m], axis=1)).astype(BF16)
            g = h // (H_A // HKV_A)
            acc_ref[h * TQ:(h + 1) * TQ, :] += jnp.dot(
                p, akv_ref[0, blk(j), (g + 1) * LANES:(g + 2) * LANES], preferred_element_type=F32)

    _for_blocks(nkb, pass_b)

    for p in range(4):
        out_ref[0, :, LANES * p:LANES * (p + 1)] = _normalize_pair(
            acc_ref[p * TQ:(p + 1) * TQ, :], acc_ref[(p + 4) * TQ:(p + 5) * TQ, :], low).astype(BF16)


def _dsa_prompt(rel_bias, aq, iq, iw, ik2, akv, bias_tiles, far_bucket):
    b, t, _ = aq.shape
    k_top = min(K_TOP_MAX, t // 4)
    qrow = lambda c: pl.BlockSpec((1, TQ, c), lambda i, j: (i, j, 0))
    full = lambda c: pl.BlockSpec((1, t, c), lambda i, j: (i, 0, 0))
    kern = functools.partial(_dsa_prompt_kernel, k_top=k_top, far_bucket=far_bucket, seq_len=t)
    return pl.pallas_call(
        kern,
        grid=(b, t // TQ),
        in_specs=[pl.BlockSpec(memory_space=pltpu.SMEM),
                  qrow(512), qrow(512), qrow(8), full(128), full(384),
                  _resident((2, H_A, TQ, TK))],
        out_specs=qrow(512),
        out_shape=jax.ShapeDtypeStruct((b, t, 512), BF16),
        scratch_shapes=[
            pltpu.VMEM((H_IDX * TQ, LANES), BF16),
            pltpu.VMEM((H_A * TQ, LANES), BF16),
            pltpu.VMEM((H_IDX, TQ, TK), F32),
            pltpu.VMEM((TQ, t), F32),
            pltpu.VMEM((t, TQ), F32),
            pltpu.VMEM((TQ, t), F32),
            pltpu.VMEM((H_A, TQ, t), F32),
            pltpu.VMEM((H_A, TQ, LANES), F32),
            pltpu.VMEM((H_A * TQ, LANES), F32),
            pltpu.VMEM((SUBLANES, TQ), I32),
            pltpu.VMEM((SUBLANES, TQ), F32),
        ],
        compiler_params=_cparams("parallel", "parallel"),
        name="dsa_prompt",
    )(rel_bias, aq, iq, iw, ik2, akv, bias_tiles)


def _fox_prompt_kernel(bq_ref, bk_ref, bv_ref, fkt_ref, out_ref,
                       qb_ref, fqrep_ref, lg_ref, mx_ref, acc_ref):
    i = pl.program_id(1)
    lane = lax.broadcasted_iota(I32, (TQ, LANES), 1)
    low = lane < HEAD_DIM

    for p in range(4):
        s_b = bq_ref[0, :, LANES * p:LANES * (p + 1)].astype(F32)
        qb_ref[(2 * p) * TQ:(2 * p + 1) * TQ, :] = jnp.where(low, s_b, 0.0).astype(BF16)
        qb_ref[(2 * p + 1) * TQ:(2 * p + 2) * TQ, :] = jnp.where(low, 0.0, s_b).astype(BF16)
    fq_t = fkt_ref[0, :, pl.ds(pl.multiple_of(i * TQ, TQ), TQ)]
    for h in range(H_B):
        fqrep_ref[h] = jnp.broadcast_to(fq_t[h:h + 1, :], (LANES, TQ)).T
        mx_ref[h] = jnp.full((TQ, LANES), NEG, F32)
    acc_ref[...] = jnp.zeros_like(acc_ref)

    def blk(j):
        return pl.ds(pl.multiple_of(j * TK, TK), TK)

    r2 = lax.broadcasted_iota(I32, (TQ, LANES), 0)

    def pass_a(j, diagonal):
        for p in range(4):
            s = lax.dot_general(qb_ref[(2 * p) * TQ:(2 * p + 2) * TQ, :],
                                bk_ref[0, blk(j), LANES * p:LANES * (p + 1)],
                                NT_DIMS, preferred_element_type=F32)
            for e in range(2):
                h = 2 * p + e
                fk = fkt_ref[0, h:h + 1, blk(j)]
                halves = []
                for half in range(2):
                    lg = (s[e * TQ:(e + 1) * TQ, half * LANES:(half + 1) * LANES]
                          + (fqrep_ref[h] - fk[:, half * LANES:(half + 1) * LANES]))
                    if diagonal:
                        lg = jnp.where(lane + half * LANES <= r2, lg, NEG)
                    lg_ref[h, :, pl.ds(pl.multiple_of(j * TK + half * LANES, LANES), LANES)] = lg
                    halves.append(lg)
                mx_ref[h] = jnp.maximum(mx_ref[h], jnp.maximum(halves[0], halves[1]))

    _for_blocks(i, lambda j, slot: pass_a(j, False))
    pass_a(i, True)

    for h in range(H_B):
        mx_ref[h] = jnp.broadcast_to(jnp.max(mx_ref[h], axis=1, keepdims=True), (TQ, LANES))

    def pass_b(j, slot):
        for h in range(H_B):
            lg = lg_ref[h, :, blk(j)]
            m = mx_ref[h]
            p = jnp.exp2(lg - jnp.concatenate([m, m], axis=1)).astype(BF16)
            acc_ref[h * TQ:(h + 1) * TQ, :] += jnp.dot(
                p, bv_ref[0, blk(j), LANES * h:LANES * (h + 1)], preferred_element_type=F32)

    _for_blocks(i + 1, pass_b)

    for p in range(4):
        out_ref[0, :, LANES * p:LANES * (p + 1)] = _normalize_pair(
            acc_ref[(2 * p) * TQ:(2 * p + 1) * TQ, :], acc_ref[(2 * p + 1) * TQ:(2 * p + 2) * TQ, :], low).astype(BF16)


def _fox_prompt(bq, bk, bv, fkt):
    b, t, _ = bq.shape
    qrow = lambda c: pl.BlockSpec((1, TQ, c), lambda i, j: (i, j, 0))
    full = lambda c: pl.BlockSpec((1, t, c), lambda i, j: (i, 0, 0))
    return pl.pallas_call(
        _fox_prompt_kernel,
        grid=(b, t // TQ),
        in_specs=[qrow(512), full(512), full(1024),
                  pl.BlockSpec((1, H_B, t), lambda i, j: (i, 0, 0))],
        out_specs=qrow(512),
        out_shape=jax.ShapeDtypeStruct((b, t, 512), BF16),
        scratch_shapes=[
            pltpu.VMEM((H_B * TQ, LANES), BF16),
            pltpu.VMEM((H_B, TQ, LANES), F32),
            pltpu.VMEM((H_B, TQ, t), F32),
            pltpu.VMEM((H_B, TQ, LANES), F32),
            pltpu.VMEM((H_B * TQ, LANES), F32),
        ],
        compiler_params=_cparams("parallel", "parallel"),
        name="fox_prompt",
    )(bq, bk, bv, fkt)


FF_BLK = 1024


def _tail_kernel(x_ref, oa_ref, ob_ref, woa_ref, wob_ref, gm_ref, wup_ref, wdn_ref, gf_ref, y_ref):
    mixed = (jnp.dot(oa_ref[...], woa_ref[...], preferred_element_type=F32)
             + jnp.dot(ob_ref[...], wob_ref[...], preferred_element_type=F32))
    x1 = x_ref[...] + mixed
    h2 = _rms(x1, gm_ref[...]).astype(BF16)
    acc = x1
    for c in range(D_FF // FF_BLK):
        u = jnp.dot(h2, wup_ref[:, c * FF_BLK:(c + 1) * FF_BLK], preferred_element_type=F32)
        u = jnp.square(jnp.maximum(u, 0.0)).astype(BF16)
        acc = acc + jnp.dot(u, wdn_ref[c * FF_BLK:(c + 1) * FF_BLK, :], preferred_element_type=F32)
    y_ref[...] = _rms(acc, gf_ref[...])


def _tail(x, oa, ob, woa, wob, g_mlp, w_up, w_down, g_final, tm):
    n, d = x.shape
    row = lambda c: pl.BlockSpec((tm, c), lambda i: (i, 0))
    return pl.pallas_call(
        _tail_kernel,
        grid=(n // tm,),
        in_specs=[row(d), row(512), row(512), _resident((512, d)), _resident((512, d)), _resident((1, d)),
                  _resident((d, D_FF)), _resident((D_FF, d)), _resident((1, d))],
        out_specs=row(d),
        out_shape=jax.ShapeDtypeStruct((n, d), F32),
        compiler_params=_cparams("parallel"),
        name="tail",
    )(x, oa, ob, woa, wob, g_mlp, w_up, w_down, g_final)


def _dsa_sample_kernel(aq_ref, iq_ref, iw_ref, ik2_ref, akv_ref, cikt_ref, cakt_ref, cavt_ref, bias_ref, out_ref,
                       *, past, n_new, l_pad, k_top, group):
    l_all = past + n_new
    n_tail = l_pad - past
    lane = lax.broadcasted_iota(I32, (n_new, LANES), 1)
    low = lane < HEAD_DIM

    def pad_rows(x):
        return jnp.concatenate([x, jnp.zeros((n_tail - n_new, x.shape[1]), F32)], axis=0).astype(BF16)

    scores = []
    for g in range(group):
        ik_new = pad_rows(ik2_ref[g].astype(F32)[:, 0:D_IDX])
        iq32 = iq_ref[g].astype(F32)
        qi = jnp.concatenate([iq32[:, D_IDX * h:D_IDX * (h + 1)] for h in range(H_IDX)], axis=0).astype(BF16)
        d = jnp.concatenate(
            [jnp.dot(qi, cikt_ref[g].astype(BF16), preferred_element_type=F32),
             lax.dot_general(qi, ik_new, NT_DIMS, preferred_element_type=F32)], axis=1)
        iw = iw_ref[g]
        score = jnp.zeros((n_new, l_pad), F32)
        for h in range(H_IDX):
            score = score + iw[:, h:h + 1] * jnp.maximum(d[h * n_new:(h + 1) * n_new, :], 0.0)
        scores.append(score)
    score = jnp.concatenate(scores, axis=0)

    rows = group * n_new
    row = lax.broadcasted_iota(I32, (rows, l_pad), 0) % n_new
    col = lax.broadcasted_iota(I32, (rows, l_pad), 1)
    adm = (col < l_all) & ((col // CHUNK) <= ((past + row) // CHUNK))
    sc = jnp.where(adm, score, -jnp.inf)

    def count(mask):
        return jnp.sum(jnp.where(mask, 1.0, 0.0), axis=1, keepdims=True)

    kf = float(k_top)
    ans_key = jnp.where(count(sc >= 0.0) >= kf, 0, INT_MIN).astype(I32)

    def bit_body(t, ans_key):
        cand_key = ans_key | lax.shift_left(jnp.int32(1), 30 - t)
        return jnp.where(count(sc >= _key_to_float(cand_key)) >= kf, cand_key, ans_key)

    ans_key = lax.fori_loop(0, 31, bit_body, ans_key)
    ans = jnp.where(ans_key > KEY_NEG_INF, _key_to_float(ans_key), -jnp.inf)
    eq = sc == ans
    need = kf - count(sc > ans)
    tie = (count(eq) > need) & (ans > -jnp.inf)
    flag = jnp.max(jnp.where(tie, 1.0, 0.0), axis=0, keepdims=True)
    nbits = int(l_pad - 1).bit_length()

    def tie_search(_):
        def tie_body(t, m):
            cand = m | lax.shift_left(jnp.int32(1), nbits - 1 - t)
            return jnp.where(count(eq & (col < cand)) < need, cand, m)
        m = lax.fori_loop(0, nbits, tie_body, jnp.zeros((rows, 1), I32))
        return jnp.where(tie, m, l_pad)

    thr = lax.cond(flag[0, 0] > 0.0, tie_search, lambda _: jnp.full((rows, 1), l_pad, I32), 0)
    sel = ((sc > ans) | (eq & (col <= thr))) & (sc > -jnp.inf)
    selb_all = jnp.where(sel, 0.0, NEG)

    for g in range(group):
        akv = akv_ref[g].astype(F32)
        ak_new = pad_rows(akv[:, 0:LANES])
        av_new = pad_rows(jnp.where(low, akv[:, LANES:2 * LANES], akv[:, 2 * LANES:3 * LANES]))
        selb = selb_all[g * n_new:(g + 1) * n_new, :]
        aq32 = aq_ref[g].astype(F32)
        slabs = [aq32[:, LANES * p:LANES * (p + 1)] for p in range(4)]
        qa = jnp.concatenate([jnp.where(low, s, 0.0) for s in slabs] + [jnp.where(low, 0.0, s) for s in slabs],
                             axis=0).astype(BF16)
        s = jnp.concatenate(
            [jnp.dot(qa, cakt_ref[g].astype(BF16), preferred_element_type=F32),
             lax.dot_general(qa, ak_new, NT_DIMS, preferred_element_type=F32)], axis=1)
        lg = s + bias_ref[...] + jnp.concatenate([selb] * H_A, axis=0)
        m = jnp.max(lg, axis=1, keepdims=True)
        p = jnp.exp2(lg - m)
        l = jnp.sum(p, axis=1, keepdims=True)
        pb = p.astype(BF16)
        o = (lax.dot_general(pb[:, 0:past], cavt_ref[g].astype(BF16), NT_DIMS, preferred_element_type=F32)
             + jnp.dot(pb[:, past:l_pad], av_new, preferred_element_type=F32)) / l
        for q in range(4):
            out_ref[g, :, LANES * q:LANES * (q + 1)] = jnp.where(
                low, o[q * n_new:(q + 1) * n_new, :], o[(q + 4) * n_new:(q + 5) * n_new, :]).astype(BF16)


SAMPLE_GROUP = 4


def _dsa_sample(aq, iq, iw, ik2, akv, cikt, cakt, cavt, bias_rows, past):
    b, n_new, _ = aq.shape
    l_pad = bias_rows.shape[1]
    assert past % LANES == 0 and l_pad == past + LANES and n_new <= LANES and b % SAMPLE_GROUP == 0
    k_top = min(K_TOP_MAX, (past + n_new) // 4)
    new = lambda c: pl.BlockSpec((SAMPLE_GROUP, n_new, c), lambda i: (i, 0, 0))
    old = lambda c: pl.BlockSpec((SAMPLE_GROUP, c, past), lambda i: (i, 0, 0))
    kern = functools.partial(_dsa_sample_kernel, past=past, n_new=n_new, l_pad=l_pad, k_top=k_top,
                             group=SAMPLE_GROUP)
    return pl.pallas_call(
        kern,
        grid=(b // SAMPLE_GROUP,),
        in_specs=[new(512), new(512), new(8), new(128), new(384), old(D_IDX), old(LANES), old(LANES),
                  _resident((H_A * n_new, l_pad))],
        out_specs=new(512),
        out_shape=jax.ShapeDtypeStruct((b, n_new, 512), BF16),
        compiler_params=_cparams("parallel"),
        name="dsa_sample",
    )(aq, iq, iw, ik2, akv, cikt, cakt, cavt, bias_rows)


def _fox_sample_kernel(bq_ref, bkn_ref, bvn_ref, ckt_ref, cvt_ref, fkt_ref, tot_ref, lf_ref, lft_ref, out_ref,
                       *, past, n_new):
    lf = lf_ref[0] * LOG2E
    lft = lft_ref[0] * LOG2E
    rown = lax.broadcasted_iota(I32, (n_new, H_B), 0)
    lanen = lax.broadcasted_iota(I32, (H_B, n_new), 1)
    fq = jnp.broadcast_to(tot_ref[0], (n_new, H_B))
    fqt = jnp.broadcast_to(fkt_ref[0][:, past - 1:past], (H_B, n_new))
    for s in range(n_new):
        fq = fq + jnp.where(rown >= s, lf[s:s + 1, :], 0.0)
        fqt = fqt + jnp.where(lanen >= s, lft[:, s:s + 1], 0.0)

    kct = ckt_ref[0].astype(BF16)
    vct = cvt_ref[0].astype(BF16)
    bq32 = bq_ref[0].astype(F32)
    head_of_lane = lax.broadcasted_iota(I32, (n_new, H_B * HEAD_DIM), 1) // HEAD_DIM
    qb = jnp.concatenate([jnp.where(head_of_lane == h, bq32, 0.0) for h in range(H_B)], axis=0).astype(BF16)
    s_past = jnp.dot(qb, kct, preferred_element_type=F32)
    s_new = lax.dot_general(qb, bkn_ref[0], NT_DIMS, preferred_element_type=F32)
    fkt = fkt_ref[0]
    fq_col = jnp.concatenate([fq[:, h:h + 1] for h in range(H_B)], axis=0)
    fk_past = jnp.concatenate([jnp.broadcast_to(fkt[h:h + 1, :], (n_new, past)) for h in range(H_B)], axis=0)
    fk_new = jnp.concatenate([jnp.broadcast_to(fqt[h:h + 1, :], (n_new, n_new)) for h in range(H_B)], axis=0)
    lg_past = s_past + (fq_col - fk_past)
    trow = lax.broadcasted_iota(I32, (H_B * n_new, n_new), 0) % n_new
    tcol = lax.broadcasted_iota(I32, (H_B * n_new, n_new), 1)
    lg_new = jnp.where(tcol <= trow, s_new + (fq_col - fk_new), NEG)
    m = jnp.maximum(jnp.max(lg_past, axis=1, keepdims=True), jnp.max(lg_new, axis=1, keepdims=True))
    p_past = jnp.exp2(lg_past - m)
    p_new = jnp.exp2(lg_new - m)
    l = jnp.sum(p_past, axis=1, keepdims=True) + jnp.sum(p_new, axis=1, keepdims=True)
    o = (lax.dot_general(p_past.astype(BF16), vct, NT_DIMS, preferred_element_type=F32)
         + jnp.dot(p_new.astype(BF16), bvn_ref[0].astype(BF16), preferred_element_type=F32)) / l
    out = jnp.zeros((n_new, H_B * HEAD_DIM), F32)
    for h in range(H_B):
        out = out + jnp.where(head_of_lane == h, o[h * n_new:(h + 1) * n_new, :], 0.0)
    out_ref[0] = out.astype(BF16)


def _fox_sample(bq, bkn, bvn, ckt, cvt, fkt, tot, lf, lft, past):
    b, n_new, _ = bq.shape
    new = lambda c: pl.BlockSpec((1, n_new, c), lambda i: (i, 0, 0))
    old = pl.BlockSpec((1, H_B * HEAD_DIM, past), lambda i: (i, 0, 0))
    kern = functools.partial(_fox_sample_kernel, past=past, n_new=n_new)
    return pl.pallas_call(
        kern,
        grid=(b,),
        in_specs=[new(512), new(512), new(512), old, old,
                  pl.BlockSpec((1, H_B, past), lambda i: (i, 0, 0)),
                  pl.BlockSpec((1, 1, H_B), lambda i: (i, 0, 0)),
                  new(H_B),
                  pl.BlockSpec((1, H_B, n_new), lambda i: (i, 0, 0))],
        out_specs=new(512),
        out_shape=jax.ShapeDtypeStruct((b, n_new, 512), BF16),
        compiler_params=_cparams("parallel"),
        name="fox_sample",
    )(bq, bkn, bvn, ckt, cvt, fkt, tot, lf, lft)


def _prep_weights(w_in, w_o, w_up, w_down):
    offs = np.concatenate([[0], np.cumsum(IN_SPLITS)])
    seg = lambda k: w_in[:, int(offs[k]):int(offs[k + 1])]
    a_q, a_k, a_v, i_q, i_k, i_w, b_q, b_k, b_v, f_z = (seg(k) for k in range(10))
    a_q_pairs = jnp.concatenate([a_q[:, HEAD_DIM * h:HEAD_DIM * (h + 1)] for h in AQ_HEAD_ORDER], axis=1)
    pad = jnp.zeros((w_in.shape[0], N_CAT - C_SM - 16), w_in.dtype)
    w_cat = jnp.concatenate([a_q_pairs, i_q, b_q, a_k, a_v, b_k, b_v, i_k, i_k, i_w, f_z, pad], axis=1)
    wo_a = jnp.concatenate([w_o[HEAD_DIM * h:HEAD_DIM * (h + 1)] for h in AQ_HEAD_ORDER], axis=0)
    wo_b = w_o[H_A * HEAD_DIM:]
    return (w_cat.astype(BF16), wo_a.astype(BF16), wo_b.astype(BF16), w_up.astype(BF16), w_down.astype(BF16))


def kernel(x_prompt, x_sample, cache_a_k, cache_a_v, cache_idx_k, cache_b_k, cache_b_v, cache_b_logf,
           w_in, w_o, b_f, rel_bias, g_attn, w_up, w_down, g_mlp, g_final):
    assert w_in.shape[0] == 1, "single-layer trunk"
    bp, tp, d = x_prompt.shape
    bs, ts, _ = x_sample.shape
    past = cache_a_k.shape[2]
    w_cat, wo_a, wo_b, wup, wdn = _prep_weights(w_in[0], w_o[0], w_up[0], w_down[0])
    g_a = g_attn[0].reshape(1, d)
    g_m = g_mlp[0].reshape(1, d)
    g_f = g_final.reshape(1, d)
    bf = b_f[0].reshape(H_B, 1)
    rel_bias = rel_bias.astype(F32)

    (aq, iq, bq, akv, bk, bv, ik2, ak32, av32, bk32, bv32, iw, ikt32, logft) = _inproj(x_prompt, g_a, w_cat, bf, 512)
    fkt = _cumsum_lanes(logft)
    r = np.arange(TQ)[:, None]
    c = np.arange(TK)[None, :]
    bucket = _t5_bucket_np(np.stack([c - TK - r, c - r]))
    far_bucket = int(_t5_bucket_np(np.array(-TK - 1)))
    assert far_bucket == int(_t5_bucket_np(np.array(-tp)))
    bias_tiles = _bias_tiles(rel_bias, jnp.asarray(bucket))
    out_a = _dsa_prompt(rel_bias, aq, iq, iw, ik2, akv, bias_tiles, far_bucket)
    out_b = _fox_prompt(bq, bk, bv, fkt)
    y_p = _tail(x_prompt.reshape(bp * tp, d), out_a.reshape(bp * tp, 512), out_b.reshape(bp * tp, 512),
                wo_a, wo_b, g_m, wup, wdn, g_f, 512).reshape(bp, tp, d)

    n_s = bs * ts
    outs = _inproj(x_sample.reshape(1, n_s, d), g_a, w_cat, bf, n_s)
    (aq_s, iq_s, bq_s, akv_s, bk_s, bv_s, ik2_s, ak32_s, av32_s, bk32_s, bv32_s, iw_s) = (
        o.reshape(bs, ts, o.shape[-1]) for o in outs[:12])
    ik32_s = jnp.swapaxes(outs[12][0], 0, 1).reshape(bs, ts, D_IDX)
    logf_s = jnp.swapaxes(outs[13][0], 0, 1).reshape(bs, ts, H_B)
    l_all = past + ts
    l_pad = -(-l_all // LANES) * LANES
    rel_s = np.arange(l_pad)[None, :] - (past + np.arange(ts))[:, None]
    bias_s = _bias_tiles(rel_bias, jnp.asarray(_t5_bucket_np(rel_s))[None])[0].reshape(H_A * ts, l_pad)
    def feature_major(cache):
        c = cache[0]
        c = jnp.transpose(c, (0, 2, 3, 1)) if c.ndim == 4 else jnp.transpose(c, (0, 2, 1))
        return c.reshape(bs, -1, past)

    out_a_s = _dsa_sample(aq_s, iq_s, iw_s, ik2_s, akv_s, feature_major(cache_idx_k),
                          feature_major(cache_a_k), feature_major(cache_a_v), bias_s, past)
    fkt_c = _cumsum_lanes(jnp.swapaxes(cache_b_logf[0].astype(F32), 1, 2))
    tot = fkt_c[:, :, past - 1].reshape(bs, 1, H_B)
    out_b_s = _fox_sample(bq_s, bk_s, bv32_s, feature_major(cache_b_k), feature_major(cache_b_v), fkt_c, tot,
                          logf_s, jnp.swapaxes(logf_s, 1, 2), past)
    y_s = _tail(x_sample.reshape(n_s, d), out_a_s.reshape(n_s, 512), out_b_s.reshape(n_s, 512),
                wo_a, wo_b, g_m, wup, wdn, g_f, n_s).reshape(bs, ts, d)

    def rows(a, heads, b, t):
        return a.reshape(1, b, t, heads, HEAD_DIM)

    return (y_p, y_s,
            rows(ak32, HKV_A, bp, tp), rows(av32, HKV_A, bp, tp), jnp.swapaxes(ikt32, 1, 2)[None],
            rows(bk32, H_B, bp, tp), rows(bv32, H_B, bp, tp), jnp.swapaxes(logft, 1, 2)[None],
            rows(ak32_s, HKV_A, bs, ts), rows(av32_s, HKV_A, bs, ts), ik32_s.reshape(1, bs, ts, D_IDX),
            rows(bk32_s, H_B, bs, ts), rows(bv32_s, H_B, bs, ts), logf_s.reshape(1, bs, ts, H_B))
```

```python
import functools
import math

import numpy as np
import jax
import jax.numpy as jnp
from jax import lax
from jax.experimental import pallas as pl
from jax.experimental.pallas import tpu as pltpu

F32 = jnp.float32
BF16 = jnp.bfloat16
I32 = jnp.int32

D_MODEL = 1024
CHUNK = 64
HEAD_DIM = 64
H_A = 8
HKV_A = 2
H_B = 8
H_IDX = 8
D_IDX = 64
K_TOP_MAX = 256
NUM_BUCKETS = 32
MAX_DISTANCE = 128
D_FF = 4 * D_MODEL
EPS = 1e-6
IN_SPLITS = (H_A * HEAD_DIM, HKV_A * HEAD_DIM, HKV_A * HEAD_DIM, H_IDX * D_IDX, D_IDX, H_IDX,
             H_B * HEAD_DIM, H_B * HEAD_DIM, H_B * HEAD_DIM, H_B)

LANES = 128
SUBLANES = 8
VMEM_LIMIT = 56 * 1024 * 1024

TQ = 256
TK = 256
NEG = -1e30
LOG2E = 1.4426950408889634
KEY_NEG_INF = -2139095041
INT_MIN = -2147483648

C_AQ, C_IQ, C_BQ, C_AKV, C_BK, C_BV, C_IK2, C_SM, N_CAT = 0, 512, 1024, 1536, 1792, 2304, 2816, 2944, 3072
AQ_HEAD_ORDER = (0, 4, 1, 5, 2, 6, 3, 7)

NT_DIMS = (((1,), (1,)), ((), ()))


def _cparams(*sem):
    return pltpu.CompilerParams(dimension_semantics=sem, vmem_limit_bytes=VMEM_LIMIT)


def _resident(shape):
    nd = len(shape)
    return pl.BlockSpec(shape, lambda *_: (0,) * nd, pipeline_mode=pl.Buffered(1))


def _rms(x, g):
    ms = jnp.mean(x * x, axis=-1, keepdims=True)
    return (x * lax.rsqrt(ms + EPS)) * g


def _for_blocks(n, body):
    def pair(k, c):
        body(2 * k, 0)
        body(2 * k + 1, 1)
        return c

    lax.fori_loop(0, n // 2, pair, 0)

    @pl.when(n % 2 == 1)
    def _():
        body(n - 1, 0)


def _normalize_pair(a_lo, a_hi, low):
    num = jnp.where(low, a_lo, a_hi)
    den = pltpu.roll(jnp.where(low, a_hi, a_lo), HEAD_DIM, axis=1)
    return num / den


def _key_to_float(k):
    return lax.bitcast_convert_type(k ^ (lax.shift_right_arithmetic(k, 31) & 0x7FFFFFFF), F32)


def _t5_bucket_np(rel):
    half = NUM_BUCKETS // 2
    max_exact = half // 2
    ret = np.where(rel > 0, half, 0)
    n = np.abs(rel)
    n_f = np.maximum(n, max_exact).astype(np.float64)
    large = max_exact + (np.log(n_f / max_exact) / math.log(MAX_DISTANCE / max_exact)
                         * (half - max_exact)).astype(np.int32)
    large = np.minimum(large, half - 1)
    return (ret + np.where(n < max_exact, n, large)).astype(np.int32)


def _inproj_kernel(x_ref, g_ref, w_ref, bf_ref,
                   aq_ref, iq_ref, bq_ref, akv_ref, bk_ref, bv_ref, ik2_ref,
                   ak32_ref, av32_ref, bk32_ref, bv32_ref, iw_ref, ikt_ref, logft_ref):
    h = _rms(x_ref[0], g_ref[...])
    p = jnp.dot(h.astype(BF16), w_ref[...], preferred_element_type=F32)
    qscale = HEAD_DIM ** -0.5 * LOG2E
    aq_ref[0] = (p[:, C_AQ:C_AQ + 512] * qscale).astype(BF16)
    iq_ref[0] = (p[:, C_IQ:C_IQ + 512] * (D_IDX ** -0.5)).astype(BF16)
    bq_ref[0] = (p[:, C_BQ:C_BQ + 512] * qscale).astype(BF16)
    low = lax.broadcasted_iota(I32, (p.shape[0], LANES), 1) < HEAD_DIM
    av = p[:, C_AKV + LANES:C_AKV + 2 * LANES]
    akv_ref[0, :, 0:LANES] = p[:, C_AKV:C_AKV + LANES].astype(BF16)
    akv_ref[0, :, LANES:2 * LANES] = jnp.where(low, av, 1.0).astype(BF16)
    akv_ref[0, :, 2 * LANES:3 * LANES] = jnp.where(low, 1.0, av).astype(BF16)
    for q in range(4):
        bk_ref[0, q] = p[:, C_BK + LANES * q:C_BK + LANES * (q + 1)].astype(BF16)
        bv = p[:, C_BV + LANES * q:C_BV + LANES * (q + 1)]
        bv_ref[0, 2 * q] = jnp.where(low, bv, 1.0).astype(BF16)
        bv_ref[0, 2 * q + 1] = jnp.where(low, 1.0, bv).astype(BF16)
    ik2_ref[0] = p[:, C_IK2:C_IK2 + 128].astype(BF16)
    ak32_ref[0] = p[:, C_AKV:C_AKV + 128]
    av32_ref[0] = p[:, C_AKV + 128:C_AKV + 256]
    bk32_ref[0] = p[:, C_BK:C_BK + 512]
    bv32_ref[0] = p[:, C_BV:C_BV + 512]
    iw_ref[0] = p[:, C_SM:C_SM + 8] * (H_IDX ** -0.5)
    ikt_ref[0] = p[:, C_IK2:C_IK2 + LANES].T[0:D_IDX, :]
    z = p[:, C_SM:C_SM + LANES].T[8:16, :] + bf_ref[...]
    logft_ref[0] = jnp.minimum(z, 0.0) - jnp.log1p(jnp.exp(-jnp.abs(z)))


def _inproj(x, g, w_cat, b_f, tm):
    b, t, d = x.shape
    grid = (b, t // tm)
    row = lambda c: pl.BlockSpec((1, tm, c), lambda i, j: (i, j, 0))
    col = lambda c: pl.BlockSpec((1, c, tm), lambda i, j: (i, 0, j))
    slab = lambda n: pl.BlockSpec((1, n, tm, LANES), lambda i, j: (i, 0, j, 0))
    outs = [(512, BF16), (512, BF16), (512, BF16), (384, BF16), (4, BF16), (8, BF16), (128, BF16),
            (128, F32), (128, F32), (512, F32), (512, F32), (8, F32)]
    slab_outs = (4, 5)
    outs_t = [D_IDX, H_B]
    return pl.pallas_call(
        _inproj_kernel,
        grid=grid,
        in_specs=[row(d), _resident((1, d)), _resident((d, N_CAT)), _resident((H_B, 1))],
        out_specs=([slab(c) if k in slab_outs else row(c) for k, (c, _) in enumerate(outs)]
                   + [col(c) for c in outs_t]),
        out_shape=([jax.ShapeDtypeStruct((b, c, t, LANES) if k in slab_outs else (b, t, c), dt)
                    for k, (c, dt) in enumerate(outs)]
                   + [jax.ShapeDtypeStruct((b, c, t), F32) for c in outs_t]),
        compiler_params=_cparams("parallel", "parallel"),
        name="inproj",
    )(x, g, w_cat, b_f)


CS_BLK = 512


def _cumsum_kernel(x_ref, out_ref, carry_ref):
    @pl.when(pl.program_id(0) == 0)
    def _():
        carry_ref[...] = jnp.zeros_like(carry_ref)

    x = x_ref[...] * LOG2E
    hi = x.astype(BF16)
    r1 = x - hi.astype(F32)
    mid = r1.astype(BF16)
    lo = (r1 - mid.astype(F32)).astype(BF16)
    src = lax.broadcasted_iota(I32, (CS_BLK, CS_BLK), 0)
    dst = lax.broadcasted_iota(I32, (CS_BLK, CS_BLK), 1)
    tri = jnp.where(src <= dst, 1.0, 0.0).astype(BF16)
    cs = (jnp.dot(hi, tri, preferred_element_type=F32)
          + jnp.dot(mid, tri, preferred_element_type=F32)
          + jnp.dot(lo, tri, preferred_element_type=F32))
    out = cs + carry_ref[...]
    out_ref[...] = out
    carry_ref[...] = jnp.broadcast_to(out[:, CS_BLK - 1:CS_BLK], carry_ref.shape)


def _cumsum_lanes(x_t):
    b, h, l = x_t.shape
    spec = pl.BlockSpec((b * h, CS_BLK), lambda j: (0, j))
    return pl.pallas_call(
        _cumsum_kernel,
        grid=(l // CS_BLK,),
        in_specs=[spec],
        out_specs=spec,
        out_shape=jax.ShapeDtypeStruct((b * h, l), F32),
        scratch_shapes=[pltpu.VMEM((b * h, CS_BLK), F32)],
        compiler_params=_cparams("arbitrary"),
        name="cumsum",
    )(x_t.reshape(b * h, l)).reshape(b, h, l)


def _bias_kernel(rb_ref, bucket_ref, out_ref):
    b = bucket_ref[0]
    for h in range(H_A):
        acc = jnp.zeros(b.shape, F32)
        for k in range(NUM_BUCKETS):
            acc = jnp.where(b == k, rb_ref[k, h] * LOG2E, acc)
        out_ref[0, h] = acc


def _bias_tiles(rel_bias, bucket):
    n, r, c = bucket.shape
    return pl.pallas_call(
        _bias_kernel,
        grid=(n,),
        in_specs=[pl.BlockSpec(memory_space=pltpu.SMEM),
                  pl.BlockSpec((1, r, c), lambda i: (i, 0, 0))],
        out_specs=pl.BlockSpec((1, H_A, r, c), lambda i: (i, 0, 0, 0)),
        out_shape=jax.ShapeDtypeStruct((n, H_A, r, c), F32),
        compiler_params=_cparams("parallel"),
        name="t5_bias",
    )(rel_bias, bucket)


def _dsa_prompt_kernel(rb_ref, aq_ref, iq_ref, iw_ref, ik2_ref, akv_ref, bias_ref, out_ref,
                       qi_ref, qa_ref, iwrep_ref, sc_ref, sct_ref, lg_ref, mx_ref,
                       acc_ref, thr_ref, ans_ref, *, k_top, far_bucket, seq_len):
    i = pl.program_id(1)
    nkb = i + 1
    lane = lax.broadcasted_iota(I32, (TQ, LANES), 1)
    low = lane < HEAD_DIM

    for p in range(4):
        s_i = iq_ref[0, :, LANES * p:LANES * (p + 1)].astype(F32)
        qi_ref[(2 * p) * TQ:(2 * p + 1) * TQ, :] = jnp.where(low, s_i, 0.0).astype(BF16)
        qi_ref[(2 * p + 1) * TQ:(2 * p + 2) * TQ, :] = jnp.where(low, 0.0, s_i).astype(BF16)
        s_a = aq_ref[0, :, LANES * p:LANES * (p + 1)].astype(F32)
        qa_ref[p * TQ:(p + 1) * TQ, :] = jnp.where(low, s_a, 0.0).astype(BF16)
        qa_ref[(p + 4) * TQ:(p + 5) * TQ, :] = jnp.where(low, 0.0, s_a).astype(BF16)
    iw = iw_ref[0]
    for h in range(H_IDX):
        iwrep_ref[h] = jnp.broadcast_to(iw[:, h:h + 1], (TQ, LANES))

    def blk(j):
        return pl.ds(pl.multiple_of(j * TK, TK), TK)

    r2 = lax.broadcasted_iota(I32, (TQ, TK), 0)
    c2 = lax.broadcasted_iota(I32, (TQ, TK), 1)

    def score_blk(j, slot, diagonal=False):
        kb = ik2_ref[0, blk(j), :]
        acc = jnp.zeros((TQ, TK), F32)
        for h in range(H_IDX):
            d = lax.dot_general(qi_ref[h * TQ:(h + 1) * TQ, :], kb, NT_DIMS, preferred_element_type=F32)
            w = iwrep_ref[h]
            acc = acc + jnp.concatenate([w, w], axis=1) * jnp.maximum(d, 0.0)
        if diagonal:
            acc = jnp.where((c2 // CHUNK) <= (r2 // CHUNK), acc, -jnp.inf)
        sc_ref[j] = acc
        sct_ref[blk(j), :] = acc.T

    _for_blocks(i, score_blk)
    score_blk(i, 0, diagonal=True)

    sub = lax.broadcasted_iota(I32, (SUBLANES, TQ), 0)
    kf = float(k_top)

    def search(n_blocks):
        def count(pred):
            accs = [jnp.zeros((SUBLANES, TQ), F32)] * 4
            for j in range(n_blocks):
                for g in range(TK // SUBLANES):
                    r0 = j * TK + SUBLANES * g
                    hit = jnp.where(pred(sct_ref[r0:r0 + SUBLANES, :], r0 + sub), 1.0, 0.0)
                    accs[g % 4] = accs[g % 4] + hit
            tot = jnp.sum((accs[0] + accs[1]) + (accs[2] + accs[3]), axis=0, keepdims=True)
            return jnp.broadcast_to(tot, (SUBLANES, TQ))

        c0 = count(lambda s, idx: s >= 0.0)
        ans_key = jnp.where(c0 >= kf, 0, INT_MIN).astype(I32)

        def bit_body(t, ans_key):
            cand_key = ans_key | lax.shift_left(jnp.int32(1), 30 - t)
            cand = _key_to_float(cand_key)
            c = count(lambda s, idx: s >= cand)
            return jnp.where(c >= kf, cand_key, ans_key)

        ans_key = lax.fori_loop(0, 31, bit_body, ans_key)
        ans = jnp.where(ans_key > KEY_NEG_INF, _key_to_float(ans_key), -jnp.inf)
        ans_ref[...] = ans

        cgt = count(lambda s, idx: s > ans)
        ceq = count(lambda s, idx: s == ans)
        need = kf - cgt
        tie = (ceq > need) & (ans > -jnp.inf)
        flag = jnp.max(jnp.where(tie[0:1, :], 1.0, 0.0), axis=1, keepdims=True)
        thr_ref[...] = jnp.full((SUBLANES, TQ), seq_len, I32)

        @pl.when(flag[0, 0] > 0.0)
        def _():
            nbits = int(seq_len - 1).bit_length()

            def tie_body(t, m):
                cand = m | lax.shift_left(jnp.int32(1), nbits - 1 - t)
                c = count(lambda s, idx: (s == ans) & (idx < cand))
                return jnp.where(c < need, cand, m)

            m = lax.fori_loop(0, nbits, tie_body, jnp.zeros((SUBLANES, TQ), I32))
            thr_ref[...] = jnp.where(tie, m, seq_len)

    for n in range(seq_len // TQ):
        pl.when(i == n)(functools.partial(search, n + 1))

    def to_rows(x):
        return jnp.broadcast_to(x[0:1, :], (LANES, TQ)).T

    ans_r = to_rows(ans_ref[...])
    thr_r = to_rows(thr_ref[...])

    def select_bias(j):
        sc = sc_ref[j]
        col = j * TK + lane
        halves = []
        for half in range(2):
            sh = sc[:, half * LANES:(half + 1) * LANES]
            ch = col + half * LANES
            sel = ((sh > ans_r) | ((sh == ans_r) & (ch <= thr_r))) & (sh > -jnp.inf)
            halves.append(jnp.where(sel, 0.0, NEG))
        return jnp.concatenate(halves, axis=1)

    for h in range(H_A):
        mx_ref[h] = jnp.full((TQ, LANES), NEG, F32)
    acc_ref[...] = jnp.zeros_like(acc_ref)

    def pass_a(j, bias_of_head):
        kb = akv_ref[0, blk(j), 0:LANES]
        sb = select_bias(j)
        for h in range(H_A):
            s = lax.dot_general(qa_ref[h * TQ:(h + 1) * TQ, :], kb, NT_DIMS, preferred_element_type=F32)
            lg = s + sb + bias_of_head(h)
            lg_ref[h, j] = lg
            mx_ref[h] = jnp.maximum(mx_ref[h], jnp.maximum(lg[:, :LANES], lg[:, LANES:]))

    _for_blocks(jnp.maximum(i - 1, 0), lambda j, slot: pass_a(j, lambda h: rb_ref[far_bucket, h] * LOG2E))

    @pl.when(i >= 1)
    def _():
        pass_a(i - 1, lambda h: bias_ref[0, h])

    pass_a(i, lambda h: bias_ref[1, h])

    for h in range(H_A):
        mx_ref[h] = jnp.broadcast_to(jnp.max(mx_ref[h], axis=1, keepdims=True), (TQ, LANES))

    half = (H_A // 2) * TQ

    def pass_b(j, slot):
        for h in range(H_A):
            lg = lg_ref[h, j]
            m = mx_ref[h]
            p = jnp.exp2(lg - jnp.concatenate([m, m], axis=1)).astype(BF16)
            g = h // (H_A // HKV_A)
            acc_ref[h * TQ:(h + 1) * TQ, :] += jnp.dot(
                p, akv_ref[0, blk(j), (g + 1) * LANES:(g + 2) * LANES], preferred_element_type=F32)

    _for_blocks(nkb, pass_b)

    for p in range(4):
        out_ref[0, :, LANES * p:LANES * (p + 1)] = _normalize_pair(
            acc_ref[p * TQ:(p + 1) * TQ, :], acc_ref[(p + 4) * TQ:(p + 5) * TQ, :], low).astype(BF16)


def _dsa_prompt(rel_bias, aq, iq, iw, ik2, akv, bias_tiles, far_bucket):
    b, t, _ = aq.shape
    k_top = min(K_TOP_MAX, t // 4)
    qrow = lambda c: pl.BlockSpec((1, TQ, c), lambda i, j: (i, j, 0))
    full = lambda c: pl.BlockSpec((1, t, c), lambda i, j: (i, 0, 0))
    kern = functools.partial(_dsa_prompt_kernel, k_top=k_top, far_bucket=far_bucket, seq_len=t)
    return pl.pallas_call(
        kern,
        grid=(b, t // TQ),
        in_specs=[pl.BlockSpec(memory_space=pltpu.SMEM),
                  qrow(512), qrow(512), qrow(8), full(128), full(384),
                  _resident((2, H_A, TQ, TK))],
        out_specs=qrow(512),
        out_shape=jax.ShapeDtypeStruct((b, t, 512), BF16),
        scratch_shapes=[
            pltpu.VMEM((H_IDX * TQ, LANES), BF16),
            pltpu.VMEM((H_A * TQ, LANES), BF16),
            pltpu.VMEM((H_IDX, TQ, LANES), F32),
            pltpu.VMEM((t // TK, TQ, TK), F32),
            pltpu.VMEM((t, TQ), F32),
            pltpu.VMEM((H_A, t // TK, TQ, TK), F32),
            pltpu.VMEM((H_A, TQ, LANES), F32),
            pltpu.VMEM((H_A * TQ, LANES), F32),
            pltpu.VMEM((SUBLANES, TQ), I32),
            pltpu.VMEM((SUBLANES, TQ), F32),
        ],
        compiler_params=_cparams("parallel", "parallel"),
        name="dsa_prompt",
    )(rel_bias, aq, iq, iw, ik2, akv, bias_tiles)


def _fox_prompt_kernel(bq_ref, bk_ref, bv_ref, fkt_ref, out_ref,
                       qb_ref, fqrep_ref, lg_ref, mx_ref, acc_ref):
    i = pl.program_id(1)
    lane = lax.broadcasted_iota(I32, (TQ, LANES), 1)
    low = lane < HEAD_DIM

    for p in range(4):
        s_b = bq_ref[0, :, LANES * p:LANES * (p + 1)].astype(F32)
        qb_ref[(2 * p) * TQ:(2 * p + 1) * TQ, :] = jnp.where(low, s_b, 0.0).astype(BF16)
        qb_ref[(2 * p + 1) * TQ:(2 * p + 2) * TQ, :] = jnp.where(low, 0.0, s_b).astype(BF16)
    fq_t = fkt_ref[0, :, pl.ds(pl.multiple_of(i * TQ, TQ), TQ)]
    for h in range(H_B):
        fqrep_ref[h] = jnp.broadcast_to(fq_t[h:h + 1, :], (LANES, TQ)).T
        mx_ref[h] = jnp.full((TQ, LANES), NEG, F32)
    acc_ref[...] = jnp.zeros_like(acc_ref)

    def blk(j):
        return pl.ds(pl.multiple_of(j * TK, TK), TK)

    r2 = lax.broadcasted_iota(I32, (TQ, LANES), 0)

    def pass_a(j, diagonal):
        for p in range(4):
            s = lax.dot_general(qb_ref[(2 * p) * TQ:(2 * p + 2) * TQ, :],
                                bk_ref[0, p, blk(j), :],
                                NT_DIMS, preferred_element_type=F32)
            for e in range(2):
                h = 2 * p + e
                fk = fkt_ref[0, h:h + 1, blk(j)]
                fq_h = fqrep_ref[h]
                halves = []
                for half in range(2):
                    lg = (s[e * TQ:(e + 1) * TQ, half * LANES:(half + 1) * LANES]
                          + (fq_h - fk[:, half * LANES:(half + 1) * LANES]))
                    if diagonal:
                        lg = jnp.where(lane + half * LANES <= r2, lg, NEG)
                    lg_ref[h, j, :, half * LANES:(half + 1) * LANES] = lg
                    halves.append(lg)
                mx_ref[h] = jnp.maximum(mx_ref[h], jnp.maximum(halves[0], halves[1]))

    _for_blocks(i, lambda j, slot: pass_a(j, False))
    pass_a(i, True)

    for h in range(H_B):
        mx_ref[h] = jnp.broadcast_to(jnp.max(mx_ref[h], axis=1, keepdims=True), (TQ, LANES))

    def pass_b(j, slot):
        for h in range(H_B):
            lg = lg_ref[h, j]
            m = mx_ref[h]
            p = jnp.exp2(lg - jnp.concatenate([m, m], axis=1)).astype(BF16)
            acc_ref[h * TQ:(h + 1) * TQ, :] += jnp.dot(
                p, bv_ref[0, h, blk(j), :], preferred_element_type=F32)

    _for_blocks(i + 1, pass_b)

    for p in range(4):
        out_ref[0, :, LANES * p:LANES * (p + 1)] = _normalize_pair(
            acc_ref[(2 * p) * TQ:(2 * p + 1) * TQ, :], acc_ref[(2 * p + 1) * TQ:(2 * p + 2) * TQ, :], low).astype(BF16)


def _fox_prompt(bq, bk, bv, fkt):
    b, t, _ = bq.shape
    qrow = lambda c: pl.BlockSpec((1, TQ, c), lambda i, j: (i, j, 0))
    slabs = lambda n: pl.BlockSpec((1, n, t, LANES), lambda i, j: (i, 0, 0, 0))
    return pl.pallas_call(
        _fox_prompt_kernel,
        grid=(b, t // TQ),
        in_specs=[qrow(512), slabs(4), slabs(H_B),
                  pl.BlockSpec((1, H_B, t), lambda i, j: (i, 0, 0))],
        out_specs=qrow(512),
        out_shape=jax.ShapeDtypeStruct((b, t, 512), BF16),
        scratch_shapes=[
            pltpu.VMEM((H_B * TQ, LANES), BF16),
            pltpu.VMEM((H_B, TQ, LANES), F32),
            pltpu.VMEM((H_B, t // TK, TQ, TK), F32),
            pltpu.VMEM((H_B, TQ, LANES), F32),
            pltpu.VMEM((H_B * TQ, LANES), F32),
        ],
        compiler_params=_cparams("parallel", "parallel"),
        name="fox_prompt",
    )(bq, bk, bv, fkt)


FF_BLK = 1024


def _tail_kernel(x_ref, oa_ref, ob_ref, woa_ref, wob_ref, gm_ref, wup_ref, wdn_ref, gf_ref, y_ref):
    mixed = (jnp.dot(oa_ref[...], woa_ref[...], preferred_element_type=F32)
             + jnp.dot(ob_ref[...], wob_ref[...], preferred_element_type=F32))
    x1 = x_ref[...] + mixed
    h2 = _rms(x1, gm_ref[...]).astype(BF16)
    acc = x1
    for c in range(D_FF // FF_BLK):
        u = jnp.dot(h2, wup_ref[:, c * FF_BLK:(c + 1) * FF_BLK], preferred_element_type=F32)
        u = jnp.square(jnp.maximum(u, 0.0)).astype(BF16)
        acc = acc + jnp.dot(u, wdn_ref[c * FF_BLK:(c + 1) * FF_BLK, :], preferred_element_type=F32)
    y_ref[...] = _rms(acc, gf_ref[...])


def _tail(x, oa, ob, woa, wob, g_mlp, w_up, w_down, g_final, tm):
    n, d = x.shape
    row = lambda c: pl.BlockSpec((tm, c), lambda i: (i, 0))
    return pl.pallas_call(
        _tail_kernel,
        grid=(n // tm,),
        in_specs=[row(d), row(512), row(512), _resident((512, d)), _resident((512, d)), _resident((1, d)),
                  _resident((d, D_FF)), _resident((D_FF, d)), _resident((1, d))],
        out_specs=row(d),
        out_shape=jax.ShapeDtypeStruct((n, d), F32),
        compiler_params=_cparams("parallel"),
        name="tail",
    )(x, oa, ob, woa, wob, g_mlp, w_up, w_down, g_final)


def _dsa_sample_kernel(aq_ref, iq_ref, iw_ref, ik2_ref, akv_ref, cikt_ref, cakt_ref, cavt_ref, bias_ref, out_ref,
                       *, past, n_new, l_pad, k_top, group):
    l_all = past + n_new
    n_tail = l_pad - past
    lane = lax.broadcasted_iota(I32, (n_new, LANES), 1)
    low = lane < HEAD_DIM

    def pad_rows(x):
        return jnp.concatenate([x, jnp.zeros((n_tail - n_new, x.shape[1]), F32)], axis=0).astype(BF16)

    scores = []
    for g in range(group):
        ik_new = pad_rows(ik2_ref[g].astype(F32)[:, 0:D_IDX])
        iq32 = iq_ref[g].astype(F32)
        qi = jnp.concatenate([iq32[:, D_IDX * h:D_IDX * (h + 1)] for h in range(H_IDX)], axis=0).astype(BF16)
        d = jnp.concatenate(
            [jnp.dot(qi, cikt_ref[g].astype(BF16), preferred_element_type=F32),
             lax.dot_general(qi, ik_new, NT_DIMS, preferred_element_type=F32)], axis=1)
        iw = iw_ref[g]
        score = jnp.zeros((n_new, l_pad), F32)
        for h in range(H_IDX):
            score = score + iw[:, h:h + 1] * jnp.maximum(d[h * n_new:(h + 1) * n_new, :], 0.0)
        scores.append(score)
    score = jnp.concatenate(scores, axis=0)

    rows = group * n_new
    row = lax.broadcasted_iota(I32, (rows, l_pad), 0) % n_new
    col = lax.broadcasted_iota(I32, (rows, l_pad), 1)
    adm = (col < l_all) & ((col // CHUNK) <= ((past + row) // CHUNK))
    sc = jnp.where(adm, score, -jnp.inf)

    def count(mask):
        return jnp.sum(jnp.where(mask, 1.0, 0.0), axis=1, keepdims=True)

    kf = float(k_top)
    ans_key = jnp.where(count(sc >= 0.0) >= kf, 0, INT_MIN).astype(I32)

    def bit_body(t, ans_key):
        cand_key = ans_key | lax.shift_left(jnp.int32(1), 30 - t)
        return jnp.where(count(sc >= _key_to_float(cand_key)) >= kf, cand_key, ans_key)

    ans_key = lax.fori_loop(0, 31, bit_body, ans_key)
    ans = jnp.where(ans_key > KEY_NEG_INF, _key_to_float(ans_key), -jnp.inf)
    eq = sc == ans
    need = kf - count(sc > ans)
    tie = (count(eq) > need) & (ans > -jnp.inf)
    flag = jnp.max(jnp.where(tie, 1.0, 0.0), axis=0, keepdims=True)
    nbits = int(l_pad - 1).bit_length()

    def tie_search(_):
        def tie_body(t, m):
            cand = m | lax.shift_left(jnp.int32(1), nbits - 1 - t)
            return jnp.where(count(eq & (col < cand)) < need, cand, m)
        m = lax.fori_loop(0, nbits, tie_body, jnp.zeros((rows, 1), I32))
        return jnp.where(tie, m, l_pad)

    thr = lax.cond(flag[0, 0] > 0.0, tie_search, lambda _: jnp.full((rows, 1), l_pad, I32), 0)
    sel = ((sc > ans) | (eq & (col <= thr))) & (sc > -jnp.inf)
    selb_all = jnp.where(sel, 0.0, NEG)

    for g in range(group):
        akv = akv_ref[g].astype(F32)
        ak_new = pad_rows(akv[:, 0:LANES])
        av_new = pad_rows(jnp.where(low, akv[:, LANES:2 * LANES], akv[:, 2 * LANES:3 * LANES]))
        selb = selb_all[g * n_new:(g + 1) * n_new, :]
        aq32 = aq_ref[g].astype(F32)
        slabs = [aq32[:, LANES * p:LANES * (p + 1)] for p in range(4)]
        qa = jnp.concatenate([jnp.where(low, s, 0.0) for s in slabs] + [jnp.where(low, 0.0, s) for s in slabs],
                             axis=0).astype(BF16)
        s = jnp.concatenate(
            [jnp.dot(qa, cakt_ref[g].astype(BF16), preferred_element_type=F32),
             lax.dot_general(qa, ak_new, NT_DIMS, preferred_element_type=F32)], axis=1)
        lg = s + bias_ref[...] + jnp.concatenate([selb] * H_A, axis=0)
        m = jnp.max(lg, axis=1, keepdims=True)
        p = jnp.exp2(lg - m)
        l = jnp.sum(p, axis=1, keepdims=True)
        pb = p.astype(BF16)
        o = (lax.dot_general(pb[:, 0:past], cavt_ref[g].astype(BF16), NT_DIMS, preferred_element_type=F32)
             + jnp.dot(pb[:, past:l_pad], av_new, preferred_element_type=F32)) / l
        for q in range(4):
            out_ref[g, :, LANES * q:LANES * (q + 1)] = jnp.where(
                low, o[q * n_new:(q + 1) * n_new, :], o[(q + 4) * n_new:(q + 5) * n_new, :]).astype(BF16)


SAMPLE_GROUP = 4


def _dsa_sample(aq, iq, iw, ik2, akv, cikt, cakt, cavt, bias_rows, past):
    b, n_new, _ = aq.shape
    l_pad = bias_rows.shape[1]
    assert past % LANES == 0 and l_pad == past + LANES and n_new <= LANES and b % SAMPLE_GROUP == 0
    k_top = min(K_TOP_MAX, (past + n_new) // 4)
    new = lambda c: pl.BlockSpec((SAMPLE_GROUP, n_new, c), lambda i: (i, 0, 0))
    old = lambda c: pl.BlockSpec((SAMPLE_GROUP, c, past), lambda i: (i, 0, 0))
    kern = functools.partial(_dsa_sample_kernel, past=past, n_new=n_new, l_pad=l_pad, k_top=k_top,
                             group=SAMPLE_GROUP)
    return pl.pallas_call(
        kern,
        grid=(b // SAMPLE_GROUP,),
        in_specs=[new(512), new(512), new(8), new(128), new(384), old(D_IDX), old(LANES), old(LANES),
                  _resident((H_A * n_new, l_pad))],
        out_specs=new(512),
        out_shape=jax.ShapeDtypeStruct((b, n_new, 512), BF16),
        compiler_params=_cparams("parallel"),
        name="dsa_sample",
    )(aq, iq, iw, ik2, akv, cikt, cakt, cavt, bias_rows)


def _fox_sample_kernel(bq_ref, bkn_ref, bvn_ref, ckt_ref, cvt_ref, fkt_ref, tot_ref, lf_ref, lft_ref, out_ref,
                       *, past, n_new):
    lf = lf_ref[0] * LOG2E
    lft = lft_ref[0] * LOG2E
    rown = lax.broadcasted_iota(I32, (n_new, H_B), 0)
    lanen = lax.broadcasted_iota(I32, (H_B, n_new), 1)
    fq = jnp.broadcast_to(tot_ref[0], (n_new, H_B))
    fqt = jnp.broadcast_to(fkt_ref[0][:, past - 1:past], (H_B, n_new))
    for s in range(n_new):
        fq = fq + jnp.where(rown >= s, lf[s:s + 1, :], 0.0)
        fqt = fqt + jnp.where(lanen >= s, lft[:, s:s + 1], 0.0)

    kct = ckt_ref[0].astype(BF16)
    vct = cvt_ref[0].astype(BF16)
    bq32 = bq_ref[0].astype(F32)
    head_of_lane = lax.broadcasted_iota(I32, (n_new, H_B * HEAD_DIM), 1) // HEAD_DIM
    qb = jnp.concatenate([jnp.where(head_of_lane == h, bq32, 0.0) for h in range(H_B)], axis=0).astype(BF16)
    s_past = jnp.dot(qb, kct, preferred_element_type=F32)
    s_new = lax.dot_general(qb, bkn_ref[0], NT_DIMS, preferred_element_type=F32)
    fkt = fkt_ref[0]
    fq_col = jnp.concatenate([fq[:, h:h + 1] for h in range(H_B)], axis=0)
    fk_past = jnp.concatenate([jnp.broadcast_to(fkt[h:h + 1, :], (n_new, past)) for h in range(H_B)], axis=0)
    fk_new = jnp.concatenate([jnp.broadcast_to(fqt[h:h + 1, :], (n_new, n_new)) for h in range(H_B)], axis=0)
    lg_past = s_past + (fq_col - fk_past)
    trow = lax.broadcasted_iota(I32, (H_B * n_new, n_new), 0) % n_new
    tcol = lax.broadcasted_iota(I32, (H_B * n_new, n_new), 1)
    lg_new = jnp.where(tcol <= trow, s_new + (fq_col - fk_new), NEG)
    m = jnp.maximum(jnp.max(lg_past, axis=1, keepdims=True), jnp.max(lg_new, axis=1, keepdims=True))
    p_past = jnp.exp2(lg_past - m)
    p_new = jnp.exp2(lg_new - m)
    l = jnp.sum(p_past, axis=1, keepdims=True) + jnp.sum(p_new, axis=1, keepdims=True)
    o = (lax.dot_general(p_past.astype(BF16), vct, NT_DIMS, preferred_element_type=F32)
         + jnp.dot(p_new.astype(BF16), bvn_ref[0].astype(BF16), preferred_element_type=F32)) / l
    out = jnp.zeros((n_new, H_B * HEAD_DIM), F32)
    for h in range(H_B):
        out = out + jnp.where(head_of_lane == h, o[h * n_new:(h + 1) * n_new, :], 0.0)
    out_ref[0] = out.astype(BF16)


def _fox_sample(bq, bkn, bvn, ckt, cvt, fkt, tot, lf, lft, past):
    b, n_new, _ = bq.shape
    new = lambda c: pl.BlockSpec((1, n_new, c), lambda i: (i, 0, 0))
    old = pl.BlockSpec((1, H_B * HEAD_DIM, past), lambda i: (i, 0, 0))
    kern = functools.partial(_fox_sample_kernel, past=past, n_new=n_new)
    return pl.pallas_call(
        kern,
        grid=(b,),
        in_specs=[new(512), new(512), new(512), old, old,
                  pl.BlockSpec((1, H_B, past), lambda i: (i, 0, 0)),
                  pl.BlockSpec((1, 1, H_B), lambda i: (i, 0, 0)),
                  new(H_B),
                  pl.BlockSpec((1, H_B, n_new), lambda i: (i, 0, 0))],
        out_specs=new(512),
        out_shape=jax.ShapeDtypeStruct((b, n_new, 512), BF16),
        compiler_params=_cparams("parallel"),
        name="fox_sample",
    )(bq, bkn, bvn, ckt, cvt, fkt, tot, lf, lft)


def _prep_weights(w_in, w_o, w_up, w_down):
    offs = np.concatenate([[0], np.cumsum(IN_SPLITS)])
    seg = lambda k: w_in[:, int(offs[k]):int(offs[k + 1])]
    a_q, a_k, a_v, i_q, i_k, i_w, b_q, b_k, b_v, f_z = (seg(k) for k in range(10))
    a_q_pairs = jnp.concatenate([a_q[:, HEAD_DIM * h:HEAD_DIM * (h + 1)] for h in AQ_HEAD_ORDER], axis=1)
    pad = jnp.zeros((w_in.shape[0], N_CAT - C_SM - 16), w_in.dtype)
    w_cat = jnp.concatenate([a_q_pairs, i_q, b_q, a_k, a_v, b_k, b_v, i_k, i_k, i_w, f_z, pad], axis=1)
    wo_a = jnp.concatenate([w_o[HEAD_DIM * h:HEAD_DIM * (h + 1)] for h in AQ_HEAD_ORDER], axis=0)
    wo_b = w_o[H_A * HEAD_DIM:]
    return (w_cat.astype(BF16), wo_a.astype(BF16), wo_b.astype(BF16), w_up.astype(BF16), w_down.astype(BF16))


def kernel(x_prompt, x_sample, cache_a_k, cache_a_v, cache_idx_k, cache_b_k, cache_b_v, cache_b_logf,
           w_in, w_o, b_f, rel_bias, g_attn, w_up, w_down, g_mlp, g_final):
    assert w_in.shape[0] == 1, "single-layer trunk"
    bp, tp, d = x_prompt.shape
    bs, ts, _ = x_sample.shape
    past = cache_a_k.shape[2]
    w_cat, wo_a, wo_b, wup, wdn = _prep_weights(w_in[0], w_o[0], w_up[0], w_down[0])
    g_a = g_attn[0].reshape(1, d)
    g_m = g_mlp[0].reshape(1, d)
    g_f = g_final.reshape(1, d)
    bf = b_f[0].reshape(H_B, 1)
    rel_bias = rel_bias.astype(F32)

    (aq, iq, bq, akv, bk, bv, ik2, ak32, av32, bk32, bv32, iw, ikt32, logft) = _inproj(x_prompt, g_a, w_cat, bf, 512)
    fkt = _cumsum_lanes(logft)
    r = np.arange(TQ)[:, None]
    c = np.arange(TK)[None, :]
    bucket = _t5_bucket_np(np.stack([c - TK - r, c - r]))
    far_bucket = int(_t5_bucket_np(np.array(-TK - 1)))
    assert far_bucket == int(_t5_bucket_np(np.array(-tp)))
    bias_tiles = _bias_tiles(rel_bias, jnp.asarray(bucket))
    out_a = _dsa_prompt(rel_bias, aq, iq, iw, ik2, akv, bias_tiles, far_bucket)
    out_b = _fox_prompt(bq, bk, bv, fkt)
    y_p = _tail(x_prompt.reshape(bp * tp, d), out_a.reshape(bp * tp, 512), out_b.reshape(bp * tp, 512),
                wo_a, wo_b, g_m, wup, wdn, g_f, 512).reshape(bp, tp, d)

    n_s = bs * ts
    outs = _inproj(x_sample.reshape(1, n_s, d), g_a, w_cat, bf, n_s)
    (aq_s, iq_s, bq_s, akv_s, ik2_s, ak32_s, av32_s, bk32_s, bv32_s, iw_s) = (
        outs[k].reshape(bs, ts, outs[k].shape[-1]) for k in (0, 1, 2, 3, 6, 7, 8, 9, 10, 11))
    bk_s = jnp.swapaxes(outs[4][0], 0, 1).reshape(bs, ts, H_B * HEAD_DIM)
    ik32_s = jnp.swapaxes(outs[12][0], 0, 1).reshape(bs, ts, D_IDX)
    logf_s = jnp.swapaxes(outs[13][0], 0, 1).reshape(bs, ts, H_B)
    l_all = past + ts
    l_pad = -(-l_all // LANES) * LANES
    rel_s = np.arange(l_pad)[None, :] - (past + np.arange(ts))[:, None]
    bias_s = _bias_tiles(rel_bias, jnp.asarray(_t5_bucket_np(rel_s))[None])[0].reshape(H_A * ts, l_pad)
    def feature_major(cache):
        c = cache[0]
        c = jnp.transpose(c, (0, 2, 3, 1)) if c.ndim == 4 else jnp.transpose(c, (0, 2, 1))
        return c.reshape(bs, -1, past)

    out_a_s = _dsa_sample(aq_s, iq_s, iw_s, ik2_s, akv_s, feature_major(cache_idx_k),
                          feature_major(cache_a_k), feature_major(cache_a_v), bias_s, past)
    fkt_c = _cumsum_lanes(jnp.swapaxes(cache_b_logf[0].astype(F32), 1, 2))
    tot = fkt_c[:, :, past - 1].reshape(bs, 1, H_B)
    out_b_s = _fox_sample(bq_s, bk_s, bv32_s, feature_major(cache_b_k), feature_major(cache_b_v), fkt_c, tot,
                          logf_s, jnp.swapaxes(logf_s, 1, 2), past)
    y_s = _tail(x_sample.reshape(n_s, d), out_a_s.reshape(n_s, 512), out_b_s.reshape(n_s, 512),
                wo_a, wo_b, g_m, wup, wdn, g_f, n_s).reshape(bs, ts, d)

    def rows(a, heads, b, t):
        return a.reshape(1, b, t, heads, HEAD_DIM)

    return (y_p, y_s,
            rows(ak32, HKV_A, bp, tp), rows(av32, HKV_A, bp, tp), jnp.swapaxes(ikt32, 1, 2)[None],
            rows(bk32, H_B, bp, tp), rows(bv32, H_B, bp, tp), jnp.swapaxes(logft, 1, 2)[None],
            rows(ak32_s, HKV_A, bs, ts), rows(av32_s, HKV_A, bs, ts), ik32_s.reshape(1, bs, ts, D_IDX),
            rows(bk32_s, H_B, bs, ts), rows(bv32_s, H_B, bs, ts), logf_s.reshape(1, bs, ts, H_B))
```

```python
import functools
import math

import numpy as np
import jax
import jax.numpy as jnp
from jax import lax
from jax.experimental import pallas as pl
from jax.experimental.pallas import tpu as pltpu

F32 = jnp.float32
BF16 = jnp.bfloat16
I32 = jnp.int32

D_MODEL = 1024
CHUNK = 64
HEAD_DIM = 64
H_A = 8
HKV_A = 2
H_B = 8
H_IDX = 8
D_IDX = 64
K_TOP_MAX = 256
NUM_BUCKETS = 32
MAX_DISTANCE = 128
D_FF = 4 * D_MODEL
EPS = 1e-6
IN_SPLITS = (H_A * HEAD_DIM, HKV_A * HEAD_DIM, HKV_A * HEAD_DIM, H_IDX * D_IDX, D_IDX, H_IDX,
             H_B * HEAD_DIM, H_B * HEAD_DIM, H_B * HEAD_DIM, H_B)

LANES = 128
SUBLANES = 8
VMEM_LIMIT = 56 * 1024 * 1024

TQ = 256
TK = 256
NEG = -1e30
LOG2E = 1.4426950408889634
KEY_NEG_INF = -2139095041
INT_MIN = -2147483648

C_AQ, C_IQ, C_BQ, C_AKV, C_BK, C_BV, C_IK2, C_SM, N_CAT = 0, 512, 1024, 1536, 1792, 2304, 2816, 2944, 3072
AQ_HEAD_ORDER = (0, 4, 1, 5, 2, 6, 3, 7)

NT_DIMS = (((1,), (1,)), ((), ()))


def _cparams(*sem):
    return pltpu.CompilerParams(dimension_semantics=sem, vmem_limit_bytes=VMEM_LIMIT)


def _resident(shape):
    nd = len(shape)
    return pl.BlockSpec(shape, lambda *_: (0,) * nd, pipeline_mode=pl.Buffered(1))


def _rms(x, g):
    ms = jnp.mean(x * x, axis=-1, keepdims=True)
    return (x * lax.rsqrt(ms + EPS)) * g


def _for_blocks(n, body):
    def pair(k, c):
        body(2 * k, 0)
        body(2 * k + 1, 1)
        return c

    lax.fori_loop(0, n // 2, pair, 0)

    @pl.when(n % 2 == 1)
    def _():
        body(n - 1, 0)


def _normalize_pair(a_lo, a_hi, low):
    num = jnp.where(low, a_lo, a_hi)
    den = pltpu.roll(jnp.where(low, a_hi, a_lo), HEAD_DIM, axis=1)
    return num / den


def _key_to_float(k):
    return lax.bitcast_convert_type(k ^ (lax.shift_right_arithmetic(k, 31) & 0x7FFFFFFF), F32)


def _t5_bucket_np(rel):
    half = NUM_BUCKETS // 2
    max_exact = half // 2
    ret = np.where(rel > 0, half, 0)
    n = np.abs(rel)
    n_f = np.maximum(n, max_exact).astype(np.float64)
    large = max_exact + (np.log(n_f / max_exact) / math.log(MAX_DISTANCE / max_exact)
                         * (half - max_exact)).astype(np.int32)
    large = np.minimum(large, half - 1)
    return (ret + np.where(n < max_exact, n, large)).astype(np.int32)


def _inproj_kernel(x_ref, g_ref, w_ref, bf_ref,
                   aq_ref, iq_ref, bq_ref, akv_ref, bk_ref, bv_ref, ik2_ref,
                   ak32_ref, av32_ref, bk32_ref, bv32_ref, iw_ref, ikt_ref, logft_ref):
    h = _rms(x_ref[0], g_ref[...])
    p = jnp.dot(h.astype(BF16), w_ref[...], preferred_element_type=F32)
    qscale = HEAD_DIM ** -0.5 * LOG2E
    aq_ref[0] = (p[:, C_AQ:C_AQ + 512] * qscale).astype(BF16)
    iq_ref[0] = (p[:, C_IQ:C_IQ + 512] * (D_IDX ** -0.5)).astype(BF16)
    bq_ref[0] = (p[:, C_BQ:C_BQ + 512] * qscale).astype(BF16)
    low = lax.broadcasted_iota(I32, (p.shape[0], LANES), 1) < HEAD_DIM
    av = p[:, C_AKV + LANES:C_AKV + 2 * LANES]
    akv_ref[0, :, 0:LANES] = p[:, C_AKV:C_AKV + LANES].astype(BF16)
    akv_ref[0, :, LANES:2 * LANES] = jnp.where(low, av, 1.0).astype(BF16)
    akv_ref[0, :, 2 * LANES:3 * LANES] = jnp.where(low, 1.0, av).astype(BF16)
    for q in range(4):
        bk_ref[0, q] = p[:, C_BK + LANES * q:C_BK + LANES * (q + 1)].astype(BF16)
        bv = p[:, C_BV + LANES * q:C_BV + LANES * (q + 1)]
        bv_ref[0, 2 * q] = jnp.where(low, bv, 1.0).astype(BF16)
        bv_ref[0, 2 * q + 1] = jnp.where(low, 1.0, bv).astype(BF16)
    ik2_ref[0] = p[:, C_IK2:C_IK2 + 128].astype(BF16)
    ak32_ref[0] = p[:, C_AKV:C_AKV + 128]
    av32_ref[0] = p[:, C_AKV + 128:C_AKV + 256]
    bk32_ref[0] = p[:, C_BK:C_BK + 512]
    bv32_ref[0] = p[:, C_BV:C_BV + 512]
    iw_ref[0] = p[:, C_SM:C_SM + 8] * (H_IDX ** -0.5)
    ikt_ref[0] = p[:, C_IK2:C_IK2 + LANES].T[0:D_IDX, :]
    z = p[:, C_SM:C_SM + LANES].T[8:16, :] + bf_ref[...]
    logft_ref[0] = jnp.minimum(z, 0.0) - jnp.log1p(jnp.exp(-jnp.abs(z)))


def _inproj(x, g, w_cat, b_f, tm):
    b, t, d = x.shape
    grid = (b, t // tm)
    row = lambda c: pl.BlockSpec((1, tm, c), lambda i, j: (i, j, 0))
    col = lambda c: pl.BlockSpec((1, c, tm), lambda i, j: (i, 0, j))
    slab = lambda n: pl.BlockSpec((1, n, tm, LANES), lambda i, j: (i, 0, j, 0))
    outs = [(512, BF16), (512, BF16), (512, BF16), (384, BF16), (4, BF16), (8, BF16), (128, BF16),
            (128, F32), (128, F32), (512, F32), (512, F32), (8, F32)]
    slab_outs = (4, 5)
    outs_t = [D_IDX, H_B]
    return pl.pallas_call(
        _inproj_kernel,
        grid=grid,
        in_specs=[row(d), _resident((1, d)), _resident((d, N_CAT)), _resident((H_B, 1))],
        out_specs=([slab(c) if k in slab_outs else row(c) for k, (c, _) in enumerate(outs)]
                   + [col(c) for c in outs_t]),
        out_shape=([jax.ShapeDtypeStruct((b, c, t, LANES) if k in slab_outs else (b, t, c), dt)
                    for k, (c, dt) in enumerate(outs)]
                   + [jax.ShapeDtypeStruct((b, c, t), F32) for c in outs_t]),
        compiler_params=_cparams("parallel", "parallel"),
        name="inproj",
    )(x, g, w_cat, b_f)


CS_BLK = 512


def _cumsum_kernel(x_ref, out_ref, carry_ref):
    @pl.when(pl.program_id(0) == 0)
    def _():
        carry_ref[...] = jnp.zeros_like(carry_ref)

    x = x_ref[...] * LOG2E
    hi = x.astype(BF16)
    r1 = x - hi.astype(F32)
    mid = r1.astype(BF16)
    lo = (r1 - mid.astype(F32)).astype(BF16)
    src = lax.broadcasted_iota(I32, (CS_BLK, CS_BLK), 0)
    dst = lax.broadcasted_iota(I32, (CS_BLK, CS_BLK), 1)
    tri = jnp.where(src <= dst, 1.0, 0.0).astype(BF16)
    cs = (jnp.dot(hi, tri, preferred_element_type=F32)
          + jnp.dot(mid, tri, preferred_element_type=F32)
          + jnp.dot(lo, tri, preferred_element_type=F32))
    out = cs + carry_ref[...]
    out_ref[...] = out
    carry_ref[...] = jnp.broadcast_to(out[:, CS_BLK - 1:CS_BLK], carry_ref.shape)


def _cumsum_lanes(x_t):
    b, h, l = x_t.shape
    spec = pl.BlockSpec((b * h, CS_BLK), lambda j: (0, j))
    return pl.pallas_call(
        _cumsum_kernel,
        grid=(l // CS_BLK,),
        in_specs=[spec],
        out_specs=spec,
        out_shape=jax.ShapeDtypeStruct((b * h, l), F32),
        scratch_shapes=[pltpu.VMEM((b * h, CS_BLK), F32)],
        compiler_params=_cparams("arbitrary"),
        name="cumsum",
    )(x_t.reshape(b * h, l)).reshape(b, h, l)


def _bias_kernel(rb_ref, bucket_ref, out_ref):
    b = bucket_ref[0]
    for h in range(H_A):
        acc = jnp.zeros(b.shape, F32)
        for k in range(NUM_BUCKETS):
            acc = jnp.where(b == k, rb_ref[k, h] * LOG2E, acc)
        out_ref[0, h] = acc


def _bias_tiles(rel_bias, bucket):
    n, r, c = bucket.shape
    return pl.pallas_call(
        _bias_kernel,
        grid=(n,),
        in_specs=[pl.BlockSpec(memory_space=pltpu.SMEM),
                  pl.BlockSpec((1, r, c), lambda i: (i, 0, 0))],
        out_specs=pl.BlockSpec((1, H_A, r, c), lambda i: (i, 0, 0, 0)),
        out_shape=jax.ShapeDtypeStruct((n, H_A, r, c), F32),
        compiler_params=_cparams("parallel"),
        name="t5_bias",
    )(rel_bias, bucket)


def _dsa_prompt_kernel(rb_ref, aq_ref, iq_ref, iw_ref, ik2_ref, akv_ref, bias_ref, out_ref,
                       qi_ref, qa_ref, iwrep_ref, sc_ref, sct_ref, lg_ref, mx_ref,
                       acc_ref, thr_ref, ans_ref, *, k_top, far_bucket, seq_len):
    i = pl.program_id(1)
    nkb = i + 1
    lane = lax.broadcasted_iota(I32, (TQ, LANES), 1)
    low = lane < HEAD_DIM

    for p in range(4):
        s_i = iq_ref[0, :, LANES * p:LANES * (p + 1)].astype(F32)
        qi_ref[(2 * p) * TQ:(2 * p + 1) * TQ, :] = jnp.where(low, s_i, 0.0).astype(BF16)
        qi_ref[(2 * p + 1) * TQ:(2 * p + 2) * TQ, :] = jnp.where(low, 0.0, s_i).astype(BF16)
        s_a = aq_ref[0, :, LANES * p:LANES * (p + 1)].astype(F32)
        qa_ref[p * TQ:(p + 1) * TQ, :] = jnp.where(low, s_a, 0.0).astype(BF16)
        qa_ref[(p + 4) * TQ:(p + 5) * TQ, :] = jnp.where(low, 0.0, s_a).astype(BF16)
    iw = iw_ref[0]
    for h in range(H_IDX):
        iwrep_ref[h] = jnp.broadcast_to(iw[:, h:h + 1], (TQ, LANES))

    def blk(j):
        return pl.ds(pl.multiple_of(j * TK, TK), TK)

    r2 = lax.broadcasted_iota(I32, (TQ, TK), 0)
    c2 = lax.broadcasted_iota(I32, (TQ, TK), 1)

    def score_blk(j, slot, diagonal=False):
        kb = ik2_ref[0, blk(j), :]
        acc = jnp.zeros((TQ, TK), F32)
        for h in range(H_IDX):
            d = lax.dot_general(qi_ref[h * TQ:(h + 1) * TQ, :], kb, NT_DIMS, preferred_element_type=F32)
            w = iwrep_ref[h]
            acc = acc + jnp.concatenate([w, w], axis=1) * jnp.maximum(d, 0.0)
        if diagonal:
            acc = jnp.where((c2 // CHUNK) <= (r2 // CHUNK), acc, -jnp.inf)
        sc_ref[j] = acc
        sct_ref[blk(j), :] = acc.T

    _for_blocks(i, score_blk)
    score_blk(i, 0, diagonal=True)

    sub = lax.broadcasted_iota(I32, (SUBLANES, TQ), 0)
    kf = float(k_top)

    def search(n_blocks):
        def count(pred):
            accs = [jnp.zeros((SUBLANES, TQ), F32)] * 4
            for j in range(n_blocks):
                for g in range(TK // SUBLANES):
                    r0 = j * TK + SUBLANES * g
                    hit = jnp.where(pred(sct_ref[r0:r0 + SUBLANES, :], r0 + sub), 1.0, 0.0)
                    accs[g % 4] = accs[g % 4] + hit
            tot = jnp.sum((accs[0] + accs[1]) + (accs[2] + accs[3]), axis=0, keepdims=True)
            return jnp.broadcast_to(tot, (SUBLANES, TQ))

        c0 = count(lambda s, idx: s >= 0.0)
        ans_key = jnp.where(c0 >= kf, 0, INT_MIN).astype(I32)

        def bit_body(t, carry):
            ans_key, cge = carry
            cand_key = ans_key | lax.shift_left(jnp.int32(1), 30 - t)
            cand = _key_to_float(cand_key)
            c = count(lambda s, idx: s >= cand)
            accept = c >= kf
            return jnp.where(accept, cand_key, ans_key), jnp.where(accept, c, cge)

        ans_key, cge = lax.fori_loop(0, 31, bit_body, (ans_key, c0))
        ans = jnp.where(ans_key > KEY_NEG_INF, _key_to_float(ans_key), -jnp.inf)
        ans_ref[...] = ans

        cgt = count(lambda s, idx: s > ans)
        need = kf - cgt
        tie = ((cge - cgt) > need) & (ans > -jnp.inf)
        flag = jnp.max(jnp.where(tie[0:1, :], 1.0, 0.0), axis=1, keepdims=True)
        thr_ref[...] = jnp.full((SUBLANES, TQ), seq_len, I32)

        @pl.when(flag[0, 0] > 0.0)
        def _():
            nbits = int(seq_len - 1).bit_length()

            def tie_body(t, m):
                cand = m | lax.shift_left(jnp.int32(1), nbits - 1 - t)
                c = count(lambda s, idx: (s == ans) & (idx < cand))
                return jnp.where(c < need, cand, m)

            m = lax.fori_loop(0, nbits, tie_body, jnp.zeros((SUBLANES, TQ), I32))
            thr_ref[...] = jnp.where(tie, m, seq_len)

    for n in range(seq_len // TQ):
        pl.when(i == n)(functools.partial(search, n + 1))

    def to_rows(x):
        return jnp.broadcast_to(x[0:1, :], (LANES, TQ)).T

    ans_r = to_rows(ans_ref[...])
    thr_r = to_rows(thr_ref[...])

    def select_bias(j):
        sc = sc_ref[j]
        col = j * TK + lane
        halves = []
        for half in range(2):
            sh = sc[:, half * LANES:(half + 1) * LANES]
            ch = col + half * LANES
            sel = ((sh > ans_r) | ((sh == ans_r) & (ch <= thr_r))) & (sh > -jnp.inf)
            halves.append(jnp.where(sel, 0.0, NEG))
        return jnp.concatenate(halves, axis=1)

    for h in range(H_A):
        mx_ref[h] = jnp.full((TQ, LANES), NEG, F32)
    acc_ref[...] = jnp.zeros_like(acc_ref)

    def pass_a(j, bias_of_head):
        kb = akv_ref[0, blk(j), 0:LANES]
        sb = select_bias(j)
        for h in range(H_A):
            s = lax.dot_general(qa_ref[h * TQ:(h + 1) * TQ, :], kb, NT_DIMS, preferred_element_type=F32)
            lg = s + sb + bias_of_head(h)
            lg_ref[h, j] = lg
            mx_ref[h] = jnp.maximum(mx_ref[h], jnp.maximum(lg[:, :LANES], lg[:, LANES:]))

    _for_blocks(jnp.maximum(i - 1, 0), lambda j, slot: pass_a(j, lambda h: rb_ref[far_bucket, h] * LOG2E))

    @pl.when(i >= 1)
    def _():
        pass_a(i - 1, lambda h: bias_ref[0, h])

    pass_a(i, lambda h: bias_ref[1, h])

    for h in range(H_A):
        mx_ref[h] = jnp.broadcast_to(jnp.max(mx_ref[h], axis=1, keepdims=True), (TQ, LANES))

    half = (H_A // 2) * TQ

    def pass_b(j, slot):
        for h in range(H_A):
            lg = lg_ref[h, j]
            m = mx_ref[h]
            p = jnp.exp2(lg - jnp.concatenate([m, m], axis=1)).astype(BF16)
            g = h // (H_A // HKV_A)
            acc_ref[h * TQ:(h + 1) * TQ, :] += jnp.dot(
                p, akv_ref[0, blk(j), (g + 1) * LANES:(g + 2) * LANES], preferred_element_type=F32)

    _for_blocks(nkb, pass_b)

    for p in range(4):
        out_ref[0, :, LANES * p:LANES * (p + 1)] = _normalize_pair(
            acc_ref[p * TQ:(p + 1) * TQ, :], acc_ref[(p + 4) * TQ:(p + 5) * TQ, :], low).astype(BF16)


def _dsa_prompt(rel_bias, aq, iq, iw, ik2, akv, bias_tiles, far_bucket):
    b, t, _ = aq.shape
    k_top = min(K_TOP_MAX, t // 4)
    qrow = lambda c: pl.BlockSpec((1, TQ, c), lambda i, j: (i, j, 0))
    full = lambda c: pl.BlockSpec((1, t, c), lambda i, j: (i, 0, 0))
    kern = functools.partial(_dsa_prompt_kernel, k_top=k_top, far_bucket=far_bucket, seq_len=t)
    return pl.pallas_call(
        kern,
        grid=(b, t // TQ),
        in_specs=[pl.BlockSpec(memory_space=pltpu.SMEM),
                  qrow(512), qrow(512), qrow(8), full(128), full(384),
                  _resident((2, H_A, TQ, TK))],
        out_specs=qrow(512),
        out_shape=jax.ShapeDtypeStruct((b, t, 512), BF16),
        scratch_shapes=[
            pltpu.VMEM((H_IDX * TQ, LANES), BF16),
            pltpu.VMEM((H_A * TQ, LANES), BF16),
            pltpu.VMEM((H_IDX, TQ, LANES), F32),
            pltpu.VMEM((t // TK, TQ, TK), F32),
            pltpu.VMEM((t, TQ), F32),
            pltpu.VMEM((H_A, t // TK, TQ, TK), F32),
            pltpu.VMEM((H_A, TQ, LANES), F32),
            pltpu.VMEM((H_A * TQ, LANES), F32),
            pltpu.VMEM((SUBLANES, TQ), I32),
            pltpu.VMEM((SUBLANES, TQ), F32),
        ],
        compiler_params=_cparams("parallel", "parallel"),
        name="dsa_prompt",
    )(rel_bias, aq, iq, iw, ik2, akv, bias_tiles)


def _fox_prompt_kernel(bq_ref, bk_ref, bv_ref, fkt_ref, out_ref,
                       qb_ref, fqrep_ref, lg_ref, mx_ref, acc_ref):
    i = pl.program_id(1)
    lane = lax.broadcasted_iota(I32, (TQ, LANES), 1)
    low = lane < HEAD_DIM

    for p in range(4):
        s_b = bq_ref[0, :, LANES * p:LANES * (p + 1)].astype(F32)
        qb_ref[(2 * p) * TQ:(2 * p + 1) * TQ, :] = jnp.where(low, s_b, 0.0).astype(BF16)
        qb_ref[(2 * p + 1) * TQ:(2 * p + 2) * TQ, :] = jnp.where(low, 0.0, s_b).astype(BF16)
    fq_t = fkt_ref[0, :, pl.ds(pl.multiple_of(i * TQ, TQ), TQ)]
    for h in range(H_B):
        fqrep_ref[h] = jnp.broadcast_to(fq_t[h:h + 1, :], (LANES, TQ)).T
        mx_ref[h] = jnp.full((TQ, LANES), NEG, F32)
    acc_ref[...] = jnp.zeros_like(acc_ref)

    def blk(j):
        return pl.ds(pl.multiple_of(j * TK, TK), TK)

    r2 = lax.broadcasted_iota(I32, (TQ, LANES), 0)

    def pass_a(j, diagonal):
        for p in range(4):
            kb = bk_ref[0, p, blk(j), :]
            for e in range(2):
                h = 2 * p + e
                s = lax.dot_general(qb_ref[h * TQ:(h + 1) * TQ, :], kb, NT_DIMS,
                                    preferred_element_type=F32)
                fk = fkt_ref[0, h:h + 1, blk(j)]
                fq_h = fqrep_ref[h]
                halves = []
                for half in range(2):
                    lg = (s[:, half * LANES:(half + 1) * LANES]
                          + (fq_h - fk[:, half * LANES:(half + 1) * LANES]))
                    if diagonal:
                        lg = jnp.where(lane + half * LANES <= r2, lg, NEG)
                    lg_ref[h, j, :, half * LANES:(half + 1) * LANES] = lg
                    halves.append(lg)
                mx_ref[h] = jnp.maximum(mx_ref[h], jnp.maximum(halves[0], halves[1]))

    _for_blocks(i, lambda j, slot: pass_a(j, False))
    pass_a(i, True)

    for h in range(H_B):
        mx_ref[h] = jnp.broadcast_to(jnp.max(mx_ref[h], axis=1, keepdims=True), (TQ, LANES))

    def pass_b(j, slot):
        for h in range(H_B):
            lg = lg_ref[h, j]
            m = mx_ref[h]
            p = jnp.exp2(lg - jnp.concatenate([m, m], axis=1)).astype(BF16)
            acc_ref[h * TQ:(h + 1) * TQ, :] += jnp.dot(
                p, bv_ref[0, h, blk(j), :], preferred_element_type=F32)

    _for_blocks(i + 1, pass_b)

    for p in range(4):
        out_ref[0, :, LANES * p:LANES * (p + 1)] = _normalize_pair(
            acc_ref[(2 * p) * TQ:(2 * p + 1) * TQ, :], acc_ref[(2 * p + 1) * TQ:(2 * p + 2) * TQ, :], low).astype(BF16)


def _fox_prompt(bq, bk, bv, fkt):
    b, t, _ = bq.shape
    qrow = lambda c: pl.BlockSpec((1, TQ, c), lambda i, j: (i, j, 0))
    slabs = lambda n: pl.BlockSpec((1, n, t, LANES), lambda i, j: (i, 0, 0, 0))
    return pl.pallas_call(
        _fox_prompt_kernel,
        grid=(b, t // TQ),
        in_specs=[qrow(512), slabs(4), slabs(H_B),
                  pl.BlockSpec((1, H_B, t), lambda i, j: (i, 0, 0))],
        out_specs=qrow(512),
        out_shape=jax.ShapeDtypeStruct((b, t, 512), BF16),
        scratch_shapes=[
            pltpu.VMEM((H_B * TQ, LANES), BF16),
            pltpu.VMEM((H_B, TQ, LANES), F32),
            pltpu.VMEM((H_B, t // TK, TQ, TK), F32),
            pltpu.VMEM((H_B, TQ, LANES), F32),
            pltpu.VMEM((H_B * TQ, LANES), F32),
        ],
        compiler_params=_cparams("parallel", "parallel"),
        name="fox_prompt",
    )(bq, bk, bv, fkt)


FF_BLK = 1024


def _tail_kernel(x_ref, oa_ref, ob_ref, woa_ref, wob_ref, gm_ref, wup_ref, wdn_ref, gf_ref, y_ref):
    mixed = (jnp.dot(oa_ref[...], woa_ref[...], preferred_element_type=F32)
             + jnp.dot(ob_ref[...], wob_ref[...], preferred_element_type=F32))
    x1 = x_ref[...] + mixed
    h2 = _rms(x1, gm_ref[...]).astype(BF16)
    acc = x1
    for c in range(D_FF // FF_BLK):
        u = jnp.dot(h2, wup_ref[:, c * FF_BLK:(c + 1) * FF_BLK], preferred_element_type=F32)
        u = jnp.square(jnp.maximum(u, 0.0)).astype(BF16)
        acc = acc + jnp.dot(u, wdn_ref[c * FF_BLK:(c + 1) * FF_BLK, :], preferred_element_type=F32)
    y_ref[...] = _rms(acc, gf_ref[...])


def _tail(x, oa, ob, woa, wob, g_mlp, w_up, w_down, g_final, tm):
    n, d = x.shape
    row = lambda c: pl.BlockSpec((tm, c), lambda i: (i, 0))
    return pl.pallas_call(
        _tail_kernel,
        grid=(n // tm,),
        in_specs=[row(d), row(512), row(512), _resident((512, d)), _resident((512, d)), _resident((1, d)),
                  _resident((d, D_FF)), _resident((D_FF, d)), _resident((1, d))],
        out_specs=row(d),
        out_shape=jax.ShapeDtypeStruct((n, d), F32),
        compiler_params=_cparams("parallel"),
        name="tail",
    )(x, oa, ob, woa, wob, g_mlp, w_up, w_down, g_final)


def _dsa_sample_kernel(aq_ref, iq_ref, iw_ref, ik2_ref, akv_ref, cikt_ref, cakt_ref, cavt_ref, bias_ref, out_ref,
                       *, past, n_new, l_pad, k_top, group):
    l_all = past + n_new
    n_tail = l_pad - past
    lane = lax.broadcasted_iota(I32, (n_new, LANES), 1)
    low = lane < HEAD_DIM

    def pad_rows(x):
        return jnp.concatenate([x, jnp.zeros((n_tail - n_new, x.shape[1]), F32)], axis=0).astype(BF16)

    scores = []
    for g in range(group):
        ik_new = pad_rows(ik2_ref[g].astype(F32)[:, 0:D_IDX])
        iq32 = iq_ref[g].astype(F32)
        qi = jnp.concatenate([iq32[:, D_IDX * h:D_IDX * (h + 1)] for h in range(H_IDX)], axis=0).astype(BF16)
        d = jnp.concatenate(
            [jnp.dot(qi, cikt_ref[g].astype(BF16), preferred_element_type=F32),
             lax.dot_general(qi, ik_new, NT_DIMS, preferred_element_type=F32)], axis=1)
        iw = iw_ref[g]
        score = jnp.zeros((n_new, l_pad), F32)
        for h in range(H_IDX):
            score = score + iw[:, h:h + 1] * jnp.maximum(d[h * n_new:(h + 1) * n_new, :], 0.0)
        scores.append(score)
    score = jnp.concatenate(scores, axis=0)

    rows = group * n_new
    row = lax.broadcasted_iota(I32, (rows, l_pad), 0) % n_new
    col = lax.broadcasted_iota(I32, (rows, l_pad), 1)
    adm = (col < l_all) & ((col // CHUNK) <= ((past + row) // CHUNK))
    sc = jnp.where(adm, score, -jnp.inf)

    def count(mask):
        return jnp.sum(jnp.where(mask, 1.0, 0.0), axis=1, keepdims=True)

    kf = float(k_top)
    ans_key = jnp.where(count(sc >= 0.0) >= kf, 0, INT_MIN).astype(I32)

    def bit_body(t, ans_key):
        cand_key = ans_key | lax.shift_left(jnp.int32(1), 30 - t)
        return jnp.where(count(sc >= _key_to_float(cand_key)) >= kf, cand_key, ans_key)

    ans_key = lax.fori_loop(0, 31, bit_body, ans_key)
    ans = jnp.where(ans_key > KEY_NEG_INF, _key_to_float(ans_key), -jnp.inf)
    eq = sc == ans
    need = kf - count(sc > ans)
    tie = (count(eq) > need) & (ans > -jnp.inf)
    flag = jnp.max(jnp.where(tie, 1.0, 0.0), axis=0, keepdims=True)
    nbits = int(l_pad - 1).bit_length()

    def tie_search(_):
        def tie_body(t, m):
            cand = m | lax.shift_left(jnp.int32(1), nbits - 1 - t)
            return jnp.where(count(eq & (col < cand)) < need, cand, m)
        m = lax.fori_loop(0, nbits, tie_body, jnp.zeros((rows, 1), I32))
        return jnp.where(tie, m, l_pad)

    thr = lax.cond(flag[0, 0] > 0.0, tie_search, lambda _: jnp.full((rows, 1), l_pad, I32), 0)
    sel = ((sc > ans) | (eq & (col <= thr))) & (sc > -jnp.inf)
    selb_all = jnp.where(sel, 0.0, NEG)

    for g in range(group):
        akv = akv_ref[g].astype(F32)
        ak_new = pad_rows(akv[:, 0:LANES])
        av_new = pad_rows(jnp.where(low, akv[:, LANES:2 * LANES], akv[:, 2 * LANES:3 * LANES]))
        selb = selb_all[g * n_new:(g + 1) * n_new, :]
        aq32 = aq_ref[g].astype(F32)
        slabs = [aq32[:, LANES * p:LANES * (p + 1)] for p in range(4)]
        qa = jnp.concatenate([jnp.where(low, s, 0.0) for s in slabs] + [jnp.where(low, 0.0, s) for s in slabs],
                             axis=0).astype(BF16)
        s = jnp.concatenate(
            [jnp.dot(qa, cakt_ref[g].astype(BF16), preferred_element_type=F32),
             lax.dot_general(qa, ak_new, NT_DIMS, preferred_element_type=F32)], axis=1)
        lg = s + bias_ref[...] + jnp.concatenate([selb] * H_A, axis=0)
        m = jnp.max(lg, axis=1, keepdims=True)
        p = jnp.exp2(lg - m)
        l = jnp.sum(p, axis=1, keepdims=True)
        pb = p.astype(BF16)
        o = (lax.dot_general(pb[:, 0:past], cavt_ref[g].astype(BF16), NT_DIMS, preferred_element_type=F32)
             + jnp.dot(pb[:, past:l_pad], av_new, preferred_element_type=F32)) / l
        for q in range(4):
            out_ref[g, :, LANES * q:LANES * (q + 1)] = jnp.where(
                low, o[q * n_new:(q + 1) * n_new, :], o[(q + 4) * n_new:(q + 5) * n_new, :]).astype(BF16)


SAMPLE_GROUP = 8


def _dsa_sample(aq, iq, iw, ik2, akv, cikt, cakt, cavt, bias_rows, past):
    b, n_new, _ = aq.shape
    l_pad = bias_rows.shape[1]
    assert past % LANES == 0 and l_pad == past + LANES and n_new <= LANES and b % SAMPLE_GROUP == 0
    k_top = min(K_TOP_MAX, (past + n_new) // 4)
    new = lambda c: pl.BlockSpec((SAMPLE_GROUP, n_new, c), lambda i: (i, 0, 0))
    old = lambda c: pl.BlockSpec((SAMPLE_GROUP, c, past), lambda i: (i, 0, 0))
    kern = functools.partial(_dsa_sample_kernel, past=past, n_new=n_new, l_pad=l_pad, k_top=k_top,
                             group=SAMPLE_GROUP)
    return pl.pallas_call(
        kern,
        grid=(b // SAMPLE_GROUP,),
        in_specs=[new(512), new(512), new(8), new(128), new(384), old(D_IDX), old(LANES), old(LANES),
                  _resident((H_A * n_new, l_pad))],
        out_specs=new(512),
        out_shape=jax.ShapeDtypeStruct((b, n_new, 512), BF16),
        compiler_params=_cparams("parallel"),
        name="dsa_sample",
    )(aq, iq, iw, ik2, akv, cikt, cakt, cavt, bias_rows)


def _fox_sample_kernel(bq_ref, bkn_ref, bvn_ref, ckt_ref, cvt_ref, fkt_ref, tot_ref, lf_ref, lft_ref, out_ref,
                       *, past, n_new):
    lf = lf_ref[0] * LOG2E
    lft = lft_ref[0] * LOG2E
    rown = lax.broadcasted_iota(I32, (n_new, H_B), 0)
    lanen = lax.broadcasted_iota(I32, (H_B, n_new), 1)
    fq = jnp.broadcast_to(tot_ref[0], (n_new, H_B))
    fqt = jnp.broadcast_to(fkt_ref[0][:, past - 1:past], (H_B, n_new))
    for s in range(n_new):
        fq = fq + jnp.where(rown >= s, lf[s:s + 1, :], 0.0)
        fqt = fqt + jnp.where(lanen >= s, lft[:, s:s + 1], 0.0)

    kct = ckt_ref[0].astype(BF16)
    vct = cvt_ref[0].astype(BF16)
    bq32 = bq_ref[0].astype(F32)
    head_of_lane = lax.broadcasted_iota(I32, (n_new, H_B * HEAD_DIM), 1) // HEAD_DIM
    qb = jnp.concatenate([jnp.where(head_of_lane == h, bq32, 0.0) for h in range(H_B)], axis=0).astype(BF16)
    s_past = jnp.dot(qb, kct, preferred_element_type=F32)
    s_new = lax.dot_general(qb, bkn_ref[0], NT_DIMS, preferred_element_type=F32)
    fkt = fkt_ref[0]
    fq_col = jnp.concatenate([fq[:, h:h + 1] for h in range(H_B)], axis=0)
    fk_past = jnp.concatenate([jnp.broadcast_to(fkt[h:h + 1, :], (n_new, past)) for h in range(H_B)], axis=0)
    fk_new = jnp.concatenate([jnp.broadcast_to(fqt[h:h + 1, :], (n_new, n_new)) for h in range(H_B)], axis=0)
    lg_past = s_past + (fq_col - fk_past)
    trow = lax.broadcasted_iota(I32, (H_B * n_new, n_new), 0) % n_new
    tcol = lax.broadcasted_iota(I32, (H_B * n_new, n_new), 1)
    lg_new = jnp.where(tcol <= trow, s_new + (fq_col - fk_new), NEG)
    m = jnp.maximum(jnp.max(lg_past, axis=1, keepdims=True), jnp.max(lg_new, axis=1, keepdims=True))
    p_past = jnp.exp2(lg_past - m)
    p_new = jnp.exp2(lg_new - m)
    l = jnp.sum(p_past, axis=1, keepdims=True) + jnp.sum(p_new, axis=1, keepdims=True)
    o = (lax.dot_general(p_past.astype(BF16), vct, NT_DIMS, preferred_element_type=F32)
         + jnp.dot(p_new.astype(BF16), bvn_ref[0].astype(BF16), preferred_element_type=F32)) / l
    out = jnp.zeros((n_new, H_B * HEAD_DIM), F32)
    for h in range(H_B):
        out = out + jnp.where(head_of_lane == h, o[h * n_new:(h + 1) * n_new, :], 0.0)
    out_ref[0] = out.astype(BF16)


def _fox_sample(bq, bkn, bvn, ckt, cvt, fkt, tot, lf, lft, past):
    b, n_new, _ = bq.shape
    new = lambda c: pl.BlockSpec((1, n_new, c), lambda i: (i, 0, 0))
    old = pl.BlockSpec((1, H_B * HEAD_DIM, past), lambda i: (i, 0, 0))
    kern = functools.partial(_fox_sample_kernel, past=past, n_new=n_new)
    return pl.pallas_call(
        kern,
        grid=(b,),
        in_specs=[new(512), new(512), new(512), old, old,
                  pl.BlockSpec((1, H_B, past), lambda i: (i, 0, 0)),
                  pl.BlockSpec((1, 1, H_B), lambda i: (i, 0, 0)),
                  new(H_B),
                  pl.BlockSpec((1, H_B, n_new), lambda i: (i, 0, 0))],
        out_specs=new(512),
        out_shape=jax.ShapeDtypeStruct((b, n_new, 512), BF16),
        compiler_params=_cparams("parallel"),
        name="fox_sample",
    )(bq, bkn, bvn, ckt, cvt, fkt, tot, lf, lft)


def _prep_weights(w_in, w_o, w_up, w_down):
    offs = np.concatenate([[0], np.cumsum(IN_SPLITS)])
    seg = lambda k: w_in[:, int(offs[k]):int(offs[k + 1])]
    a_q, a_k, a_v, i_q, i_k, i_w, b_q, b_k, b_v, f_z = (seg(k) for k in range(10))
    a_q_pairs = jnp.concatenate([a_q[:, HEAD_DIM * h:HEAD_DIM * (h + 1)] for h in AQ_HEAD_ORDER], axis=1)
    pad = jnp.zeros((w_in.shape[0], N_CAT - C_SM - 16), w_in.dtype)
    w_cat = jnp.concatenate([a_q_pairs, i_q, b_q, a_k, a_v, b_k, b_v, i_k, i_k, i_w, f_z, pad], axis=1)
    wo_a = jnp.concatenate([w_o[HEAD_DIM * h:HEAD_DIM * (h + 1)] for h in AQ_HEAD_ORDER], axis=0)
    wo_b = w_o[H_A * HEAD_DIM:]
    return (w_cat.astype(BF16), wo_a.astype(BF16), wo_b.astype(BF16), w_up.astype(BF16), w_down.astype(BF16))


def kernel(x_prompt, x_sample, cache_a_k, cache_a_v, cache_idx_k, cache_b_k, cache_b_v, cache_b_logf,
           w_in, w_o, b_f, rel_bias, g_attn, w_up, w_down, g_mlp, g_final):
    assert w_in.shape[0] == 1, "single-layer trunk"
    bp, tp, d = x_prompt.shape
    bs, ts, _ = x_sample.shape
    past = cache_a_k.shape[2]
    w_cat, wo_a, wo_b, wup, wdn = _prep_weights(w_in[0], w_o[0], w_up[0], w_down[0])
    g_a = g_attn[0].reshape(1, d)
    g_m = g_mlp[0].reshape(1, d)
    g_f = g_final.reshape(1, d)
    bf = b_f[0].reshape(H_B, 1)
    rel_bias = rel_bias.astype(F32)

    (aq, iq, bq, akv, bk, bv, ik2, ak32, av32, bk32, bv32, iw, ikt32, logft) = _inproj(x_prompt, g_a, w_cat, bf, 512)
    fkt = _cumsum_lanes(logft)
    r = np.arange(TQ)[:, None]
    c = np.arange(TK)[None, :]
    bucket = _t5_bucket_np(np.stack([c - TK - r, c - r]))
    far_bucket = int(_t5_bucket_np(np.array(-TK - 1)))
    assert far_bucket == int(_t5_bucket_np(np.array(-tp)))
    bias_tiles = _bias_tiles(rel_bias, jnp.asarray(bucket))
    out_a = _dsa_prompt(rel_bias, aq, iq, iw, ik2, akv, bias_tiles, far_bucket)
    out_b = _fox_prompt(bq, bk, bv, fkt)
    y_p = _tail(x_prompt.reshape(bp * tp, d), out_a.reshape(bp * tp, 512), out_b.reshape(bp * tp, 512),
                wo_a, wo_b, g_m, wup, wdn, g_f, 512).reshape(bp, tp, d)

    n_s = bs * ts
    outs = _inproj(x_sample.reshape(1, n_s, d), g_a, w_cat, bf, n_s)
    (aq_s, iq_s, bq_s, akv_s, ik2_s, ak32_s, av32_s, bk32_s, bv32_s, iw_s) = (
        outs[k].reshape(bs, ts, outs[k].shape[-1]) for k in (0, 1, 2, 3, 6, 7, 8, 9, 10, 11))
    bk_s = jnp.swapaxes(outs[4][0], 0, 1).reshape(bs, ts, H_B * HEAD_DIM)
    ik32_s = jnp.swapaxes(outs[12][0], 0, 1).reshape(bs, ts, D_IDX)
    logf_s = jnp.swapaxes(outs[13][0], 0, 1).reshape(bs, ts, H_B)
    l_all = past + ts
    l_pad = -(-l_all // LANES) * LANES
    rel_s = np.arange(l_pad)[None, :] - (past + np.arange(ts))[:, None]
    bias_s = _bias_tiles(rel_bias, jnp.asarray(_t5_bucket_np(rel_s))[None])[0].reshape(H_A * ts, l_pad)
    def feature_major(cache):
        c = cache[0]
        c = jnp.transpose(c, (0, 2, 3, 1)) if c.ndim == 4 else jnp.transpose(c, (0, 2, 1))
        return c.reshape(bs, -1, past)

    out_a_s = _dsa_sample(aq_s, iq_s, iw_s, ik2_s, akv_s, feature_major(cache_idx_k),
                          feature_major(cache_a_k), feature_major(cache_a_v), bias_s, past)
    fkt_c = _cumsum_lanes(jnp.swapaxes(cache_b_logf[0].astype(F32), 1, 2))
    tot = fkt_c[:, :, past - 1].reshape(bs, 1, H_B)
    out_b_s = _fox_sample(bq_s, bk_s, bv32_s, feature_major(cache_b_k), feature_major(cache_b_v), fkt_c, tot,
                          logf_s, jnp.swapaxes(logf_s, 1, 2), past)
    y_s = _tail(x_sample.reshape(n_s, d), out_a_s.reshape(n_s, 512), out_b_s.reshape(n_s, 512),
                wo_a, wo_b, g_m, wup, wdn, g_f, n_s).reshape(bs, ts, d)

    def rows(a, heads, b, t):
        return a.reshape(1, b, t, heads, HEAD_DIM)

    return (y_p, y_s,
            rows(ak32, HKV_A, bp, tp), rows(av32, HKV_A, bp, tp), jnp.swapaxes(ikt32, 1, 2)[None],
            rows(bk32, H_B, bp, tp), rows(bv32, H_B, bp, tp), jnp.swapaxes(logft, 1, 2)[None],
            rows(ak32_s, HKV_A, bs, ts), rows(av32_s, HKV_A, bs, ts), ik32_s.reshape(1, bs, ts, D_IDX),
            rows(bk32_s, H_B, bs, ts), rows(bv32_s, H_B, bs, ts), logf_s.reshape(1, bs, ts, H_B))
```
